```python
import math
import jax
import jax.numpy as jnp
from jax import lax
import numpy as np

D_MODEL = 1024
BATCH = 8
SEQ = 2048
DEPTH = 2

GRID_W = 64
CTX_LEN = 256
EPS = 1e-6
F32 = jnp.float32

S5_WIDTH = D_MODEL
S5_GROUP = 16
S5_GROUPS = S5_WIDTH // S5_GROUP
S5_STATE = 64

DN_HEAD_DIM = 128
DN_HEADS = D_MODEL // DN_HEAD_DIM
DN_WIDTH = DN_HEADS * DN_HEAD_DIM
DN_CONV = 3
DN_CHUNK = 64

CV_WIDTH = D_MODEL
CV_TAPS = 31

FFN_HIDDEN = ((8 * D_MODEL // 3 + 255) // 256) * 256
FFN_CONV = 3

N_BRANCH = 3
N_MOD = 6

COL_S5 = 0
COL_QKV = COL_S5 + S5_WIDTH
COL_BETA = COL_QKV + 3 * DN_WIDTH
COL_DECAY = COL_BETA + 2 * DN_HEADS
N_STATE_COLS = COL_DECAY + 2 * DN_HEADS
COL_Z = N_STATE_COLS
COL_CV = COL_Z + DN_WIDTH
COL_GATE = COL_CV + 2 * CV_WIDTH
N_IN_COLS = COL_GATE + N_BRANCH * D_MODEL

kernel_name = "hybrid_s5_deltanet_conformer_prefix_dit"


def rmsnorm(x, w):
    xf = x.astype(F32)
    xf = xf * lax.rsqrt(jnp.mean(xf * xf, axis=-1, keepdims=True) + EPS)
    return (xf * w.astype(F32)).astype(x.dtype)


def layernorm(x, w, b):
    xf = x.astype(F32)
    mu = jnp.mean(xf, axis=-1, keepdims=True)
    var = jnp.mean(jnp.square(xf - mu), axis=-1, keepdims=True)
    y = (xf - mu) * lax.rsqrt(var + EPS)
    return (y * w.astype(F32) + b.astype(F32)).astype(x.dtype)


def dwconv1d(x, w):
    k = w.shape[0]
    return lax.conv_general_dilated(
        x, w[:, None, :].astype(x.dtype), (1,), [(k // 2, k // 2)],
        dimension_numbers=("NWC", "WIO", "NWC"), feature_group_count=x.shape[-1])


def dwconv2d(x, w):
    kh, kw = w.shape[:2]
    return lax.conv_general_dilated(
        x, w[:, :, None, :].astype(x.dtype), (1, 1), [(kh // 2, kh // 2), (kw // 2, kw // 2)],
        dimension_numbers=("NHWC", "HWIO", "NHWC"), feature_group_count=x.shape[-1])


def l2norm(x):
    return x * lax.rsqrt(jnp.sum(x * x, axis=-1, keepdims=True) + EPS)


def s5_discretise(a_re, a_im, log_dt, b_re, b_im):
    lam = lax.complex(a_re.astype(F32), a_im.astype(F32))
    dt = jnp.exp(log_dt.astype(F32))[:, None]
    a_bar = jnp.exp(lam * dt)
    b = lax.complex(b_re.astype(F32), b_im.astype(F32))
    b_bar = ((a_bar - 1.0) / lam)[..., None] * b
    return a_bar, b_bar


def _linear_combine(e1, e2):
    a1, b1 = e1
    a2, b2 = e2
    return a1 * a2, a2 * b1 + b2


def s5_scan(a_bar, bu, h0, reverse):
    if h0 is not None:
        start = -1 if reverse else 0
        bu = bu.at[:, start].add(a_bar * h0)
    a = jnp.broadcast_to(a_bar, (1,) + bu.shape[1:])
    _, h = lax.associative_scan(_linear_combine, (a, bu), reverse=reverse, axis=1)
    return h


def s5_scans(u, p, init):
    bsz, n, _ = u.shape
    ug = u.astype(F32).reshape(bsz, n, S5_GROUPS, S5_GROUP).astype(jnp.complex64)
    hs = []
    for d in range(2):
        a_bar, b_bar = s5_discretise(p["s5_a_re"][d], p["s5_a_im"][d], p["s5_log_dt"][d],
                                     p["s5_b_re"][d], p["s5_b_im"][d])
        bu = jnp.einsum("blgc,gpc->blgp", ug, b_bar)
        hs.append(s5_scan(a_bar, bu, None if init is None else init[d], reverse=(d == 1)))
    return hs[0], hs[1]


def s5_readout(h, u, p):
    bsz, n, _ = u.shape
    y = (jnp.einsum("blgp,gcp->blgc", h.real, p["s5_c_re"].astype(F32))
         - jnp.einsum("blgp,gcp->blgc", h.imag, p["s5_c_im"].astype(F32)))
    y = y.reshape(bsz, n, S5_WIDTH) + p["s5_d"].astype(F32) * u.astype(F32)
    y = jax.nn.gelu(y).astype(u.dtype)
    return y * jax.nn.sigmoid(y @ p["s5_w_glu"])


def dn_inputs(qkv, beta_logits, decay_logits, p):
    bsz, n, _ = qkv.shape
    qkv = jax.nn.silu(dwconv1d(qkv, p["dn_conv_w"])).astype(F32)
    qkv = qkv.reshape(bsz, n, 3, DN_HEADS, DN_HEAD_DIM)
    q = l2norm(qkv[:, :, 0]) * (DN_HEAD_DIM ** -0.5)
    k = l2norm(qkv[:, :, 1])
    v = qkv[:, :, 2]
    beta = jax.nn.sigmoid(beta_logits.astype(F32)).reshape(bsz, n, 2, DN_HEADS)
    g = -jnp.exp(p["dn_a_log"].astype(F32)) * jax.nn.softplus(
        decay_logits.astype(F32).reshape(bsz, n, 2, DN_HEADS) + p["dn_dt_bias"].astype(F32))
    return q, k, v, beta, g


def gated_delta_chunked(q, k, v, beta, g, s0, with_output):
    bsz, n, h, dk = k.shape
    dv = v.shape[-1]
    nc = n // DN_CHUNK

    def chunks(t):
        return jnp.transpose(t.reshape(bsz, nc, DN_CHUNK, h, -1), (1, 0, 3, 2, 4))

    q, k, v = chunks(q), chunks(k), chunks(v)
    beta = chunks(beta[..., None])[..., 0]
    g = jnp.cumsum(chunks(g[..., None])[..., 0], axis=-1)
    pos = jnp.arange(DN_CHUNK)
    incl = pos[:, None] >= pos[None, :]
    strict = pos[:, None] > pos[None, :]
    decay = jnp.exp(jnp.where(incl, g[..., :, None] - g[..., None, :], -jnp.inf))
    kb = k * beta[..., None]
    m = jnp.where(strict, jnp.einsum("nbhid,nbhjd->nbhij", kb, k) * decay, 0.0)
    eye = jnp.eye(DN_CHUNK, dtype=F32)
    t = lax.linalg.triangular_solve(m + eye, jnp.broadcast_to(eye, m.shape),
                                    left_side=True, lower=True, unit_diagonal=True)
    u = t @ (v * beta[..., None])
    w = t @ (kb * jnp.exp(g)[..., None])
    g_last = g[..., -1:]
    k_tail = k * jnp.exp(g_last - g)[..., None]
    state_decay = jnp.exp(g_last)[..., None]
    s0 = jnp.zeros((bsz, h, dk, dv), F32) if s0 is None else s0

    def step(s, xs):
        u_i, w_i, kt_i, sd_i = xs[:4]
        v_new = u_i - w_i @ s
        s_new = s * sd_i + jnp.swapaxes(kt_i, -1, -2) @ v_new
        if not with_output:
            return s_new, None
        qg_i, a_i = xs[4:]
        return s_new, qg_i @ s + a_i @ v_new

    xs = (u, w, k_tail, state_decay)
    if with_output:
        xs = xs + (q * jnp.exp(g)[..., None], jnp.einsum("nbhid,nbhjd->nbhij", q, k) * decay)
    s_final, o = lax.scan(step, s0, xs)
    if not with_output:
        return None, s_final
    o = jnp.transpose(o, (1, 0, 3, 2, 4)).reshape(bsz, n, h, dv)
    return o, s_final


def dn_bidirectional(q, k, v, beta, g, init, with_output):
    s_f0, s_b0 = (None, None) if init is None else init
    o_f, s_f = gated_delta_chunked(q, k, v, beta[:, :, 0], g[:, :, 0], s_f0, with_output)
    flip = lambda t: jnp.flip(t, axis=1)
    o_b, s_b = gated_delta_chunked(flip(q), flip(k), flip(v), flip(beta[:, :, 1]),
                                   flip(g[:, :, 1]), s_b0, with_output)
    o = o_f + flip(o_b) if with_output else None
    return o, (s_f, s_b)


def dn_readout(o, z, p):
    bsz, n = o.shape[:2]
    o = o * lax.rsqrt(jnp.mean(o * o, axis=-1, keepdims=True) + EPS) * p["dn_norm_w"].astype(F32)
    o = o.reshape(bsz, n, DN_WIDTH) * jax.nn.silu(z.astype(F32))
    return o.astype(z.dtype)


def conv_module(glu_in, p):
    a, gate = jnp.split(glu_in, 2, axis=-1)
    y = a * jax.nn.sigmoid(gate)
    y = dwconv1d(y, p["cv_dw_w"]) + p["cv_dw_b"]
    return jax.nn.silu(layernorm(y, p["cv_ln_w"], p["cv_ln_b"]))


def recurrent_mixers(ps, p, init, with_output):
    u = ps[..., COL_S5:COL_QKV]
    s5_init, dn_init = (None, None) if init is None else init
    h_f, h_b = s5_scans(u, p, s5_init)
    s5_final = (h_f[:, -1], h_b[:, 0])
    q, k, v, beta, g = dn_inputs(ps[..., COL_QKV:COL_BETA], ps[..., COL_BETA:COL_DECAY],
                                 ps[..., COL_DECAY:N_STATE_COLS], p)
    o, dn_final = dn_bidirectional(q, k, v, beta, g, dn_init, with_output)
    s5_out = s5_readout(h_f + h_b, u, p) if with_output else None
    return s5_out, o, (s5_final, dn_final)


def token_mixer(h, p, init):
    proj = h @ p["w_in"]
    s5_out, o, final = recurrent_mixers(proj[..., :N_STATE_COLS], p, init, True)
    dn_out = dn_readout(o, proj[..., COL_Z:COL_CV], p)
    cv_out = conv_module(proj[..., COL_CV:COL_GATE], p)
    gates = jax.nn.sigmoid(proj[..., COL_GATE:].astype(F32)).astype(h.dtype)
    gates = gates.reshape(h.shape[:2] + (N_BRANCH, D_MODEL))
    merged = (gates[..., 0, :] * (s5_out @ p["w_br_s5"])
              + gates[..., 1, :] * (dn_out @ p["w_br_dn"])
              + gates[..., 2, :] * (cv_out @ p["w_br_cv"]))
    return merged @ p["w_out"], final


def context_states(hc, p):
    proj = hc @ p["w_in"][:, :N_STATE_COLS]
    _, _, final = recurrent_mixers(proj, p, None, False)
    return final


def conv_ffn(h, p, grid):
    a, v = jnp.split(h @ p["ffn_w_up"], 2, axis=-1)
    if grid:
        bsz, n, f = a.shape
        rows = n // GRID_W
        a = dwconv2d(a.reshape(bsz, rows, GRID_W, f), p["ffn_dw_w"]).reshape(bsz, n, f)
    else:
        a = dwconv1d(a, p["ffn_dw_w"][FFN_CONV // 2])
    return (jax.nn.silu(a + p["ffn_dw_b"]) * v) @ p["ffn_w_down"]


def modulation(cvec, p, n_chunks):
    m = jax.nn.silu(cvec) @ p["ada_w"][:, :n_chunks * D_MODEL] + p["ada_b"][:n_chunks * D_MODEL]
    return [m[:, None, i * D_MODEL:(i + 1) * D_MODEL] for i in range(n_chunks)]


def layer(x, xc, c, c_ctx, p, last):
    csh1, csc1, *ctx_rest = modulation(c_ctx[None, :], p, 2 if last else N_MOD)
    hc = rmsnorm(xc, p["norm1_w"]) * (1 + csc1) + csh1
    if last:
        ctx_state = context_states(hc, p)
    else:
        cg1, csh2, csc2, cg2 = ctx_rest
        yc, ctx_state = token_mixer(hc, p, None)
        xc = xc + cg1 * yc
        hc = rmsnorm(xc, p["norm2_w"]) * (1 + csc2) + csh2
        xc = xc + cg2 * conv_ffn(hc, p, grid=False)
    sh1, sc1, g1, sh2, sc2, g2 = modulation(c, p, N_MOD)
    h = rmsnorm(x, p["norm1_w"]) * (1 + sc1) + sh1
    y, _ = token_mixer(h, p, ctx_state)
    x = x + g1 * y
    h = rmsnorm(x, p["norm2_w"]) * (1 + sc2) + sh2
    x = x + g2 * conv_ffn(h, p, grid=True)
    return x, xc


def setup_inputs(seed: int = 0) -> dict:
    key = jax.random.key(seed)
    keys = iter(jax.random.split(key, 48))

    def nrm(shape, scale):
        return jax.random.normal(next(keys), shape, F32) * scale

    def unif(shape, lo, hi):
        return jax.random.uniform(next(keys), shape, F32, lo, hi)

    L, G, P = DEPTH, S5_GROUPS, S5_STATE
    n_idx = jnp.arange(P, dtype=F32)
    dn_dt = jnp.exp(unif((L, 2, DN_HEADS), math.log(1e-3), math.log(1e-1)))
    return {
        "x": nrm((BATCH, SEQ, D_MODEL), 1.0),
        "c": nrm((BATCH, D_MODEL), 1.0),
        "ctx": nrm((BATCH, CTX_LEN, D_MODEL), 1.0),
        "c_ctx": nrm((D_MODEL,), 1.0),
        "ada_w": nrm((L, D_MODEL, N_MOD * D_MODEL), 0.5 * D_MODEL ** -0.5),
        "ada_b": nrm((L, N_MOD * D_MODEL), 0.02),
        "norm1_w": 1.0 + nrm((L, D_MODEL), 0.02),
        "norm2_w": 1.0 + nrm((L, D_MODEL), 0.02),
        "w_in": nrm((L, D_MODEL, N_IN_COLS), D_MODEL ** -0.5),
        "s5_a_re": -0.5 + nrm((L, 2, G, P), 0.01),
        "s5_a_im": math.pi * n_idx + nrm((L, 2, G, P), 0.01),
        "s5_log_dt": unif((L, 2, G), math.log(1e-3), math.log(1e-1)),
        "s5_b_re": nrm((L, 2, G, P, S5_GROUP), (2 * S5_GROUP) ** -0.5),
        "s5_b_im": nrm((L, 2, G, P, S5_GROUP), (2 * S5_GROUP) ** -0.5),
        "s5_c_re": nrm((L, G, S5_GROUP, P), P ** -0.5),
        "s5_c_im": nrm((L, G, S5_GROUP, P), P ** -0.5),
        "s5_d": nrm((L, S5_WIDTH), 1.0),
        "s5_w_glu": nrm((L, S5_WIDTH, S5_WIDTH), S5_WIDTH ** -0.5),
        "dn_conv_w": nrm((L, DN_CONV, 3 * DN_WIDTH), DN_CONV ** -0.5),
        "dn_a_log": jnp.log(unif((L, 2, DN_HEADS), 1.0, 16.0)),
        "dn_dt_bias": dn_dt + jnp.log(-jnp.expm1(-dn_dt)),
        "dn_norm_w": 1.0 + nrm((L, DN_HEAD_DIM), 0.02),
        "cv_dw_w": nrm((L, CV_TAPS, CV_WIDTH), CV_TAPS ** -0.5),
        "cv_dw_b": nrm((L, CV_WIDTH), 0.02),
        "cv_ln_w": 1.0 + nrm((L, CV_WIDTH), 0.02),
        "cv_ln_b": nrm((L, CV_WIDTH), 0.02),
        "w_br_s5": nrm((L, S5_WIDTH, D_MODEL), S5_WIDTH ** -0.5),
        "w_br_dn": nrm((L, DN_WIDTH, D_MODEL), DN_WIDTH ** -0.5),
        "w_br_cv": nrm((L, CV_WIDTH, D_MODEL), CV_WIDTH ** -0.5),
        "w_out": nrm((L, D_MODEL, D_MODEL), D_MODEL ** -0.5),
        "ffn_w_up": nrm((L, D_MODEL, 2 * FFN_HIDDEN), D_MODEL ** -0.5),
        "ffn_dw_w": nrm((L, FFN_CONV, FFN_CONV, FFN_HIDDEN), 1.0 / FFN_CONV),
        "ffn_dw_b": nrm((L, FFN_HIDDEN), 0.02),
        "ffn_w_down": nrm((L, FFN_HIDDEN, D_MODEL), FFN_HIDDEN ** -0.5),
        "final_norm_w": 1.0 + nrm((D_MODEL,), 0.02),
    }


def reference(x, c, ctx, c_ctx, ada_w, ada_b, norm1_w, norm2_w, w_in,
              s5_a_re, s5_a_im, s5_log_dt, s5_b_re, s5_b_im, s5_c_re, s5_c_im, s5_d, s5_w_glu,
              dn_conv_w, dn_a_log, dn_dt_bias, dn_norm_w,
              cv_dw_w, cv_dw_b, cv_ln_w, cv_ln_b,
              w_br_s5, w_br_dn, w_br_cv, w_out,
              ffn_w_up, ffn_dw_w, ffn_dw_b, ffn_w_down, final_norm_w):
    xc = ctx
    for i in range(DEPTH):
        p = {
            "ada_w": ada_w[i], "ada_b": ada_b[i], "norm1_w": norm1_w[i], "norm2_w": norm2_w[i],
            "w_in": w_in[i],
            "s5_a_re": s5_a_re[i], "s5_a_im": s5_a_im[i], "s5_log_dt": s5_log_dt[i],
            "s5_b_re": s5_b_re[i], "s5_b_im": s5_b_im[i], "s5_c_re": s5_c_re[i], "s5_c_im": s5_c_im[i],
            "s5_d": s5_d[i], "s5_w_glu": s5_w_glu[i],
            "dn_conv_w": dn_conv_w[i], "dn_a_log": dn_a_log[i], "dn_dt_bias": dn_dt_bias[i],
            "dn_norm_w": dn_norm_w[i],
            "cv_dw_w": cv_dw_w[i], "cv_dw_b": cv_dw_b[i], "cv_ln_w": cv_ln_w[i], "cv_ln_b": cv_ln_b[i],
            "w_br_s5": w_br_s5[i], "w_br_dn": w_br_dn[i], "w_br_cv": w_br_cv[i], "w_out": w_out[i],
            "ffn_w_up": ffn_w_up[i], "ffn_dw_w": ffn_dw_w[i], "ffn_dw_b": ffn_dw_b[i],
            "ffn_w_down": ffn_w_down[i],
        }
        x, xc = layer(x, xc, c, c_ctx, p, last=(i == DEPTH - 1))
    return rmsnorm(x, final_norm_w)
```

```python
import functools
import math

import jax
import jax.numpy as jnp
from jax import lax
from jax.experimental import pallas as pl
from jax.experimental.pallas import tpu as pltpu

F32 = jnp.float32
BF16 = jnp.bfloat16
HIGHEST = lax.Precision.HIGHEST

EPS = 1e-6
GRID_W = 64
S5_GROUP = 16
DN_HEAD_DIM = 128
N_MOD = 6
N_BRANCH = 3

LANE = 128
SUBLANE = 8
S5_CHUNK = 16
DN_CHUNK = 64
MOD_ROWS = 16
VMEM_LIMIT = 56 * 1024 * 1024


def _cparams(sem, vmem=VMEM_LIMIT):
    return pltpu.CompilerParams(dimension_semantics=sem, vmem_limit_bytes=vmem)


def _dot(a, b, precision=None):
    return jnp.dot(a, b, preferred_element_type=F32, precision=precision)


def _dot_nt(a, b, precision=None):
    return lax.dot_general(a, b, (((1,), (1,)), ((), ())), preferred_element_type=F32,
                           precision=precision)


def _dot_tn(a, b, precision=None):
    return lax.dot_general(a, b, (((0,), (0,)), ((), ())), preferred_element_type=F32,
                           precision=precision)


def _silu(x):
    return x * jax.nn.sigmoid(x)


def _iota(shape, dim):
    return lax.broadcasted_iota(jnp.int32, shape, dim)


def _shifted_rows(ref, base, off, rows, cols):
    lo = (off // SUBLANE) * SUBLANE
    shift = off - lo
    start = pl.multiple_of(base + lo, SUBLANE)
    if shift == 0:
        return ref[pl.ds(start, rows), cols]
    return ref[pl.ds(start, rows + SUBLANE), cols][shift:shift + rows]


def _mod_kernel(c_ref, w_ref, b_ref, o_ref):
    o_ref[...] = _dot(_silu(c_ref[...]), w_ref[...], HIGHEST) + b_ref[...]


def _modulation(cmat, ada_w, ada_b):
    rows, d = cmat.shape
    n = ada_w.shape[1]
    tn = 1024
    return pl.pallas_call(
        _mod_kernel,
        grid=(n // tn,),
        in_specs=[pl.BlockSpec((rows, d), lambda j: (0, 0)),
                  pl.BlockSpec((d, tn), lambda j: (0, j)),
                  pl.BlockSpec((1, tn), lambda j: (0, j))],
        out_specs=pl.BlockSpec((rows, tn), lambda j: (0, j)),
        out_shape=jax.ShapeDtypeStruct((rows, n), F32),
        compiler_params=_cparams(("parallel",)),
        name="modulation",
    )(cmat, ada_w, ada_b.reshape(1, n))


def _norm_matmul_kernel(*refs, with_bd):
    if with_bd:
        x_ref, nw_ref, sh_ref, sc_ref, w_ref, wbd_ref, o_ref, bd_ref, h_ref = refs
    else:
        x_ref, nw_ref, sh_ref, sc_ref, w_ref, o_ref, h_ref = refs

    @pl.when(pl.program_id(1) == 0)
    def _():
        x = x_ref[...]
        h = x * lax.rsqrt(jnp.mean(x * x, axis=-1, keepdims=True) + EPS) * nw_ref[...]
        h = h * (1.0 + sc_ref[0]) + sh_ref[0]
        h_ref[...] = h.astype(BF16)
        if with_bd:
            bd_ref[...] = _dot(h, wbd_ref[...], HIGHEST)

    o_ref[...] = _dot(h_ref[...], w_ref[...]).astype(o_ref.dtype)


def _row_of_tile(tm, rows_per_batch, fixed_row):
    if fixed_row is not None:
        return lambda i: fixed_row
    return lambda i: (i * tm) // rows_per_batch


def _norm_matmul(x2, norm_w, mod3, sh_idx, sc_idx, w, *, rows_per_batch, fixed_row, wbd=None,
                 tn):
    t, d = x2.shape
    n = w.shape[1]
    tm = min(1024, t if fixed_row is not None else rows_per_batch)
    assert t % tm == 0 and n % tn == 0
    row = _row_of_tile(tm, rows_per_batch, fixed_row)
    with_bd = wbd is not None
    in_specs = [pl.BlockSpec((tm, d), lambda i, j: (i, 0)),
                pl.BlockSpec((1, d), lambda i, j: (0, 0)),
                pl.BlockSpec((None, 1, d), lambda i, j: (row(i), 0, sh_idx)),
                pl.BlockSpec((None, 1, d), lambda i, j: (row(i), 0, sc_idx)),
                pl.BlockSpec((d, tn), lambda i, j: (0, j))]
    args = [x2, norm_w.reshape(1, d), mod3, mod3, w]
    out_specs = [pl.BlockSpec((tm, tn), lambda i, j: (i, j))]
    out_shape = [jax.ShapeDtypeStruct((t, n), BF16)]
    if with_bd:
        in_specs.append(pl.BlockSpec((d, LANE), lambda i, j: (0, 0)))
        args.append(wbd)
        out_specs.append(pl.BlockSpec((tm, LANE), lambda i, j: (i, 0)))
        out_shape.append(jax.ShapeDtypeStruct((t, LANE), F32))
    res = pl.pallas_call(
        functools.partial(_norm_matmul_kernel, with_bd=with_bd),
        grid=(t // tm, n // tn),
        in_specs=in_specs,
        out_specs=out_specs,
        out_shape=out_shape,
        scratch_shapes=[pltpu.VMEM((tm, d), BF16)],
        compiler_params=_cparams(("parallel", "arbitrary")),
        name="norm_matmul",
    )(*args)
    return res if with_bd else res[0]


def _dn_gates_kernel(bd_ref, aneg_ref, dtb_ref, o_ref, *, n_heads):
    x = bd_ref[...]
    tm = x.shape[0]
    y = x + dtb_ref[...]
    g = aneg_ref[...] * (jnp.maximum(y, 0.0) + jnp.log1p(jnp.exp(-jnp.abs(y))))
    r = _iota((tm, tm), 0)
    c = _iota((tm, tm), 1)
    shift = DN_CHUNK.bit_length() - 1
    same = (r >> shift) == (c >> shift)
    prefix = _dot(jnp.where(same & (c <= r), 1.0, 0.0), g, HIGHEST)
    suffix = _dot(jnp.where(same & (c >= r), 1.0, 0.0), g, HIGHEST)
    lane = _iota(x.shape, 1)
    o_ref[...] = jnp.where(lane < 2 * n_heads, jax.nn.sigmoid(x),
                           jnp.where(lane < 3 * n_heads, prefix, suffix))


def _dn_gates(bd, aneg, dtb, n_heads):
    t = bd.shape[0]
    tm = min(512, t)
    assert t % tm == 0 and tm % DN_CHUNK == 0
    return pl.pallas_call(
        functools.partial(_dn_gates_kernel, n_heads=n_heads),
        grid=(t // tm,),
        in_specs=[pl.BlockSpec((tm, LANE), lambda i: (i, 0)),
                  pl.BlockSpec((1, LANE), lambda i: (0, 0)),
                  pl.BlockSpec((1, LANE), lambda i: (0, 0))],
        out_specs=pl.BlockSpec((tm, LANE), lambda i: (i, 0)),
        out_shape=jax.ShapeDtypeStruct((t, LANE), F32),
        compiler_params=_cparams(("parallel",)),
        name="dn_gates",
    )(bd, aneg, dtb)


def _s5_kernel(u_ref, wt_ref, win_ref, wst_ref, a_ref, h0_ref, y_ref, hfin_ref, s_ref, hp_ref,
               *, n_chunks, batch):
    half = LANE // 2
    u = u_ref[...].astype(F32)
    s_ref[...] = _dot(u, win_ref[...], HIGHEST)
    ar = a_ref[0:1, :]
    ai = a_ref[1:2, :]
    lane = _iota((batch, LANE), 1)
    is_fwd = lane < half

    def body(s, carry):
        hre, him = carry
        rf = pl.multiple_of(s * batch, batch)
        rb = pl.multiple_of((n_chunks - 1 - s) * batch, batch)
        hp_ref[pl.ds(rf, batch), 0:half] = hre[:, 0:half]
        hp_ref[pl.ds(rb, batch), half:LANE] = hre[:, half:LANE]
        hp_ref[pl.ds(rf, batch), LANE:LANE + half] = him[:, 0:half]
        hp_ref[pl.ds(rb, batch), LANE + half:2 * LANE] = him[:, half:LANE]
        sf = s_ref[pl.ds(rf, batch), :]
        sb = s_ref[pl.ds(rb, batch), :]
        sre = jnp.where(is_fwd, sf[:, 0:LANE], sb[:, 0:LANE])
        sim = jnp.where(is_fwd, sf[:, LANE:2 * LANE], sb[:, LANE:2 * LANE])
        return ar * hre - ai * him + sre, ar * him + ai * hre + sim

    h0 = h0_ref[...]
    hre, him = lax.fori_loop(0, n_chunks, body, (h0[:, 0:LANE], h0[:, LANE:2 * LANE]))
    hfin_ref[:, 0:LANE] = hre
    hfin_ref[:, LANE:2 * LANE] = him
    y_ref[...] = _dot(u, wt_ref[...], HIGHEST) + _dot(hp_ref[...], wst_ref[...], HIGHEST)


def _s5_core(ug, wt, win, wst, avec, h0, *, n_chunks, batch):
    g, rows, k = ug.shape
    assert batch == SUBLANE and k == 2 * LANE and rows == n_chunks * batch
    return pl.pallas_call(
        functools.partial(_s5_kernel, n_chunks=n_chunks, batch=batch),
        grid=(g,),
        in_specs=[pl.BlockSpec((None, rows, k), lambda i: (i, 0, 0)),
                  pl.BlockSpec((None, k, k), lambda i: (i, 0, 0)),
                  pl.BlockSpec((None, k, k), lambda i: (i, 0, 0)),
                  pl.BlockSpec((None, k, k), lambda i: (i, 0, 0)),
                  pl.BlockSpec((None, 2, LANE), lambda i: (i, 0, 0)),
                  pl.BlockSpec((None, batch, k), lambda i: (i, 0, 0))],
        out_specs=[pl.BlockSpec((None, rows, k), lambda i: (i, 0, 0)),
                   pl.BlockSpec((None, batch, k), lambda i: (i, 0, 0))],
        out_shape=[jax.ShapeDtypeStruct((g, rows, k), F32),
                   jax.ShapeDtypeStruct((g, batch, k), F32)],
        scratch_shapes=[pltpu.VMEM((rows, k), F32), pltpu.VMEM((rows, k), F32)],
        compiler_params=_cparams(("parallel",)),
        name="s5_core",
    )(ug, wt, win, wst, avec, h0)


def _s5_matrices(a_re, a_im, log_dt, b_re, b_im, c_re, c_im):
    tc = S5_CHUNK
    dt = jnp.exp(log_dt)[..., None]
    mag = jnp.exp(a_re * dt)
    abr, abi = mag * jnp.cos(a_im * dt), mag * jnp.sin(a_im * dt)
    den = a_re * a_re + a_im * a_im
    cr = ((abr - 1.0) * a_re + abi * a_im) / den
    ci = (abi * a_re - (abr - 1.0) * a_im) / den
    bbr = cr[..., None] * b_re - ci[..., None] * b_im
    bbi = cr[..., None] * b_im + ci[..., None] * b_re
    n = jnp.arange(tc + 1, dtype=F32)[:, None, None, None]
    pmag = jnp.exp(a_re * dt * n)
    pr, pi = pmag * jnp.cos(a_im * dt * n), pmag * jnp.sin(a_im * dt * n)
    car = c_re[None, None] * pr[:, :, :, None, :] - c_im[None, None] * pi[:, :, :, None, :]
    cai = c_re[None, None] * pi[:, :, :, None, :] + c_im[None, None] * pr[:, :, :, None, :]
    kern = (jnp.einsum("ndgcp,dgpe->ndgce", car[:tc], bbr, precision=HIGHEST)
            - jnp.einsum("ndgcp,dgpe->ndgce", cai[:tc], bbi, precision=HIGHEST))
    i_idx = jnp.arange(tc)[:, None]
    j_idx = jnp.arange(tc)[None, :]
    kf = jnp.where((j_idx >= i_idx)[:, :, None, None, None],
                   kern[jnp.clip(j_idx - i_idx, 0, tc - 1), 0], 0.0)
    kb = jnp.where((i_idx >= j_idx)[:, :, None, None, None],
                   kern[jnp.clip(i_idx - j_idx, 0, tc - 1), 1], 0.0)
    g_, c_ = c_re.shape[0], c_re.shape[1]
    wt = jnp.transpose(kf + kb, (2, 0, 4, 1, 3)).reshape(g_, tc * c_, tc * c_)
    ii = jnp.arange(tc)
    pf_r, pf_i = pr[tc - 1 - ii, 0], pi[tc - 1 - ii, 0]
    pb_r, pb_i = pr[ii, 1], pi[ii, 1]

    def in_block(p_r, p_i, d):
        re = p_r[..., None] * bbr[d][None] - p_i[..., None] * bbi[d][None]
        im = p_r[..., None] * bbi[d][None] + p_i[..., None] * bbr[d][None]
        to = lambda t_: jnp.transpose(t_, (1, 0, 3, 2)).reshape(g_, tc * c_, -1)
        return to(re), to(im)

    f_re, f_im = in_block(pf_r, pf_i, 0)
    b_re2, b_im2 = in_block(pb_r, pb_i, 1)
    win = jnp.concatenate([f_re, b_re2, f_im, b_im2], axis=-1)
    jj = jnp.arange(tc)
    to_rows = lambda t_: jnp.transpose(t_, (1, 3, 0, 2)).reshape(g_, -1, tc * c_)
    wst = jnp.concatenate([to_rows(car[jj + 1, 0]), to_rows(car[tc - jj, 1]),
                           to_rows(-cai[jj + 1, 0]), to_rows(-cai[tc - jj, 1])], axis=1)
    avec = jnp.stack([jnp.concatenate([pr[tc, 0], pr[tc, 1]], axis=-1),
                      jnp.concatenate([pi[tc, 0], pi[tc, 1]], axis=-1)], axis=1)
    return wt, win, wst, avec


def _s5_post_kernel(y_ref, u_ref, d_ref, w_ref, o_ref):
    y = y_ref[...] + d_ref[...] * u_ref[...].astype(F32)
    k0 = math.sqrt(2.0 / math.pi)
    g = 0.5 * y * (1.0 + jnp.tanh(k0 * (y + 0.044715 * (y * y * y))))
    o_ref[...] = (g * jax.nn.sigmoid(_dot(g.astype(BF16), w_ref[...]))).astype(o_ref.dtype)


def _s5_post(y, proj, s5_d, w_glu):
    t, d = y.shape
    tm = min(512, t)
    return pl.pallas_call(
        _s5_post_kernel,
        grid=(t // tm,),
        in_specs=[pl.BlockSpec((tm, d), lambda i: (i, 0)),
                  pl.BlockSpec((tm, d), lambda i: (i, 0)),
                  pl.BlockSpec((1, d), lambda i: (0, 0)),
                  pl.BlockSpec((d, d), lambda i: (0, 0))],
        out_specs=pl.BlockSpec((tm, d), lambda i: (i, 0)),
        out_shape=jax.ShapeDtypeStruct((t, d), BF16),
        compiler_params=_cparams(("parallel",)),
        name="s5_post",
    )(y, proj, s5_d.reshape(1, d), w_glu)


def _unit_tri_inverse(m, r, c):
    mm = lambda a, b: _dot(a, b, HIGHEST)
    eye = jnp.where(r == c, 1.0, 0.0)
    blk = lambda s: (r >> s) == (c >> s)
    md = jnp.where(blk(3), m, 0.0)
    m2 = mm(md, md)
    m4 = mm(m2, m2)
    t = eye - md
    t = t + mm(t, m2)
    t = t + mm(t, m4)
    for s in (3, 4, 5):
        mo = jnp.where(blk(s + 1) & jnp.logical_not(blk(s)), m, 0.0)
        t = t - mm(t, mm(mo, t))
    return t


def _dn_kernel(q_ref, k_ref, v_ref, z_ref, col_ref, cwq_ref, cwk_ref, cwv_ref, nw_ref, s0_ref,
               o_ref, sfin_ref, xp_ref, qs_ref, ks_ref, vs_ref, gb_ref, oacc_ref,
               *, seq, n_heads):
    head = pl.program_id(1)
    c_sz = DN_CHUNK
    pad = SUBLANE
    rt = min(256, seq)

    def prep(x_ref, cw_ref, dst_ref, normalise, scale):
        zeros = jnp.zeros((pad, LANE), F32)
        xp_ref[0:pad, :] = zeros
        xp_ref[seq + pad:seq + 2 * pad, :] = zeros
        xp_ref[pad:seq + pad, :] = x_ref[...].astype(F32)
        w = cw_ref[...]
        for t in range(seq // rt):
            r0 = t * rt
            acc = (w[0:1, :] * xp_ref[r0 + pad - 1:r0 + pad - 1 + rt, :]
                   + w[1:2, :] * xp_ref[r0 + pad:r0 + pad + rt, :]
                   + w[2:3, :] * xp_ref[r0 + pad + 1:r0 + pad + 1 + rt, :])
            y = _silu(acc)
            if normalise:
                y = y * lax.rsqrt(jnp.sum(y * y, axis=-1, keepdims=True) + EPS)
            if scale != 1.0:
                y = y * scale
            dst_ref[r0:r0 + rt, :] = y

    prep(q_ref, cwq_ref, qs_ref, True, DN_HEAD_DIM ** -0.5)
    prep(k_ref, cwk_ref, ks_ref, True, 1.0)
    prep(v_ref, cwv_ref, vs_ref, False, 1.0)

    for t in range(seq // rt):
        r0 = t * rt
        col = col_ref[r0:r0 + rt, :]
        lane = _iota(col.shape, 1)
        for idx in range(4):
            pick = jnp.sum(jnp.where(lane == head + idx * n_heads, col, 0.0), axis=-1,
                           keepdims=True)
            gb_ref[idx, r0:r0 + rt, :] = jnp.broadcast_to(pick, (rt, LANE))

    r = _iota((c_sz, c_sz), 0)
    c = _iota((c_sz, c_sz), 1)

    def chunk_step(row0, kc, qc, vc, bc, gc, grow, s, lower, accumulate):
        incl = (r >= c) if lower else (r <= c)
        strict = (r > c) if lower else (r < c)
        diff = gc[:, 0:c_sz] - grow
        decay = jnp.where(incl, jnp.exp(jnp.minimum(diff, 0.0)), 0.0)
        kb16 = kc.astype(BF16)
        kkt = _dot_nt(kb16, kb16)
        qkt = _dot_nt(qc.astype(BF16), kb16)
        m = jnp.where(strict, bc[:, 0:c_sz] * kkt * decay, 0.0)
        tinv = _unit_tri_inverse(m, r, c)
        eg = jnp.exp(gc)
        kbeta = kc * bc
        rhs = jnp.concatenate([vc * bc, kbeta * eg], axis=1).astype(BF16)
        uw = _dot(tinv.astype(BF16), rhs)
        s16 = s.astype(BF16)
        v_new = uw[:, 0:LANE] - _dot(uw[:, LANE:2 * LANE].astype(BF16), s16)
        vn16 = v_new.astype(BF16)
        o = _dot((qc * eg).astype(BF16), s16) + _dot((qkt * decay).astype(BF16), vn16)
        g_last = gc[c_sz - 1:c_sz, :] if lower else gc[0:1, :]
        k_tail = kc * jnp.exp(g_last - gc)
        s_new = s * jnp.exp(g_last) + _dot_tn(k_tail.astype(BF16), vn16)
        if accumulate:
            oacc_ref[pl.ds(row0, c_sz), :] += o
        else:
            oacc_ref[pl.ds(row0, c_sz), :] = o
        return s_new

    n_pairs = seq // (2 * c_sz)

    def run_direction(direction, s_init):
        lower = direction == 0

        def pair_body(it, s):
            p = it if lower else n_pairs - 1 - it
            r0 = pl.multiple_of(p * 2 * c_sz, 2 * c_sz)
            k2 = ks_ref[pl.ds(r0, 2 * c_sz), :]
            q2 = qs_ref[pl.ds(r0, 2 * c_sz), :]
            v2 = vs_ref[pl.ds(r0, 2 * c_sz), :]
            b2 = gb_ref[direction, pl.ds(r0, 2 * c_sz), :]
            g2 = gb_ref[2 + direction, pl.ds(r0, 2 * c_sz), :]
            g2t = g2.T
            for sub in ((0, 1) if lower else (1, 0)):
                sl = slice(sub * c_sz, (sub + 1) * c_sz)
                s = chunk_step(r0 + sub * c_sz, k2[sl], q2[sl], v2[sl], b2[sl], g2[sl],
                               g2t[0:c_sz, sl], s, lower, not lower)
            return s

        return lax.fori_loop(0, n_pairs, pair_body, s_init)

    sfin_ref[0, 0] = run_direction(0, s0_ref[0, 0])
    sfin_ref[1, 0] = run_direction(1, s0_ref[1, 0])

    for t in range(seq // rt):
        r0 = t * rt
        o = oacc_ref[r0:r0 + rt, :]
        o = o * lax.rsqrt(jnp.mean(o * o, axis=-1, keepdims=True) + EPS) * nw_ref[...]
        o_ref[r0:r0 + rt, :] = (o * _silu(z_ref[r0:r0 + rt, :].astype(F32))).astype(o_ref.dtype)


def _deltanet(proj, col, conv_w, norm_w, s0, *, batch, seq, n_heads, col_q, col_z):
    t = proj.shape[0]
    d = n_heads * DN_HEAD_DIM
    hd = DN_HEAD_DIM
    blk = lambda off: pl.BlockSpec((seq, hd), lambda b, h: (b, off + h))
    cw = lambda off: pl.BlockSpec((3, hd), lambda b, h: (0, off + h))
    st = pl.BlockSpec((None, 2, 1, hd, hd), lambda b, h: (b, 0, h, 0, 0))
    return pl.pallas_call(
        functools.partial(_dn_kernel, seq=seq, n_heads=n_heads),
        grid=(batch, n_heads),
        in_specs=[blk(col_q), blk(col_q + n_heads), blk(col_q + 2 * n_heads), blk(col_z),
                  pl.BlockSpec((seq, LANE), lambda b, h: (b, 0)),
                  cw(0), cw(n_heads), cw(2 * n_heads),
                  pl.BlockSpec((1, hd), lambda b, h: (0, 0)),
                  st],
        out_specs=[pl.BlockSpec((seq, hd), lambda b, h: (b, h)), st],
        out_shape=[jax.ShapeDtypeStruct((t, d), BF16),
                   jax.ShapeDtypeStruct(s0.shape, F32)],
        scratch_shapes=[pltpu.VMEM((seq + 2 * SUBLANE, hd), F32),
                        pltpu.VMEM((seq, hd), F32), pltpu.VMEM((seq, hd), F32),
                        pltpu.VMEM((seq, hd), F32),
                        pltpu.VMEM((4, seq, LANE), F32),
                        pltpu.VMEM((seq, hd), F32)],
        compiler_params=_cparams(("parallel", "parallel")),
        name="deltanet",
    )(proj, proj, proj, proj, col, conv_w, conv_w, conv_w, norm_w.reshape(1, hd), s0)


def _conv_module_kernel(a_ref, g_ref, w_ref, b_ref, lnw_ref, lnb_ref, o_ref, yp_ref, tmp_ref,
                        *, seq, taps):
    d = a_ref.shape[1]
    half = taps // 2
    pad = 2 * SUBLANE
    assert half < pad
    rt = 64
    ct = 256
    zeros = jnp.zeros((pad, d), F32)
    yp_ref[0:pad, :] = zeros
    yp_ref[seq + pad:seq + 2 * pad, :] = zeros

    def fill(t, carry):
        r0 = pl.multiple_of(t * rt, rt)
        a = a_ref[pl.ds(r0, rt), :].astype(F32)
        g = g_ref[pl.ds(r0, rt), :].astype(F32)
        yp_ref[pl.ds(r0 + pad, rt), :] = a * jax.nn.sigmoid(g)
        return carry

    lax.fori_loop(0, seq // rt, fill, 0)

    def tile(t, carry):
        r0 = pl.multiple_of(t * rt, rt)
        for cc in range(d // ct):
            cs = slice(cc * ct, (cc + 1) * ct)
            acc = jnp.zeros((rt, ct), F32) + b_ref[:, cs]
            for j in range(taps):
                acc = acc + w_ref[j:j + 1, cs] * _shifted_rows(yp_ref, r0 + pad, j - half, rt, cs)
            tmp_ref[:, cs] = acc
        y = tmp_ref[...]
        mu = jnp.mean(y, axis=-1, keepdims=True)
        yc = y - mu
        var = jnp.mean(yc * yc, axis=-1, keepdims=True)
        y = yc * lax.rsqrt(var + EPS) * lnw_ref[...] + lnb_ref[...]
        o_ref[pl.ds(r0, rt), :] = _silu(y).astype(o_ref.dtype)
        return carry

    lax.fori_loop(0, seq // rt, tile, 0)


def _conv_module(proj, w, b, lnw, lnb, *, batch, seq, col_a):
    t = proj.shape[0]
    taps, d = w.shape
    return pl.pallas_call(
        functools.partial(_conv_module_kernel, seq=seq, taps=taps),
        grid=(batch,),
        in_specs=[pl.BlockSpec((seq, d), lambda i: (i, col_a)),
                  pl.BlockSpec((seq, d), lambda i: (i, col_a + 1)),
                  pl.BlockSpec((taps, d), lambda i: (0, 0)),
                  pl.BlockSpec((1, d), lambda i: (0, 0)),
                  pl.BlockSpec((1, d), lambda i: (0, 0)),
                  pl.BlockSpec((1, d), lambda i: (0, 0))],
        out_specs=pl.BlockSpec((seq, d), lambda i: (i, 0)),
        out_shape=jax.ShapeDtypeStruct((t, d), BF16),
        scratch_shapes=[pltpu.VMEM((seq + 4 * SUBLANE, d), F32), pltpu.VMEM((64, d), F32)],
        compiler_params=_cparams(("parallel",)),
        name="conv_module",
    )(proj, proj, w, b.reshape(1, d), lnw.reshape(1, d), lnb.reshape(1, d))


def _merge_kernel(s5_ref, dn_ref, cv_ref, g0_ref, g1_ref, g2_ref, w0_ref, w1_ref, w2_ref,
                  wo_ref, x_ref, gate_ref, o_ref):
    merged = (jax.nn.sigmoid(g0_ref[...].astype(F32)) * _dot(s5_ref[...], w0_ref[...])
              + jax.nn.sigmoid(g1_ref[...].astype(F32)) * _dot(dn_ref[...], w1_ref[...])
              + jax.nn.sigmoid(g2_ref[...].astype(F32)) * _dot(cv_ref[...], w2_ref[...]))
    y = _dot(merged.astype(BF16), wo_ref[...])
    o_ref[...] = x_ref[...] + gate_ref[0] * y


def _merge(s5_out, dn_out, cv_out, proj, w_s5, w_dn, w_cv, w_out, x2, mod3, gate_idx,
           *, rows_per_batch, fixed_row, col_gate):
    t, d = x2.shape
    tm = min(512, t if fixed_row is not None else rows_per_batch)
    assert t % tm == 0
    row = _row_of_tile(tm, rows_per_batch, fixed_row)
    act = pl.BlockSpec((tm, d), lambda i: (i, 0))
    gat = lambda k: pl.BlockSpec((tm, d), lambda i: (i, col_gate + k))
    wsp = pl.BlockSpec((d, d), lambda i: (0, 0))
    return pl.pallas_call(
        _merge_kernel,
        grid=(t // tm,),
        in_specs=[act, act, act, gat(0), gat(1), gat(2), wsp, wsp, wsp, wsp, act,
                  pl.BlockSpec((None, 1, d), lambda i: (row(i), 0, gate_idx))],
        out_specs=act,
        out_shape=jax.ShapeDtypeStruct((t, d), F32),
        compiler_params=_cparams(("parallel",)),
        name="merge",
    )(s5_out, dn_out, cv_out, proj, proj, proj, w_s5, w_dn, w_cv, w_out, x2, mod3)


def _ffn_conv_kernel(a_ref, v_ref, w_ref, b_ref, o_ref, a0_ref, al_ref, ar_ref, *, seq, grid2d):
    ct = a_ref.shape[1]
    pad = 72 if grid2d else SUBLANE
    rt = 128
    zeros = jnp.zeros((pad, ct), F32)
    bufs = (a0_ref, al_ref, ar_ref) if grid2d else (a0_ref,)
    for buf in bufs:
        buf[0:pad, :] = zeros
        buf[seq + pad:seq + 2 * pad, :] = zeros

    def fill(t, carry):
        r0 = pl.multiple_of(t * rt, rt)
        a = a_ref[pl.ds(r0, rt), :].astype(F32)
        a0_ref[pl.ds(r0 + pad, rt), :] = a
        if grid2d:
            colpos = (_iota((rt, ct), 0) + r0) & (GRID_W - 1)
            al_ref[pl.ds(r0 + pad, rt), :] = jnp.where(colpos == GRID_W - 1, 0.0, a)
            ar_ref[pl.ds(r0 + pad, rt), :] = jnp.where(colpos == 0, 0.0, a)
        return carry

    lax.fori_loop(0, seq // rt, fill, 0)

    if grid2d:
        taps = [(dr, dc) for dr in (-1, 0, 1) for dc in (-1, 0, 1)]
    else:
        taps = [(0, dc) for dc in (-1, 0, 1)]
    src = {-1: al_ref, 0: a0_ref, 1: ar_ref} if grid2d else {-1: a0_ref, 0: a0_ref, 1: a0_ref}

    def tile(t, carry):
        r0 = pl.multiple_of(t * rt, rt)
        acc = jnp.zeros((rt, ct), F32) + b_ref[...]
        for dr, dc in taps:
            widx = (dr + 1) * 3 + (dc + 1)
            acc = acc + w_ref[widx:widx + 1, :] * _shifted_rows(
                src[dc], r0 + pad, dr * GRID_W + dc, rt, slice(None))
        o_ref[pl.ds(r0, rt), :] = (_silu(acc) * v_ref[pl.ds(r0, rt), :].astype(F32)).astype(o_ref.dtype)
        return carry

    lax.fori_loop(0, seq // rt, tile, 0)


def _ffn_conv(av, w9, bias, *, batch, seq, grid2d):
    t = av.shape[0]
    f = w9.shape[1]
    ct = 256
    assert f % ct == 0 and seq % 128 == 0
    nct = f // ct
    pad = 72 if grid2d else SUBLANE
    return pl.pallas_call(
        functools.partial(_ffn_conv_kernel, seq=seq, grid2d=grid2d),
        grid=(batch, nct),
        in_specs=[pl.BlockSpec((seq, ct), lambda b, c: (b, c)),
                  pl.BlockSpec((seq, ct), lambda b, c: (b, nct + c)),
                  pl.BlockSpec((9, ct), lambda b, c: (0, c)),
                  pl.BlockSpec((1, ct), lambda b, c: (0, c))],
        out_specs=pl.BlockSpec((seq, ct), lambda b, c: (b, c)),
        out_shape=jax.ShapeDtypeStruct((t, f), BF16),
        scratch_shapes=[pltpu.VMEM((seq + 2 * pad, ct), F32)] * 3,
        compiler_params=_cparams(("parallel", "parallel")),
        name="ffn_conv",
    )(av, av, w9, bias.reshape(1, f))


def _resid_matmul_kernel(*refs, final_norm):
    if final_norm:
        a_ref, w_ref, x_ref, gate_ref, fw_ref, o_ref = refs
    else:
        a_ref, w_ref, x_ref, gate_ref, o_ref = refs
    y = x_ref[...] + gate_ref[0] * _dot(a_ref[...], w_ref[...])
    if final_norm:
        y = y * lax.rsqrt(jnp.mean(y * y, axis=-1, keepdims=True) + EPS) * fw_ref[...]
    o_ref[...] = y


def _resid_matmul(a, w, x2, mod3, gate_idx, *, rows_per_batch, fixed_row, final_w=None):
    t, d = x2.shape
    k = a.shape[1]
    tm = min(512, t if fixed_row is not None else rows_per_batch)
    assert t % tm == 0
    row = _row_of_tile(tm, rows_per_batch, fixed_row)
    final_norm = final_w is not None
    in_specs = [pl.BlockSpec((tm, k), lambda i: (i, 0)),
                pl.BlockSpec((k, d), lambda i: (0, 0)),
                pl.BlockSpec((tm, d), lambda i: (i, 0)),
                pl.BlockSpec((None, 1, d), lambda i: (row(i), 0, gate_idx))]
    args = [a, w, x2, mod3]
    if final_norm:
        in_specs.append(pl.BlockSpec((1, d), lambda i: (0, 0)))
        args.append(final_w.reshape(1, d))
    return pl.pallas_call(
        functools.partial(_resid_matmul_kernel, final_norm=final_norm),
        grid=(t // tm,),
        in_specs=in_specs,
        out_specs=pl.BlockSpec((tm, d), lambda i: (i, 0)),
        out_shape=jax.ShapeDtypeStruct((t, d), F32),
        compiler_params=_cparams(("parallel",)),
        name="resid_matmul",
    )(*args)


def _group_major(u2, batch, seq, groups):
    nk = seq // S5_CHUNK
    u5 = u2.reshape(batch, nk, S5_CHUNK, groups, S5_GROUP)
    return jnp.transpose(u5, (3, 1, 0, 2, 4)).reshape(groups, nk * batch, S5_CHUNK * S5_GROUP)


def _token_major(yg, batch, seq, groups):
    nk = seq // S5_CHUNK
    y5 = yg.reshape(groups, nk, batch, S5_CHUNK, S5_GROUP)
    return jnp.transpose(y5, (2, 1, 3, 0, 4)).reshape(batch * seq, groups * S5_GROUP)


def kernel(x, c, ctx, c_ctx, ada_w, ada_b, norm1_w, norm2_w, w_in, s5_a_re, s5_a_im, s5_log_dt, s5_b_re, s5_b_im, s5_c_re, s5_c_im, s5_d, s5_w_glu, dn_conv_w, dn_a_log, dn_dt_bias, dn_norm_w, cv_dw_w, cv_dw_b, cv_ln_w, cv_ln_b, w_br_s5, w_br_dn, w_br_cv, w_out, ffn_w_up, ffn_dw_w, ffn_dw_b, ffn_w_down, final_norm_w):
    batch, seq, d = x.shape
    ctx_len = ctx.shape[1]
    depth = ada_w.shape[0]
    groups = s5_a_re.shape[2]
    n_heads = dn_a_log.shape[2]
    dn_width = n_heads * DN_HEAD_DIM
    ffn_hidden = ffn_dw_b.shape[1]
    assert batch == SUBLANE and batch < MOD_ROWS and d == groups * S5_GROUP == dn_width
    assert d % 1024 == 0 and 4 * n_heads <= LANE

    col_qkv = d
    col_beta = col_qkv + 3 * dn_width
    n_state = col_beta + 4 * n_heads
    col_cv = n_state + dn_width
    col_gate = col_cv + 2 * d
    pk_q = d // LANE
    pk_z = (d + 3 * dn_width) // LANE
    pk_cv = (d + 4 * dn_width) // d
    pk_gate = pk_cv + 2

    cmat = jnp.zeros((MOD_ROWS, d), F32).at[:batch].set(c).at[batch].set(c_ctx)
    xl = x.reshape(batch * seq, d)
    xc = ctx.reshape(batch * ctx_len, d)

    for i in range(depth):
        last = i == depth - 1
        w_main = jnp.concatenate([w_in[i][:, :col_beta], w_in[i][:, n_state:]], axis=1).astype(BF16)
        w_bd = jnp.pad(w_in[i][:, col_beta:n_state], ((0, 0), (0, LANE - 4 * n_heads)))
        aneg = jnp.pad(-jnp.exp(dn_a_log[i].reshape(1, -1)), ((0, 0), (2 * n_heads, LANE - 4 * n_heads)))
        dtb = jnp.pad(dn_dt_bias[i].reshape(1, -1), ((0, 0), (2 * n_heads, LANE - 4 * n_heads)))
        wt, win, wst, avec = _s5_matrices(s5_a_re[i], s5_a_im[i], s5_log_dt[i], s5_b_re[i],
                                          s5_b_im[i], s5_c_re[i], s5_c_im[i])
        w_glu = s5_w_glu[i].astype(BF16)
        w_brs = w_br_s5[i].astype(BF16)
        w_brd = w_br_dn[i].astype(BF16)
        w_brc = w_br_cv[i].astype(BF16)
        w_o = w_out[i].astype(BF16)
        w_up = ffn_w_up[i].astype(BF16)
        w_dn = ffn_w_down[i].astype(BF16)
        w9 = ffn_dw_w[i].reshape(9, ffn_hidden)

        mod3 = _modulation(cmat, ada_w[i], ada_b[i]).reshape(MOD_ROWS, 1, N_MOD * d)

        def stream(x2, length, fixed_row, s5_init, dn_init, grid2d, states_only, final_w):
            rows = dict(rows_per_batch=length, fixed_row=fixed_row)
            proj, bd = _norm_matmul(x2, norm1_w[i], mod3, 0, 1, w_main, wbd=w_bd, tn=1024, **rows)
            col = _dn_gates(bd, aneg, dtb, n_heads)
            ug = _group_major(proj[:, :d], batch, length, groups)
            yg, s5_fin = _s5_core(ug, wt, win, wst, avec, s5_init,
                                  n_chunks=length // S5_CHUNK, batch=batch)
            dn_out, dn_fin = _deltanet(proj, col, dn_conv_w[i], dn_norm_w[i], dn_init,
                                       batch=batch, seq=length, n_heads=n_heads,
                                       col_q=pk_q, col_z=pk_z)
            if states_only:
                return None, (s5_fin, dn_fin)
            s5_out = _s5_post(_token_major(yg, batch, length, groups), proj, s5_d[i], w_glu)
            cv_out = _conv_module(proj, cv_dw_w[i], cv_dw_b[i], cv_ln_w[i], cv_ln_b[i],
                                  batch=batch, seq=length, col_a=pk_cv)
            x2 = _merge(s5_out, dn_out, cv_out, proj, w_brs, w_brd, w_brc, w_o, x2, mod3, 2,
                        col_gate=pk_gate, **rows)
            av = _norm_matmul(x2, norm2_w[i], mod3, 3, 4, w_up, tn=w_up.shape[1] // 4, **rows)
            hidden = _ffn_conv(av, w9, ffn_dw_b[i], batch=batch, seq=length, grid2d=grid2d)
            x2 = _resid_matmul(hidden, w_dn, x2, mod3, 5, final_w=final_w, **rows)
            return x2, (s5_fin, dn_fin)

        zero_s5 = jnp.zeros((groups, batch, 2 * LANE), F32)
        zero_dn = jnp.zeros((batch, 2, n_heads, DN_HEAD_DIM, DN_HEAD_DIM), F32)
        xc, (s5_state, dn_state) = stream(xc, ctx_len, batch, zero_s5, zero_dn, False, last, None)
        xl, _ = stream(xl, seq, None, s5_state, dn_state, True, False,
                       final_norm_w if last else None)

    return xl.reshape(batch, seq, d)
```

```python
import functools
import math

import jax
import jax.numpy as jnp
from jax import lax
from jax.experimental import pallas as pl
from jax.experimental.pallas import tpu as pltpu

F32 = jnp.float32
BF16 = jnp.bfloat16
HIGHEST = lax.Precision.HIGHEST

EPS = 1e-6
GRID_W = 64
S5_GROUP = 16
DN_HEAD_DIM = 128
N_MOD = 6
N_BRANCH = 3

LANE = 128
SUBLANE = 8
S5_CHUNK = 16
DN_CHUNK = 64
MOD_ROWS = 16
VMEM_LIMIT = 56 * 1024 * 1024


def _cparams(sem, vmem=VMEM_LIMIT):
    return pltpu.CompilerParams(dimension_semantics=sem, vmem_limit_bytes=vmem)


def _dot(a, b, precision=None):
    return jnp.dot(a, b, preferred_element_type=F32, precision=precision)


def _dot_nt(a, b, precision=None):
    return lax.dot_general(a, b, (((1,), (1,)), ((), ())), preferred_element_type=F32,
                           precision=precision)


def _dot_tn(a, b, precision=None):
    return lax.dot_general(a, b, (((0,), (0,)), ((), ())), preferred_element_type=F32,
                           precision=precision)


def _silu(x):
    return x * jax.nn.sigmoid(x)


def _iota(shape, dim):
    return lax.broadcasted_iota(jnp.int32, shape, dim)


def _shifted_rows(ref, base, off, rows, cols):
    lo = (off // SUBLANE) * SUBLANE
    shift = off - lo
    start = pl.multiple_of(base + lo, SUBLANE)
    if shift == 0:
        return ref[pl.ds(start, rows), cols]
    return ref[pl.ds(start, rows + SUBLANE), cols][shift:shift + rows]


def _mod_kernel(c_ref, w_ref, b_ref, o_ref):
    o_ref[...] = _dot(_silu(c_ref[...]), w_ref[...], HIGHEST) + b_ref[...]


def _modulation(cmat, ada_w, ada_b):
    rows, d = cmat.shape
    n = ada_w.shape[1]
    tn = 1024
    return pl.pallas_call(
        _mod_kernel,
        grid=(n // tn,),
        in_specs=[pl.BlockSpec((rows, d), lambda j: (0, 0)),
                  pl.BlockSpec((d, tn), lambda j: (0, j)),
                  pl.BlockSpec((1, tn), lambda j: (0, j))],
        out_specs=pl.BlockSpec((rows, tn), lambda j: (0, j)),
        out_shape=jax.ShapeDtypeStruct((rows, n), F32),
        compiler_params=_cparams(("parallel",)),
        name="modulation",
    )(cmat, ada_w, ada_b.reshape(1, n))


def _norm_matmul_kernel(*refs, with_bd):
    if with_bd:
        x_ref, nw_ref, sh_ref, sc_ref, w_ref, wbd_ref, o_ref, bd_ref, h_ref = refs
    else:
        x_ref, nw_ref, sh_ref, sc_ref, w_ref, o_ref, h_ref = refs

    @pl.when(pl.program_id(1) == 0)
    def _():
        x = x_ref[...]
        h = x * lax.rsqrt(jnp.mean(x * x, axis=-1, keepdims=True) + EPS) * nw_ref[...]
        h = h * (1.0 + sc_ref[0]) + sh_ref[0]
        h_ref[...] = h.astype(BF16)
        if with_bd:
            bd_ref[...] = _dot(h, wbd_ref[...], HIGHEST)

    o_ref[...] = _dot(h_ref[...], w_ref[...]).astype(o_ref.dtype)


def _row_of_tile(tm, rows_per_batch, fixed_row):
    if fixed_row is not None:
        return lambda i: fixed_row
    return lambda i: (i * tm) // rows_per_batch


def _norm_matmul(x2, norm_w, mod3, sh_idx, sc_idx, w, *, rows_per_batch, fixed_row, wbd=None,
                 tn):
    t, d = x2.shape
    n = w.shape[1]
    tm = min(1024, t if fixed_row is not None else rows_per_batch)
    assert t % tm == 0 and n % tn == 0
    row = _row_of_tile(tm, rows_per_batch, fixed_row)
    with_bd = wbd is not None
    in_specs = [pl.BlockSpec((tm, d), lambda i, j: (i, 0)),
                pl.BlockSpec((1, d), lambda i, j: (0, 0)),
                pl.BlockSpec((None, 1, d), lambda i, j: (row(i), 0, sh_idx)),
                pl.BlockSpec((None, 1, d), lambda i, j: (row(i), 0, sc_idx)),
                pl.BlockSpec((d, tn), lambda i, j: (0, j))]
    args = [x2, norm_w.reshape(1, d), mod3, mod3, w]
    out_specs = [pl.BlockSpec((tm, tn), lambda i, j: (i, j))]
    out_shape = [jax.ShapeDtypeStruct((t, n), BF16)]
    if with_bd:
        in_specs.append(pl.BlockSpec((d, LANE), lambda i, j: (0, 0)))
        args.append(wbd)
        out_specs.append(pl.BlockSpec((tm, LANE), lambda i, j: (i, 0)))
        out_shape.append(jax.ShapeDtypeStruct((t, LANE), F32))
    res = pl.pallas_call(
        functools.partial(_norm_matmul_kernel, with_bd=with_bd),
        grid=(t // tm, n // tn),
        in_specs=in_specs,
        out_specs=out_specs,
        out_shape=out_shape,
        scratch_shapes=[pltpu.VMEM((tm, d), BF16)],
        compiler_params=_cparams(("parallel", "arbitrary")),
        name="norm_matmul",
    )(*args)
    return res if with_bd else res[0]


def _dn_gates_kernel(bd_ref, aneg_ref, dtb_ref, o_ref, *, n_heads):
    x = bd_ref[...]
    tm = x.shape[0]
    y = x + dtb_ref[...]
    g = aneg_ref[...] * (jnp.maximum(y, 0.0) + jnp.log1p(jnp.exp(-jnp.abs(y))))
    r = _iota((tm, tm), 0)
    c = _iota((tm, tm), 1)
    shift = DN_CHUNK.bit_length() - 1
    same = (r >> shift) == (c >> shift)
    prefix = _dot(jnp.where(same & (c <= r), 1.0, 0.0), g, HIGHEST)
    suffix = _dot(jnp.where(same & (c >= r), 1.0, 0.0), g, HIGHEST)
    lane = _iota(x.shape, 1)
    o_ref[...] = jnp.where(lane < 2 * n_heads, jax.nn.sigmoid(x),
                           jnp.where(lane < 3 * n_heads, prefix, suffix))


def _dn_gates(bd, aneg, dtb, n_heads):
    t = bd.shape[0]
    tm = min(512, t)
    assert t % tm == 0 and tm % DN_CHUNK == 0
    return pl.pallas_call(
        functools.partial(_dn_gates_kernel, n_heads=n_heads),
        grid=(t // tm,),
        in_specs=[pl.BlockSpec((tm, LANE), lambda i: (i, 0)),
                  pl.BlockSpec((1, LANE), lambda i: (0, 0)),
                  pl.BlockSpec((1, LANE), lambda i: (0, 0))],
        out_specs=pl.BlockSpec((tm, LANE), lambda i: (i, 0)),
        out_shape=jax.ShapeDtypeStruct((t, LANE), F32),
        compiler_params=_cparams(("parallel",)),
        name="dn_gates",
    )(bd, aneg, dtb)


def _s5_kernel(u_ref, wt_ref, win_ref, wst_ref, a_ref, h0_ref, y_ref, hfin_ref, s_ref, hp_ref,
               *, n_chunks, batch):
    half = LANE // 2
    u = u_ref[...].astype(F32)
    s_ref[...] = _dot(u, win_ref[...], HIGHEST)
    ar = a_ref[0:1, :]
    ai = a_ref[1:2, :]
    lane = _iota((batch, LANE), 1)
    is_fwd = lane < half

    def body(s, carry):
        hre, him = carry
        rf = pl.multiple_of(s * batch, batch)
        rb = pl.multiple_of((n_chunks - 1 - s) * batch, batch)
        hp_ref[pl.ds(rf, batch), 0:half] = hre[:, 0:half]
        hp_ref[pl.ds(rb, batch), half:LANE] = hre[:, half:LANE]
        hp_ref[pl.ds(rf, batch), LANE:LANE + half] = him[:, 0:half]
        hp_ref[pl.ds(rb, batch), LANE + half:2 * LANE] = him[:, half:LANE]
        sf = s_ref[pl.ds(rf, batch), :]
        sb = s_ref[pl.ds(rb, batch), :]
        sre = jnp.where(is_fwd, sf[:, 0:LANE], sb[:, 0:LANE])
        sim = jnp.where(is_fwd, sf[:, LANE:2 * LANE], sb[:, LANE:2 * LANE])
        return ar * hre - ai * him + sre, ar * him + ai * hre + sim

    h0 = h0_ref[...]
    hre, him = lax.fori_loop(0, n_chunks, body, (h0[:, 0:LANE], h0[:, LANE:2 * LANE]))
    hfin_ref[:, 0:LANE] = hre
    hfin_ref[:, LANE:2 * LANE] = him
    y_ref[...] = _dot(u, wt_ref[...], HIGHEST) + _dot(hp_ref[...], wst_ref[...], HIGHEST)


def _s5_core(ug, wt, win, wst, avec, h0, *, n_chunks, batch):
    g, rows, k = ug.shape
    assert batch == SUBLANE and k == 2 * LANE and rows == n_chunks * batch
    return pl.pallas_call(
        functools.partial(_s5_kernel, n_chunks=n_chunks, batch=batch),
        grid=(g,),
        in_specs=[pl.BlockSpec((None, rows, k), lambda i: (i, 0, 0)),
                  pl.BlockSpec((None, k, k), lambda i: (i, 0, 0)),
                  pl.BlockSpec((None, k, k), lambda i: (i, 0, 0)),
                  pl.BlockSpec((None, k, k), lambda i: (i, 0, 0)),
                  pl.BlockSpec((None, 2, LANE), lambda i: (i, 0, 0)),
                  pl.BlockSpec((None, batch, k), lambda i: (i, 0, 0))],
        out_specs=[pl.BlockSpec((None, rows, k), lambda i: (i, 0, 0)),
                   pl.BlockSpec((None, batch, k), lambda i: (i, 0, 0))],
        out_shape=[jax.ShapeDtypeStruct((g, rows, k), F32),
                   jax.ShapeDtypeStruct((g, batch, k), F32)],
        scratch_shapes=[pltpu.VMEM((rows, k), F32), pltpu.VMEM((rows, k), F32)],
        compiler_params=_cparams(("parallel",)),
        name="s5_core",
    )(ug, wt, win, wst, avec, h0)


def _s5_matrices(a_re, a_im, log_dt, b_re, b_im, c_re, c_im):
    tc = S5_CHUNK
    dt = jnp.exp(log_dt)[..., None]
    mag = jnp.exp(a_re * dt)
    abr, abi = mag * jnp.cos(a_im * dt), mag * jnp.sin(a_im * dt)
    den = a_re * a_re + a_im * a_im
    cr = ((abr - 1.0) * a_re + abi * a_im) / den
    ci = (abi * a_re - (abr - 1.0) * a_im) / den
    bbr = cr[..., None] * b_re - ci[..., None] * b_im
    bbi = cr[..., None] * b_im + ci[..., None] * b_re
    n = jnp.arange(tc + 1, dtype=F32)[:, None, None, None]
    pmag = jnp.exp(a_re * dt * n)
    pr, pi = pmag * jnp.cos(a_im * dt * n), pmag * jnp.sin(a_im * dt * n)
    car = c_re[None, None] * pr[:, :, :, None, :] - c_im[None, None] * pi[:, :, :, None, :]
    cai = c_re[None, None] * pi[:, :, :, None, :] + c_im[None, None] * pr[:, :, :, None, :]
    kern = (jnp.einsum("ndgcp,dgpe->ndgce", car[:tc], bbr, precision=HIGHEST)
            - jnp.einsum("ndgcp,dgpe->ndgce", cai[:tc], bbi, precision=HIGHEST))
    i_idx = jnp.arange(tc)[:, None]
    j_idx = jnp.arange(tc)[None, :]
    kf = jnp.where((j_idx >= i_idx)[:, :, None, None, None],
                   kern[jnp.clip(j_idx - i_idx, 0, tc - 1), 0], 0.0)
    kb = jnp.where((i_idx >= j_idx)[:, :, None, None, None],
                   kern[jnp.clip(i_idx - j_idx, 0, tc - 1), 1], 0.0)
    g_, c_ = c_re.shape[0], c_re.shape[1]
    wt = jnp.transpose(kf + kb, (2, 0, 4, 1, 3)).reshape(g_, tc * c_, tc * c_)
    ii = jnp.arange(tc)
    pf_r, pf_i = pr[tc - 1 - ii, 0], pi[tc - 1 - ii, 0]
    pb_r, pb_i = pr[ii, 1], pi[ii, 1]

    def in_block(p_r, p_i, d):
        re = p_r[..., None] * bbr[d][None] - p_i[..., None] * bbi[d][None]
        im = p_r[..., None] * bbi[d][None] + p_i[..., None] * bbr[d][None]
        to = lambda t_: jnp.transpose(t_, (1, 0, 3, 2)).reshape(g_, tc * c_, -1)
        return to(re), to(im)

    f_re, f_im = in_block(pf_r, pf_i, 0)
    b_re2, b_im2 = in_block(pb_r, pb_i, 1)
    win = jnp.concatenate([f_re, b_re2, f_im, b_im2], axis=-1)
    jj = jnp.arange(tc)
    to_rows = lambda t_: jnp.transpose(t_, (1, 3, 0, 2)).reshape(g_, -1, tc * c_)
    wst = jnp.concatenate([to_rows(car[jj + 1, 0]), to_rows(car[tc - jj, 1]),
                           to_rows(-cai[jj + 1, 0]), to_rows(-cai[tc - jj, 1])], axis=1)
    avec = jnp.stack([jnp.concatenate([pr[tc, 0], pr[tc, 1]], axis=-1),
                      jnp.concatenate([pi[tc, 0], pi[tc, 1]], axis=-1)], axis=1)
    return wt, win, wst, avec


def _s5_post_kernel(y_ref, u_ref, d_ref, w_ref, o_ref):
    y = y_ref[...] + d_ref[...] * u_ref[...].astype(F32)
    k0 = math.sqrt(2.0 / math.pi)
    g = 0.5 * y * (1.0 + jnp.tanh(k0 * (y + 0.044715 * (y * y * y))))
    o_ref[...] = (g * jax.nn.sigmoid(_dot(g.astype(BF16), w_ref[...]))).astype(o_ref.dtype)


def _s5_post(y, proj, s5_d, w_glu):
    t, d = y.shape
    tm = min(512, t)
    return pl.pallas_call(
        _s5_post_kernel,
        grid=(t // tm,),
        in_specs=[pl.BlockSpec((tm, d), lambda i: (i, 0)),
                  pl.BlockSpec((tm, d), lambda i: (i, 0)),
                  pl.BlockSpec((1, d), lambda i: (0, 0)),
                  pl.BlockSpec((d, d), lambda i: (0, 0))],
        out_specs=pl.BlockSpec((tm, d), lambda i: (i, 0)),
        out_shape=jax.ShapeDtypeStruct((t, d), BF16),
        compiler_params=_cparams(("parallel",)),
        name="s5_post",
    )(y, proj, s5_d.reshape(1, d), w_glu)


def _unit_tri_inverses(ms, r, c):
    mm = lambda a, b: _dot(a.astype(BF16), b.astype(BF16))
    eye = jnp.where(r == c, 1.0, 0.0)
    blk = lambda s: (r >> s) == (c >> s)
    mds = [jnp.where(blk(3), m, 0.0) for m in ms]
    m2s = [mm(md, md) for md in mds]
    m4s = [mm(m2, m2) for m2 in m2s]
    ts = [eye - md for md in mds]
    ts = [t + mm(t, m2) for t, m2 in zip(ts, m2s)]
    ts = [t + mm(t, m4) for t, m4 in zip(ts, m4s)]
    for s in (3, 4, 5):
        off = blk(s + 1) & jnp.logical_not(blk(s))
        xs = [mm(jnp.where(off, m, 0.0), t) for m, t in zip(ms, ts)]
        ts = [t - mm(t, x) for t, x in zip(ts, xs)]
    return ts


def _dn_kernel(q_ref, k_ref, v_ref, z_ref, col_ref, cwq_ref, cwk_ref, cwv_ref, nw_ref, s0_ref,
               o_ref, sfin_ref, xp_ref, qs_ref, ks_ref, vs_ref, u_ref, w_ref, a_ref, qg_ref,
               kt_ref, el_ref, od_ref, s_ref, *, seq, n_heads, hp, ca):
    c_sz = DN_CHUNK
    hd = DN_HEAD_DIM
    pad = SUBLANE
    rt = min(256, seq)
    n_chunks = seq // c_sz
    head0 = pl.program_id(1) * hp

    def prep(x_ref, cw_ref, dst_ref, normalise, scale):
        zeros = jnp.zeros((pad, hp * hd), F32)
        xp_ref[0:pad, :] = zeros
        xp_ref[seq + pad:seq + 2 * pad, :] = zeros
        xp_ref[pad:seq + pad, :] = x_ref[...].astype(F32)
        w = cw_ref[...]
        for t in range(seq // rt):
            r0 = t * rt
            acc = (w[0:1, :] * xp_ref[r0 + pad - 1:r0 + pad - 1 + rt, :]
                   + w[1:2, :] * xp_ref[r0 + pad:r0 + pad + rt, :]
                   + w[2:3, :] * xp_ref[r0 + pad + 1:r0 + pad + 1 + rt, :])
            y = _silu(acc)
            if normalise:
                parts = []
                for hl in range(hp):
                    yh = y[:, hl * hd:(hl + 1) * hd]
                    parts.append(yh * (lax.rsqrt(jnp.sum(yh * yh, axis=-1, keepdims=True) + EPS)
                                       * scale))
                y = jnp.concatenate(parts, axis=1)
            dst_ref[r0:r0 + rt, :] = y

    prep(q_ref, cwq_ref, qs_ref, True, hd ** -0.5)
    prep(k_ref, cwk_ref, ks_ref, True, 1.0)
    prep(v_ref, cwv_ref, vs_ref, False, 1.0)

    r = _iota((c_sz, c_sz), 0)
    c = _iota((c_sz, c_sz), 1)
    lane = _iota((c_sz, LANE), 1)

    def pick(col, idx):
        v = jnp.sum(jnp.where(lane == idx, col, 0.0), axis=-1, keepdims=True)
        return jnp.broadcast_to(v, (c_sz, LANE))

    def phase_a(it, carry):
        pairs = []
        for cc in range(ca):
            chunk = it * ca + cc
            rows = pl.ds(pl.multiple_of(chunk * c_sz, c_sz), c_sz)
            col = col_ref[rows, :]
            for hl in range(hp):
                hs = slice(hl * hd, (hl + 1) * hd)
                pairs.append((chunk, rows, hl, col, ks_ref[rows, hs], qs_ref[rows, hs],
                              vs_ref[rows, hs]))
        k16s = [p[4].astype(BF16) for p in pairs]
        kkts = [_dot_nt(k16, k16) for k16 in k16s]
        qkts = [_dot_nt(p[5].astype(BF16), k16) for p, k16 in zip(pairs, k16s)]
        chains = []
        for pi, (chunk, rows, hl, col, kc, qc, vc) in enumerate(pairs):
            for direction in range(2):
                lower = direction == 0
                bc = pick(col, head0 + hl + direction * n_heads)
                gc = pick(col, head0 + hl + (2 + direction) * n_heads)
                grow = jnp.concatenate([gc, gc], axis=0).T[0:c_sz, 0:c_sz]
                incl = (r >= c) if lower else (r <= c)
                strict = (r > c) if lower else (r < c)
                decay = jnp.where(incl, jnp.exp(jnp.minimum(gc[:, 0:c_sz] - grow, 0.0)), 0.0)
                m = jnp.where(strict, bc[:, 0:c_sz] * kkts[pi] * decay, 0.0)
                chains.append((pi, direction, bc, gc, decay, m))
        tinvs = _unit_tri_inverses([ch[5] for ch in chains], r, c)
        egs = [jnp.exp(ch[3]) for ch in chains]
        uws = []
        for (pi, direction, bc, gc, decay, m), tinv, eg in zip(chains, tinvs, egs):
            kc, vc = pairs[pi][4], pairs[pi][6]
            rhs = jnp.concatenate([vc * bc, kc * bc * eg], axis=1).astype(BF16)
            uws.append(_dot(tinv.astype(BF16), rhs))
        for (pi, direction, bc, gc, decay, m), eg, uw in zip(chains, egs, uws):
            chunk, rows, hl, col, kc, qc, vc = pairs[pi]
            idx = hl * 2 + direction
            g_last = gc[c_sz - 1:c_sz, :] if direction == 0 else gc[0:1, :]
            u_ref[idx, rows, :] = uw[:, 0:hd]
            w_ref[idx, rows, :] = uw[:, hd:2 * hd].astype(BF16)
            a_ref[idx, rows, :] = (qkts[pi] * decay).astype(BF16)
            qg_ref[idx, rows, :] = (qc * eg).astype(BF16)
            kt_ref[idx, rows, :] = (kc * jnp.exp(g_last - gc)).astype(BF16)
            el_ref[idx, pl.ds(pl.multiple_of(chunk * SUBLANE, SUBLANE), SUBLANE), :] = (
                jnp.broadcast_to(jnp.exp(g_last), (SUBLANE, LANE)))
        return carry

    lax.fori_loop(0, n_chunks // ca, phase_a, 0)

    for hl in range(hp):
        for direction in range(2):
            s_ref[hl * 2 + direction] = s0_ref[direction, hl]

    def phase_b(step, carry):
        ids, rows, erows = [], [], []
        for hl in range(hp):
            for direction in range(2):
                chunk = step if direction == 0 else n_chunks - 1 - step
                ids.append(hl * 2 + direction)
                rows.append(pl.ds(pl.multiple_of(chunk * c_sz, c_sz), c_sz))
                erows.append(pl.ds(pl.multiple_of(chunk * SUBLANE, SUBLANE), SUBLANE))
        ss = [s_ref[i] for i in ids]
        s16s = [s.astype(BF16) for s in ss]
        wss = [_dot(w_ref[i, rw, :], s16) for i, rw, s16 in zip(ids, rows, s16s)]
        qss = [_dot(qg_ref[i, rw, :], s16) for i, rw, s16 in zip(ids, rows, s16s)]
        vns = [(u_ref[i, rw, :] - ws).astype(BF16) for i, rw, ws in zip(ids, rows, wss)]
        avs = [_dot(a_ref[i, rw, :], vn) for i, rw, vn in zip(ids, rows, vns)]
        kvs = [_dot_tn(kt_ref[i, rw, :], vn) for i, rw, vn in zip(ids, rows, vns)]
        for i, rw, er, s, qs_, av, kv in zip(ids, rows, erows, ss, qss, avs, kvs):
            od_ref[i, rw, :] = qs_ + av
            s_ref[i] = s * el_ref[i, er, :][0:1, :] + kv
        return carry

    lax.fori_loop(0, n_chunks, phase_b, 0)

    for hl in range(hp):
        for direction in range(2):
            sfin_ref[direction, hl] = s_ref[hl * 2 + direction]

    for t in range(seq // rt):
        r0 = t * rt
        for hl in range(hp):
            hs = slice(hl * hd, (hl + 1) * hd)
            o = od_ref[hl * 2, r0:r0 + rt, :] + od_ref[hl * 2 + 1, r0:r0 + rt, :]
            o = o * lax.rsqrt(jnp.mean(o * o, axis=-1, keepdims=True) + EPS) * nw_ref[...]
            o_ref[r0:r0 + rt, hs] = (o * _silu(z_ref[r0:r0 + rt, hs].astype(F32))).astype(o_ref.dtype)


def _deltanet(proj, col, conv_w, norm_w, s0, *, batch, seq, n_heads, col_q, col_z):
    t = proj.shape[0]
    hd = DN_HEAD_DIM
    hp = 2
    d = n_heads * hd
    wd = hp * hd
    n_chunks = seq // DN_CHUNK
    ca = min(4, n_chunks)
    assert n_heads % hp == 0 and col_q % hp == 0 and col_z % hp == 0 and n_chunks % ca == 0
    blk = lambda off: pl.BlockSpec((seq, wd), lambda b, h: (b, off // hp + h))
    cw = lambda off: pl.BlockSpec((3, wd), lambda b, h: (0, off // hp + h))
    st = pl.BlockSpec((None, 2, hp, hd, hd), lambda b, h: (b, 0, h, 0, 0))
    nst = 2 * hp
    return pl.pallas_call(
        functools.partial(_dn_kernel, seq=seq, n_heads=n_heads, hp=hp, ca=ca),
        grid=(batch, n_heads // hp),
        in_specs=[blk(col_q), blk(col_q + n_heads), blk(col_q + 2 * n_heads), blk(col_z),
                  pl.BlockSpec((seq, LANE), lambda b, h: (b, 0)),
                  cw(0), cw(n_heads), cw(2 * n_heads),
                  pl.BlockSpec((1, hd), lambda b, h: (0, 0)),
                  st],
        out_specs=[pl.BlockSpec((seq, wd), lambda b, h: (b, h)), st],
        out_shape=[jax.ShapeDtypeStruct((t, d), BF16),
                   jax.ShapeDtypeStruct(s0.shape, F32)],
        scratch_shapes=[pltpu.VMEM((seq + 2 * SUBLANE, wd), F32),
                        pltpu.VMEM((seq, wd), F32), pltpu.VMEM((seq, wd), F32),
                        pltpu.VMEM((seq, wd), F32),
                        pltpu.VMEM((nst, seq, hd), F32),
                        pltpu.VMEM((nst, seq, hd), BF16),
                        pltpu.VMEM((nst, seq, DN_CHUNK), BF16),
                        pltpu.VMEM((nst, seq, hd), BF16),
                        pltpu.VMEM((nst, seq, hd), BF16),
                        pltpu.VMEM((nst, n_chunks * SUBLANE, LANE), F32),
                        pltpu.VMEM((nst, seq, hd), F32),
                        pltpu.VMEM((nst, hd, hd), F32)],
        compiler_params=_cparams(("parallel", "parallel")),
        name="deltanet",
    )(proj, proj, proj, proj, col, conv_w, conv_w, conv_w, norm_w.reshape(1, hd), s0)


def _conv_module_kernel(a_ref, g_ref, w_ref, b_ref, lnw_ref, lnb_ref, o_ref, yp_ref, tmp_ref,
                        *, seq, taps):
    d = a_ref.shape[1]
    half = taps // 2
    pad = 2 * SUBLANE
    assert half < pad
    rt = 64
    ct = 256
    zeros = jnp.zeros((pad, d), F32)
    yp_ref[0:pad, :] = zeros
    yp_ref[seq + pad:seq + 2 * pad, :] = zeros

    def fill(t, carry):
        r0 = pl.multiple_of(t * rt, rt)
        a = a_ref[pl.ds(r0, rt), :].astype(F32)
        g = g_ref[pl.ds(r0, rt), :].astype(F32)
        yp_ref[pl.ds(r0 + pad, rt), :] = a * jax.nn.sigmoid(g)
        return carry

    lax.fori_loop(0, seq // rt, fill, 0)

    def tile(t, carry):
        r0 = pl.multiple_of(t * rt, rt)
        for cc in range(d // ct):
            cs = slice(cc * ct, (cc + 1) * ct)
            acc = jnp.zeros((rt, ct), F32) + b_ref[:, cs]
            for j in range(taps):
                acc = acc + w_ref[j:j + 1, cs] * _shifted_rows(yp_ref, r0 + pad, j - half, rt, cs)
            tmp_ref[:, cs] = acc
        y = tmp_ref[...]
        mu = jnp.mean(y, axis=-1, keepdims=True)
        yc = y - mu
        var = jnp.mean(yc * yc, axis=-1, keepdims=True)
        y = yc * lax.rsqrt(var + EPS) * lnw_ref[...] + lnb_ref[...]
        o_ref[pl.ds(r0, rt), :] = _silu(y).astype(o_ref.dtype)
        return carry

    lax.fori_loop(0, seq // rt, tile, 0)


def _conv_module(proj, w, b, lnw, lnb, *, batch, seq, col_a):
    t = proj.shape[0]
    taps, d = w.shape
    return pl.pallas_call(
        functools.partial(_conv_module_kernel, seq=seq, taps=taps),
        grid=(batch,),
        in_specs=[pl.BlockSpec((seq, d), lambda i: (i, col_a)),
                  pl.BlockSpec((seq, d), lambda i: (i, col_a + 1)),
                  pl.BlockSpec((taps, d), lambda i: (0, 0)),
                  pl.BlockSpec((1, d), lambda i: (0, 0)),
                  pl.BlockSpec((1, d), lambda i: (0, 0)),
                  pl.BlockSpec((1, d), lambda i: (0, 0))],
        out_specs=pl.BlockSpec((seq, d), lambda i: (i, 0)),
        out_shape=jax.ShapeDtypeStruct((t, d), BF16),
        scratch_shapes=[pltpu.VMEM((seq + 4 * SUBLANE, d), F32), pltpu.VMEM((64, d), F32)],
        compiler_params=_cparams(("parallel",)),
        name="conv_module",
    )(proj, proj, w, b.reshape(1, d), lnw.reshape(1, d), lnb.reshape(1, d))


def _merge_kernel(s5_ref, dn_ref, cv_ref, g0_ref, g1_ref, g2_ref, w0_ref, w1_ref, w2_ref,
                  wo_ref, x_ref, gate_ref, o_ref):
    merged = (jax.nn.sigmoid(g0_ref[...].astype(F32)) * _dot(s5_ref[...], w0_ref[...])
              + jax.nn.sigmoid(g1_ref[...].astype(F32)) * _dot(dn_ref[...], w1_ref[...])
              + jax.nn.sigmoid(g2_ref[...].astype(F32)) * _dot(cv_ref[...], w2_ref[...]))
    y = _dot(merged.astype(BF16), wo_ref[...])
    o_ref[...] = x_ref[...] + gate_ref[0] * y


def _merge(s5_out, dn_out, cv_out, proj, w_s5, w_dn, w_cv, w_out, x2, mod3, gate_idx,
           *, rows_per_batch, fixed_row, col_gate):
    t, d = x2.shape
    tm = min(512, t if fixed_row is not None else rows_per_batch)
    assert t % tm == 0
    row = _row_of_tile(tm, rows_per_batch, fixed_row)
    act = pl.BlockSpec((tm, d), lambda i: (i, 0))
    gat = lambda k: pl.BlockSpec((tm, d), lambda i: (i, col_gate + k))
    wsp = pl.BlockSpec((d, d), lambda i: (0, 0))
    return pl.pallas_call(
        _merge_kernel,
        grid=(t // tm,),
        in_specs=[act, act, act, gat(0), gat(1), gat(2), wsp, wsp, wsp, wsp, act,
                  pl.BlockSpec((None, 1, d), lambda i: (row(i), 0, gate_idx))],
        out_specs=act,
        out_shape=jax.ShapeDtypeStruct((t, d), F32),
        compiler_params=_cparams(("parallel",)),
        name="merge",
    )(s5_out, dn_out, cv_out, proj, proj, proj, w_s5, w_dn, w_cv, w_out, x2, mod3)


def _ffn_conv_kernel(a_ref, v_ref, w_ref, b_ref, o_ref, a0_ref, al_ref, ar_ref, *, seq, grid2d):
    ct = a_ref.shape[1]
    pad = 72 if grid2d else SUBLANE
    rt = 128
    zeros = jnp.zeros((pad, ct), F32)
    bufs = (a0_ref, al_ref, ar_ref) if grid2d else (a0_ref,)
    for buf in bufs:
        buf[0:pad, :] = zeros
        buf[seq + pad:seq + 2 * pad, :] = zeros

    def fill(t, carry):
        r0 = pl.multiple_of(t * rt, rt)
        a = a_ref[pl.ds(r0, rt), :].astype(F32)
        a0_ref[pl.ds(r0 + pad, rt), :] = a
        if grid2d:
            colpos = (_iota((rt, ct), 0) + r0) & (GRID_W - 1)
            al_ref[pl.ds(r0 + pad, rt), :] = jnp.where(colpos == GRID_W - 1, 0.0, a)
            ar_ref[pl.ds(r0 + pad, rt), :] = jnp.where(colpos == 0, 0.0, a)
        return carry

    lax.fori_loop(0, seq // rt, fill, 0)

    if grid2d:
        taps = [(dr, dc) for dr in (-1, 0, 1) for dc in (-1, 0, 1)]
    else:
        taps = [(0, dc) for dc in (-1, 0, 1)]
    src = {-1: al_ref, 0: a0_ref, 1: ar_ref} if grid2d else {-1: a0_ref, 0: a0_ref, 1: a0_ref}

    def tile(t, carry):
        r0 = pl.multiple_of(t * rt, rt)
        acc = jnp.zeros((rt, ct), F32) + b_ref[...]
        for dr, dc in taps:
            widx = (dr + 1) * 3 + (dc + 1)
            acc = acc + w_ref[widx:widx + 1, :] * _shifted_rows(
                src[dc], r0 + pad, dr * GRID_W + dc, rt, slice(None))
        o_ref[pl.ds(r0, rt), :] = (_silu(acc) * v_ref[pl.ds(r0, rt), :].astype(F32)).astype(o_ref.dtype)
        return carry

    lax.fori_loop(0, seq // rt, tile, 0)


def _ffn_conv(av, w9, bias, *, batch, seq, grid2d):
    t = av.shape[0]
    f = w9.shape[1]
    ct = 256
    assert f % ct == 0 and seq % 128 == 0
    nct = f // ct
    pad = 72 if grid2d else SUBLANE
    return pl.pallas_call(
        functools.partial(_ffn_conv_kernel, seq=seq, grid2d=grid2d),
        grid=(batch, nct),
        in_specs=[pl.BlockSpec((seq, ct), lambda b, c: (b, c)),
                  pl.BlockSpec((seq, ct), lambda b, c: (b, nct + c)),
                  pl.BlockSpec((9, ct), lambda b, c: (0, c)),
                  pl.BlockSpec((1, ct), lambda b, c: (0, c))],
        out_specs=pl.BlockSpec((seq, ct), lambda b, c: (b, c)),
        out_shape=jax.ShapeDtypeStruct((t, f), BF16),
        scratch_shapes=[pltpu.VMEM((seq + 2 * pad, ct), F32)] * 3,
        compiler_params=_cparams(("parallel", "parallel")),
        name="ffn_conv",
    )(av, av, w9, bias.reshape(1, f))


def _resid_matmul_kernel(*refs, final_norm):
    if final_norm:
        a_ref, w_ref, x_ref, gate_ref, fw_ref, o_ref = refs
    else:
        a_ref, w_ref, x_ref, gate_ref, o_ref = refs
    y = x_ref[...] + gate_ref[0] * _dot(a_ref[...], w_ref[...])
    if final_norm:
        y = y * lax.rsqrt(jnp.mean(y * y, axis=-1, keepdims=True) + EPS) * fw_ref[...]
    o_ref[...] = y


def _resid_matmul(a, w, x2, mod3, gate_idx, *, rows_per_batch, fixed_row, final_w=None):
    t, d = x2.shape
    k = a.shape[1]
    tm = min(512, t if fixed_row is not None else rows_per_batch)
    assert t % tm == 0
    row = _row_of_tile(tm, rows_per_batch, fixed_row)
    final_norm = final_w is not None
    in_specs = [pl.BlockSpec((tm, k), lambda i: (i, 0)),
                pl.BlockSpec((k, d), lambda i: (0, 0)),
                pl.BlockSpec((tm, d), lambda i: (i, 0)),
                pl.BlockSpec((None, 1, d), lambda i: (row(i), 0, gate_idx))]
    args = [a, w, x2, mod3]
    if final_norm:
        in_specs.append(pl.BlockSpec((1, d), lambda i: (0, 0)))
        args.append(final_w.reshape(1, d))
    return pl.pallas_call(
        functools.partial(_resid_matmul_kernel, final_norm=final_norm),
        grid=(t // tm,),
        in_specs=in_specs,
        out_specs=pl.BlockSpec((tm, d), lambda i: (i, 0)),
        out_shape=jax.ShapeDtypeStruct((t, d), F32),
        compiler_params=_cparams(("parallel",)),
        name="resid_matmul",
    )(*args)


def _group_major(u2, batch, seq, groups):
    nk = seq // S5_CHUNK
    u5 = u2.reshape(batch, nk, S5_CHUNK, groups, S5_GROUP)
    return jnp.transpose(u5, (3, 1, 0, 2, 4)).reshape(groups, nk * batch, S5_CHUNK * S5_GROUP)


def _token_major(yg, batch, seq, groups):
    nk = seq // S5_CHUNK
    y5 = yg.reshape(groups, nk, batch, S5_CHUNK, S5_GROUP)
    return jnp.transpose(y5, (2, 1, 3, 0, 4)).reshape(batch * seq, groups * S5_GROUP)


def kernel(x, c, ctx, c_ctx, ada_w, ada_b, norm1_w, norm2_w, w_in, s5_a_re, s5_a_im, s5_log_dt, s5_b_re, s5_b_im, s5_c_re, s5_c_im, s5_d, s5_w_glu, dn_conv_w, dn_a_log, dn_dt_bias, dn_norm_w, cv_dw_w, cv_dw_b, cv_ln_w, cv_ln_b, w_br_s5, w_br_dn, w_br_cv, w_out, ffn_w_up, ffn_dw_w, ffn_dw_b, ffn_w_down, final_norm_w):
    batch, seq, d = x.shape
    ctx_len = ctx.shape[1]
    depth = ada_w.shape[0]
    groups = s5_a_re.shape[2]
    n_heads = dn_a_log.shape[2]
    dn_width = n_heads * DN_HEAD_DIM
    ffn_hidden = ffn_dw_b.shape[1]
    assert batch == SUBLANE and batch < MOD_ROWS and d == groups * S5_GROUP == dn_width
    assert d % 1024 == 0 and 4 * n_heads <= LANE

    col_qkv = d
    col_beta = col_qkv + 3 * dn_width
    n_state = col_beta + 4 * n_heads
    col_cv = n_state + dn_width
    col_gate = col_cv + 2 * d
    pk_q = d // LANE
    pk_z = (d + 3 * dn_width) // LANE
    pk_cv = (d + 4 * dn_width) // d
    pk_gate = pk_cv + 2

    cmat = jnp.zeros((MOD_ROWS, d), F32).at[:batch].set(c).at[batch].set(c_ctx)
    xl = x.reshape(batch * seq, d)
    xc = ctx.reshape(batch * ctx_len, d)

    for i in range(depth):
        last = i == depth - 1
        w_main = jnp.concatenate([w_in[i][:, :col_beta], w_in[i][:, n_state:]], axis=1).astype(BF16)
        w_bd = jnp.pad(w_in[i][:, col_beta:n_state], ((0, 0), (0, LANE - 4 * n_heads)))
        aneg = jnp.pad(-jnp.exp(dn_a_log[i].reshape(1, -1)), ((0, 0), (2 * n_heads, LANE - 4 * n_heads)))
        dtb = jnp.pad(dn_dt_bias[i].reshape(1, -1), ((0, 0), (2 * n_heads, LANE - 4 * n_heads)))
        wt, win, wst, avec = _s5_matrices(s5_a_re[i], s5_a_im[i], s5_log_dt[i], s5_b_re[i],
                                          s5_b_im[i], s5_c_re[i], s5_c_im[i])
        w_glu = s5_w_glu[i].astype(BF16)
        w_brs = w_br_s5[i].astype(BF16)
        w_brd = w_br_dn[i].astype(BF16)
        w_brc = w_br_cv[i].astype(BF16)
        w_o = w_out[i].astype(BF16)
        w_up = ffn_w_up[i].astype(BF16)
        w_dn = ffn_w_down[i].astype(BF16)
        w9 = ffn_dw_w[i].reshape(9, ffn_hidden)

        mod3 = _modulation(cmat, ada_w[i], ada_b[i]).reshape(MOD_ROWS, 1, N_MOD * d)

        def stream(x2, length, fixed_row, s5_init, dn_init, grid2d, states_only, final_w):
            rows = dict(rows_per_batch=length, fixed_row=fixed_row)
            proj, bd = _norm_matmul(x2, norm1_w[i], mod3, 0, 1, w_main, wbd=w_bd, tn=1024, **rows)
            col = _dn_gates(bd, aneg, dtb, n_heads)
            ug = _group_major(proj[:, :d], batch, length, groups)
            yg, s5_fin = _s5_core(ug, wt, win, wst, avec, s5_init,
                                  n_chunks=length // S5_CHUNK, batch=batch)
            dn_out, dn_fin = _deltanet(proj, col, dn_conv_w[i], dn_norm_w[i], dn_init,
                                       batch=batch, seq=length, n_heads=n_heads,
                                       col_q=pk_q, col_z=pk_z)
            if states_only:
                return None, (s5_fin, dn_fin)
            s5_out = _s5_post(_token_major(yg, batch, length, groups), proj, s5_d[i], w_glu)
            cv_out = _conv_module(proj, cv_dw_w[i], cv_dw_b[i], cv_ln_w[i], cv_ln_b[i],
                                  batch=batch, seq=length, col_a=pk_cv)
            x2 = _merge(s5_out, dn_out, cv_out, proj, w_brs, w_brd, w_brc, w_o, x2, mod3, 2,
                        col_gate=pk_gate, **rows)
            av = _norm_matmul(x2, norm2_w[i], mod3, 3, 4, w_up, tn=w_up.shape[1] // 4, **rows)
            hidden = _ffn_conv(av, w9, ffn_dw_b[i], batch=batch, seq=length, grid2d=grid2d)
            x2 = _resid_matmul(hidden, w_dn, x2, mod3, 5, final_w=final_w, **rows)
            return x2, (s5_fin, dn_fin)

        zero_s5 = jnp.zeros((groups, batch, 2 * LANE), F32)
        zero_dn = jnp.zeros((batch, 2, n_heads, DN_HEAD_DIM, DN_HEAD_DIM), F32)
        xc, (s5_state, dn_state) = stream(xc, ctx_len, batch, zero_s5, zero_dn, False, last, None)
        xl, _ = stream(xl, seq, None, s5_state, dn_state, True, False,
                       final_norm_w if last else None)

    return xl.reshape(batch, seq, d)
```

```python
import functools
import math

import jax
import jax.numpy as jnp
from jax import lax
from jax.experimental import pallas as pl
from jax.experimental.pallas import tpu as pltpu

F32 = jnp.float32
BF16 = jnp.bfloat16
HIGHEST = lax.Precision.HIGHEST

EPS = 1e-6
GRID_W = 64
S5_GROUP = 16
DN_HEAD_DIM = 128
N_MOD = 6
N_BRANCH = 3

LANE = 128
SUBLANE = 8
S5_CHUNK = 16
DN_CHUNK = 64
MOD_ROWS = 16
CV_ROW_TILE = 64
CV_COL_TILE = 256
VMEM_LIMIT = 56 * 1024 * 1024


def _cparams(sem, vmem=VMEM_LIMIT):
    return pltpu.CompilerParams(dimension_semantics=sem, vmem_limit_bytes=vmem)


def _dot(a, b, precision=None):
    return jnp.dot(a, b, preferred_element_type=F32, precision=precision)


def _dot_nt(a, b, precision=None):
    return lax.dot_general(a, b, (((1,), (1,)), ((), ())), preferred_element_type=F32,
                           precision=precision)


def _dot_tn(a, b, precision=None):
    return lax.dot_general(a, b, (((0,), (0,)), ((), ())), preferred_element_type=F32,
                           precision=precision)


def _silu(x):
    return x * jax.nn.sigmoid(x)


def _iota(shape, dim):
    return lax.broadcasted_iota(jnp.int32, shape, dim)


def _shifted_rows(ref, base, off, rows, cols):
    lo = (off // SUBLANE) * SUBLANE
    shift = off - lo
    start = pl.multiple_of(base + lo, SUBLANE)
    if shift == 0:
        return ref[pl.ds(start, rows), cols]
    return ref[pl.ds(start, rows + SUBLANE), cols][shift:shift + rows]


def _mod_kernel(c_ref, w_ref, b_ref, o_ref):
    o_ref[...] = _dot(_silu(c_ref[...]), w_ref[...], HIGHEST) + b_ref[...]


def _modulation(cmat, ada_w, ada_b):
    rows, d = cmat.shape
    n = ada_w.shape[1]
    tn = 1024
    return pl.pallas_call(
        _mod_kernel,
        grid=(n // tn,),
        in_specs=[pl.BlockSpec((rows, d), lambda j: (0, 0)),
                  pl.BlockSpec((d, tn), lambda j: (0, j)),
                  pl.BlockSpec((1, tn), lambda j: (0, j))],
        out_specs=pl.BlockSpec((rows, tn), lambda j: (0, j)),
        out_shape=jax.ShapeDtypeStruct((rows, n), F32),
        compiler_params=_cparams(("parallel",)),
        name="modulation",
    )(cmat, ada_w, ada_b.reshape(1, n))


def _norm_matmul_kernel(*refs, with_bd):
    if with_bd:
        x_ref, nw_ref, sh_ref, sc_ref, w_ref, wbd_ref, o_ref, bd_ref, u8_ref, h_ref = refs
    else:
        x_ref, nw_ref, sh_ref, sc_ref, w_ref, o_ref, h_ref = refs

    @pl.when(pl.program_id(1) == 0)
    def _():
        x = x_ref[...]
        h = x * lax.rsqrt(jnp.mean(x * x, axis=-1, keepdims=True) + EPS) * nw_ref[...]
        h = h * (1.0 + sc_ref[0]) + sh_ref[0]
        h_ref[...] = h.astype(BF16)
        if with_bd:
            bd_ref[...] = _dot(h, wbd_ref[...], HIGHEST)

    res = _dot(h_ref[...], w_ref[...]).astype(o_ref.dtype)
    o_ref[...] = res
    if with_bd:
        @pl.when(pl.program_id(1) == 0)
        def _():
            for l in range(u8_ref.shape[0]):
                u8_ref[l] = res[:, l * LANE:(l + 1) * LANE]


def _row_of_tile(tm, rows_per_batch, fixed_row):
    if fixed_row is not None:
        return lambda i: fixed_row
    return lambda i: (i * tm) // rows_per_batch


def _norm_matmul(x2, norm_w, mod3, sh_idx, sc_idx, w, *, rows_per_batch, fixed_row, wbd=None,
                 tn):
    t, d = x2.shape
    n = w.shape[1]
    tm = min(1024, t if fixed_row is not None else rows_per_batch)
    assert t % tm == 0 and n % tn == 0
    row = _row_of_tile(tm, rows_per_batch, fixed_row)
    with_bd = wbd is not None
    in_specs = [pl.BlockSpec((tm, d), lambda i, j: (i, 0)),
                pl.BlockSpec((1, d), lambda i, j: (0, 0)),
                pl.BlockSpec((None, 1, d), lambda i, j: (row(i), 0, sh_idx)),
                pl.BlockSpec((None, 1, d), lambda i, j: (row(i), 0, sc_idx)),
                pl.BlockSpec((d, tn), lambda i, j: (0, j))]
    args = [x2, norm_w.reshape(1, d), mod3, mod3, w]
    out_specs = [pl.BlockSpec((tm, tn), lambda i, j: (i, j))]
    out_shape = [jax.ShapeDtypeStruct((t, n), BF16)]
    if with_bd:
        assert tn % LANE == 0
        in_specs.append(pl.BlockSpec((d, LANE), lambda i, j: (0, 0)))
        args.append(wbd)
        out_specs.append(pl.BlockSpec((tm, LANE), lambda i, j: (i, 0)))
        out_shape.append(jax.ShapeDtypeStruct((t, LANE), F32))
        out_specs.append(pl.BlockSpec((tn // LANE, tm, LANE), lambda i, j: (0, i, 0)))
        out_shape.append(jax.ShapeDtypeStruct((tn // LANE, t, LANE), BF16))
    res = pl.pallas_call(
        functools.partial(_norm_matmul_kernel, with_bd=with_bd),
        grid=(t // tm, n // tn),
        in_specs=in_specs,
        out_specs=out_specs,
        out_shape=out_shape,
        scratch_shapes=[pltpu.VMEM((tm, d), BF16)],
        compiler_params=_cparams(("parallel", "arbitrary")),
        name="norm_matmul",
    )(*args)
    return res if with_bd else res[0]


def _dn_gates_kernel(bd_ref, aneg_ref, dtb_ref, o_ref, *, n_heads):
    x = bd_ref[...]
    tm = x.shape[0]
    y = x + dtb_ref[...]
    g = aneg_ref[...] * (jnp.maximum(y, 0.0) + jnp.log1p(jnp.exp(-jnp.abs(y))))
    r = _iota((tm, tm), 0)
    c = _iota((tm, tm), 1)
    shift = DN_CHUNK.bit_length() - 1
    same = (r >> shift) == (c >> shift)
    prefix = _dot(jnp.where(same & (c <= r), 1.0, 0.0), g, HIGHEST)
    suffix = _dot(jnp.where(same & (c >= r), 1.0, 0.0), g, HIGHEST)
    lane = _iota(x.shape, 1)
    o_ref[...] = jnp.where(lane < 2 * n_heads, jax.nn.sigmoid(x),
                           jnp.where(lane < 3 * n_heads, prefix, suffix))


def _dn_gates(bd, aneg, dtb, n_heads):
    t = bd.shape[0]
    tm = min(512, t)
    assert t % tm == 0 and tm % DN_CHUNK == 0
    return pl.pallas_call(
        functools.partial(_dn_gates_kernel, n_heads=n_heads),
        grid=(t // tm,),
        in_specs=[pl.BlockSpec((tm, LANE), lambda i: (i, 0)),
                  pl.BlockSpec((1, LANE), lambda i: (0, 0)),
                  pl.BlockSpec((1, LANE), lambda i: (0, 0))],
        out_specs=pl.BlockSpec((tm, LANE), lambda i: (i, 0)),
        out_shape=jax.ShapeDtypeStruct((t, LANE), F32),
        compiler_params=_cparams(("parallel",)),
        name="dn_gates",
    )(bd, aneg, dtb)


def _s5_state_kernel(u_ref, win_ref, a_ref, hp_ref, s_ref, *, segments, batch):
    half = LANE // 2
    n_grp = s_ref.shape[1] // (2 * LANE)
    s_ref[...] = _dot(u_ref[...], win_ref[...])
    ars = [a_ref[0:1, g * LANE:(g + 1) * LANE] for g in range(n_grp)]
    ais = [a_ref[1:2, g * LANE:(g + 1) * LANE] for g in range(n_grp)]
    is_fwd = _iota((batch, LANE), 1) < half
    carry = tuple(jnp.zeros((batch, LANE), F32) for _ in range(2 * n_grp))
    for row0, n_chunks in segments:
        def body(s, carry, row0=row0, n_chunks=n_chunks):
            rf = pl.ds(pl.multiple_of(row0 + s * batch, batch), batch)
            rb = pl.ds(pl.multiple_of(row0 + (n_chunks - 1 - s) * batch, batch), batch)
            sf = s_ref[rf, :]
            sb = s_ref[rb, :]
            new = []
            for g in range(n_grp):
                hre, him = carry[2 * g], carry[2 * g + 1]
                c0 = g * 2 * LANE
                hp_ref[rf, c0:c0 + half] = hre[:, 0:half]
                hp_ref[rb, c0 + half:c0 + LANE] = hre[:, half:LANE]
                hp_ref[rf, c0 + LANE:c0 + LANE + half] = him[:, 0:half]
                hp_ref[rb, c0 + LANE + half:c0 + 2 * LANE] = him[:, half:LANE]
                sre = jnp.where(is_fwd, sf[:, c0:c0 + LANE], sb[:, c0:c0 + LANE])
                sim = jnp.where(is_fwd, sf[:, c0 + LANE:c0 + 2 * LANE],
                                sb[:, c0 + LANE:c0 + 2 * LANE])
                new.append(ars[g] * hre - ais[g] * him + sre)
                new.append(ars[g] * him + ais[g] * hre + sim)
            return tuple(new)

        carry = lax.fori_loop(0, n_chunks, body, carry)


S5_STATE_TILE = 4 * LANE


def _s5_state(u_rows, win8, a8, *, segments, batch):
    nt, rows, k = u_rows.shape
    ns = win8.shape[2]
    tn = S5_STATE_TILE
    assert batch == SUBLANE and ns % tn == 0
    return pl.pallas_call(
        functools.partial(_s5_state_kernel, segments=segments, batch=batch),
        grid=(nt, ns // tn),
        in_specs=[pl.BlockSpec((None, rows, k), lambda l, n: (l, 0, 0)),
                  pl.BlockSpec((None, k, tn), lambda l, n: (l, 0, n)),
                  pl.BlockSpec((None, 2, tn // 2), lambda l, n: (l, 0, n))],
        out_specs=pl.BlockSpec((None, rows, tn), lambda l, n: (l, 0, n)),
        out_shape=jax.ShapeDtypeStruct((nt, rows, ns), F32),
        scratch_shapes=[pltpu.VMEM((rows, tn), F32)],
        compiler_params=_cparams(("parallel", "parallel")),
        name="s5_state",
    )(u_rows, win8, a8)


def _s5_out_kernel(u_ref, hp_ref, wt_ref, wst_ref, y_ref):
    y_ref[...] = (_dot(u_ref[...], wt_ref[...])
                  + _dot(hp_ref[...].astype(BF16), wst_ref[...]))


def _s5_out(u_rows, hp, wt8, wst8):
    nt, rows, k = u_rows.shape
    ns = hp.shape[2]
    tn = 4 * LANE
    return pl.pallas_call(
        _s5_out_kernel,
        grid=(nt, k // tn),
        in_specs=[pl.BlockSpec((None, rows, k), lambda l, n: (l, 0, 0)),
                  pl.BlockSpec((None, rows, ns), lambda l, n: (l, 0, 0)),
                  pl.BlockSpec((None, k, tn), lambda l, n: (l, 0, n)),
                  pl.BlockSpec((None, ns, tn), lambda l, n: (l, 0, n))],
        out_specs=pl.BlockSpec((None, rows, tn), lambda l, n: (l, 0, n)),
        out_shape=jax.ShapeDtypeStruct((nt, rows, k), F32),
        compiler_params=_cparams(("parallel", "parallel")),
        name="s5_out",
    )(u_rows, hp, wt8, wst8)


def _s5_slab_operators(wt, win, wst, avec):
    g, kc, ns = win.shape
    tc, c = S5_CHUNK, S5_GROUP
    gt = LANE // c
    nt = g // gt
    eye = jnp.eye(gt, dtype=F32)
    wt8 = wt.reshape(nt, gt, tc, c, tc, 1, c) * eye[None, :, None, None, None, :, None]
    wt8 = jnp.transpose(wt8, (0, 2, 1, 3, 4, 5, 6)).reshape(nt, tc * LANE, tc * LANE)
    win8 = win.reshape(nt, gt, tc, c, 1, ns) * eye[None, :, None, None, :, None]
    win8 = jnp.transpose(win8, (0, 2, 1, 3, 4, 5)).reshape(nt, tc * LANE, gt * ns)
    wst8 = wst.reshape(nt, gt, ns, tc, 1, c) * eye[None, :, None, None, :, None]
    wst8 = wst8.reshape(nt, gt * ns, tc * LANE)
    a8 = jnp.transpose(avec.reshape(nt, gt, 2, LANE), (0, 2, 1, 3)).reshape(nt, 2, gt * LANE)
    return wt8.astype(BF16), win8.astype(BF16), wst8.astype(BF16), a8


def _s5_matrices(a_re, a_im, log_dt, b_re, b_im, c_re, c_im):
    tc = S5_CHUNK
    dt = jnp.exp(log_dt)[..., None]
    mag = jnp.exp(a_re * dt)
    abr, abi = mag * jnp.cos(a_im * dt), mag * jnp.sin(a_im * dt)
    den = a_re * a_re + a_im * a_im
    cr = ((abr - 1.0) * a_re + abi * a_im) / den
    ci = (abi * a_re - (abr - 1.0) * a_im) / den
    bbr = cr[..., None] * b_re - ci[..., None] * b_im
    bbi = cr[..., None] * b_im + ci[..., None] * b_re
    n = jnp.arange(tc + 1, dtype=F32)[:, None, None, None]
    pmag = jnp.exp(a_re * dt * n)
    pr, pi = pmag * jnp.cos(a_im * dt * n), pmag * jnp.sin(a_im * dt * n)
    car = c_re[None, None] * pr[:, :, :, None, :] - c_im[None, None] * pi[:, :, :, None, :]
    cai = c_re[None, None] * pi[:, :, :, None, :] + c_im[None, None] * pr[:, :, :, None, :]
    kern = (jnp.einsum("ndgcp,dgpe->ndgce", car[:tc], bbr, precision=HIGHEST)
            - jnp.einsum("ndgcp,dgpe->ndgce", cai[:tc], bbi, precision=HIGHEST))
    i_idx = jnp.arange(tc)[:, None]
    j_idx = jnp.arange(tc)[None, :]
    kf = jnp.where((j_idx >= i_idx)[:, :, None, None, None],
                   kern[jnp.clip(j_idx - i_idx, 0, tc - 1), 0], 0.0)
    kb = jnp.where((i_idx >= j_idx)[:, :, None, None, None],
                   kern[jnp.clip(i_idx - j_idx, 0, tc - 1), 1], 0.0)
    g_, c_ = c_re.shape[0], c_re.shape[1]
    wt = jnp.transpose(kf + kb, (2, 0, 4, 1, 3)).reshape(g_, tc * c_, tc * c_)
    ii = jnp.arange(tc)
    pf_r, pf_i = pr[tc - 1 - ii, 0], pi[tc - 1 - ii, 0]
    pb_r, pb_i = pr[ii, 1], pi[ii, 1]

    def in_block(p_r, p_i, d):
        re = p_r[..., None] * bbr[d][None] - p_i[..., None] * bbi[d][None]
        im = p_r[..., None] * bbi[d][None] + p_i[..., None] * bbr[d][None]
        to = lambda t_: jnp.transpose(t_, (1, 0, 3, 2)).reshape(g_, tc * c_, -1)
        return to(re), to(im)

    f_re, f_im = in_block(pf_r, pf_i, 0)
    b_re2, b_im2 = in_block(pb_r, pb_i, 1)
    win = jnp.concatenate([f_re, b_re2, f_im, b_im2], axis=-1)
    jj = jnp.arange(tc)
    to_rows = lambda t_: jnp.transpose(t_, (1, 3, 0, 2)).reshape(g_, -1, tc * c_)
    wst = jnp.concatenate([to_rows(car[jj + 1, 0]), to_rows(car[tc - jj, 1]),
                           to_rows(-cai[jj + 1, 0]), to_rows(-cai[tc - jj, 1])], axis=1)
    avec = jnp.stack([jnp.concatenate([pr[tc, 0], pr[tc, 1]], axis=-1),
                      jnp.concatenate([pi[tc, 0], pi[tc, 1]], axis=-1)], axis=1)
    return wt, win, wst, avec


def _s5_post_kernel(y_ref, u_ref, d_ref, w_ref, o_ref):
    y = jnp.concatenate([y_ref[l] for l in range(y_ref.shape[0])], axis=1)
    y = y + d_ref[...] * u_ref[...].astype(F32)
    k0 = math.sqrt(2.0 / math.pi)
    g = 0.5 * y * (1.0 + jnp.tanh(k0 * (y + 0.044715 * (y * y * y))))
    o_ref[...] = (g * jax.nn.sigmoid(_dot(g.astype(BF16), w_ref[...]))).astype(o_ref.dtype)


def _s5_post(y8, proj, s5_d, w_glu):
    nt, t, _ = y8.shape
    d = nt * LANE
    tm = min(512, t)
    return pl.pallas_call(
        _s5_post_kernel,
        grid=(t // tm,),
        in_specs=[pl.BlockSpec((nt, tm, LANE), lambda i: (0, i, 0)),
                  pl.BlockSpec((tm, d), lambda i: (i, 0)),
                  pl.BlockSpec((1, d), lambda i: (0, 0)),
                  pl.BlockSpec((d, d), lambda i: (0, 0))],
        out_specs=pl.BlockSpec((tm, d), lambda i: (i, 0)),
        out_shape=jax.ShapeDtypeStruct((t, d), BF16),
        compiler_params=_cparams(("parallel",)),
        name="s5_post",
    )(y8, proj, s5_d.reshape(1, d), w_glu)


def _unit_tri_inverses(ms, r, c):
    mm = lambda a, b: _dot(a.astype(BF16), b.astype(BF16))
    eye = jnp.where(r == c, 1.0, 0.0)
    blk = lambda s: (r >> s) == (c >> s)
    mds = [jnp.where(blk(3), m, 0.0) for m in ms]
    m2s = [mm(md, md) for md in mds]
    m4s = [mm(m2, m2) for m2 in m2s]
    ts = [eye - md for md in mds]
    ts = [t + mm(t, m2) for t, m2 in zip(ts, m2s)]
    ts = [t + mm(t, m4) for t, m4 in zip(ts, m4s)]
    for s in (3, 4, 5):
        off = blk(s + 1) & jnp.logical_not(blk(s))
        xs = [mm(jnp.where(off, m, 0.0), t) for m, t in zip(ms, ts)]
        ts = [t - mm(t, x) for t, x in zip(ts, xs)]
    return ts


def _dn_kernel(q_ref, k_ref, v_ref, z_ref, col_ref, cwq_ref, cwk_ref, cwv_ref, nw_ref, s0_ref,
               o_ref, sfin_ref, xp_ref, qs_ref, ks_ref, vs_ref, u_ref, w_ref, a_ref, qg_ref,
               kt_ref, el_ref, od_ref, s_ref, *, seq, n_heads, hp, ca):
    c_sz = DN_CHUNK
    hd = DN_HEAD_DIM
    pad = SUBLANE
    rt = min(256, seq)
    n_chunks = seq // c_sz
    head0 = pl.program_id(1) * hp

    def prep(x_ref, cw_ref, dst_ref, normalise, scale):
        zeros = jnp.zeros((pad, hp * hd), F32)
        xp_ref[0:pad, :] = zeros
        xp_ref[seq + pad:seq + 2 * pad, :] = zeros
        xp_ref[pad:seq + pad, :] = x_ref[...].astype(F32)
        w = cw_ref[...]
        for t in range(seq // rt):
            r0 = t * rt
            acc = (w[0:1, :] * xp_ref[r0 + pad - 1:r0 + pad - 1 + rt, :]
                   + w[1:2, :] * xp_ref[r0 + pad:r0 + pad + rt, :]
                   + w[2:3, :] * xp_ref[r0 + pad + 1:r0 + pad + 1 + rt, :])
            y = _silu(acc)
            if normalise:
                parts = []
                for hl in range(hp):
                    yh = y[:, hl * hd:(hl + 1) * hd]
                    parts.append(yh * (lax.rsqrt(jnp.sum(yh * yh, axis=-1, keepdims=True) + EPS)
                                       * scale))
                y = jnp.concatenate(parts, axis=1)
            dst_ref[r0:r0 + rt, :] = y

    prep(q_ref, cwq_ref, qs_ref, True, hd ** -0.5)
    prep(k_ref, cwk_ref, ks_ref, True, 1.0)
    prep(v_ref, cwv_ref, vs_ref, False, 1.0)

    r = _iota((c_sz, c_sz), 0)
    c = _iota((c_sz, c_sz), 1)
    lane = _iota((c_sz, LANE), 1)

    def pick(col, idx):
        v = jnp.sum(jnp.where(lane == idx, col, 0.0), axis=-1, keepdims=True)
        return jnp.broadcast_to(v, (c_sz, LANE))

    def phase_a(it, carry):
        pairs = []
        for cc in range(ca):
            chunk = it * ca + cc
            rows = pl.ds(pl.multiple_of(chunk * c_sz, c_sz), c_sz)
            col = col_ref[rows, :]
            for hl in range(hp):
                hs = slice(hl * hd, (hl + 1) * hd)
                pairs.append((chunk, rows, hl, col, ks_ref[rows, hs], qs_ref[rows, hs],
                              vs_ref[rows, hs]))
        k16s = [p[4].astype(BF16) for p in pairs]
        kkts = [_dot_nt(k16, k16) for k16 in k16s]
        qkts = [_dot_nt(p[5].astype(BF16), k16) for p, k16 in zip(pairs, k16s)]
        chains = []
        for pi, (chunk, rows, hl, col, kc, qc, vc) in enumerate(pairs):
            for direction in range(2):
                lower = direction == 0
                bc = pick(col, head0 + hl + direction * n_heads)
                gc = pick(col, head0 + hl + (2 + direction) * n_heads)
                grow = jnp.concatenate([gc, gc], axis=0).T[0:c_sz, 0:c_sz]
                incl = (r >= c) if lower else (r <= c)
                strict = (r > c) if lower else (r < c)
                decay = jnp.where(incl, jnp.exp(jnp.minimum(gc[:, 0:c_sz] - grow, 0.0)), 0.0)
                m = jnp.where(strict, bc[:, 0:c_sz] * kkts[pi] * decay, 0.0)
                chains.append((pi, direction, bc, gc, decay, m))
        tinvs = _unit_tri_inverses([ch[5] for ch in chains], r, c)
        egs = [jnp.exp(ch[3]) for ch in chains]
        uws = []
        for (pi, direction, bc, gc, decay, m), tinv, eg in zip(chains, tinvs, egs):
            kc, vc = pairs[pi][4], pairs[pi][6]
            rhs = jnp.concatenate([vc * bc, kc * bc * eg], axis=1).astype(BF16)
            uws.append(_dot(tinv.astype(BF16), rhs))
        for (pi, direction, bc, gc, decay, m), eg, uw in zip(chains, egs, uws):
            chunk, rows, hl, col, kc, qc, vc = pairs[pi]
            idx = hl * 2 + direction
            g_last = gc[c_sz - 1:c_sz, :] if direction == 0 else gc[0:1, :]
            u_ref[idx, rows, :] = uw[:, 0:hd]
            w_ref[idx, rows, :] = uw[:, hd:2 * hd].astype(BF16)
            a_ref[idx, rows, :] = (qkts[pi] * decay).astype(BF16)
            qg_ref[idx, rows, :] = (qc * eg).astype(BF16)
            kt_ref[idx, rows, :] = (kc * jnp.exp(g_last - gc)).astype(BF16)
            el_ref[idx, pl.ds(pl.multiple_of(chunk * SUBLANE, SUBLANE), SUBLANE), :] = (
                jnp.broadcast_to(jnp.exp(g_last), (SUBLANE, LANE)))
        return carry

    lax.fori_loop(0, n_chunks // ca, phase_a, 0)

    for hl in range(hp):
        for direction in range(2):
            s_ref[hl * 2 + direction] = s0_ref[direction, hl]

    def phase_b(step, carry):
        ids, rows, erows = [], [], []
        for hl in range(hp):
            for direction in range(2):
                chunk = step if direction == 0 else n_chunks - 1 - step
                ids.append(hl * 2 + direction)
                rows.append(pl.ds(pl.multiple_of(chunk * c_sz, c_sz), c_sz))
                erows.append(pl.ds(pl.multiple_of(chunk * SUBLANE, SUBLANE), SUBLANE))
        ss = [s_ref[i] for i in ids]
        s16s = [s.astype(BF16) for s in ss]
        wss = [_dot(w_ref[i, rw, :], s16) for i, rw, s16 in zip(ids, rows, s16s)]
        qss = [_dot(qg_ref[i, rw, :], s16) for i, rw, s16 in zip(ids, rows, s16s)]
        vns = [(u_ref[i, rw, :] - ws).astype(BF16) for i, rw, ws in zip(ids, rows, wss)]
        avs = [_dot(a_ref[i, rw, :], vn) for i, rw, vn in zip(ids, rows, vns)]
        kvs = [_dot_tn(kt_ref[i, rw, :], vn) for i, rw, vn in zip(ids, rows, vns)]
        for i, rw, er, s, qs_, av, kv in zip(ids, rows, erows, ss, qss, avs, kvs):
            od_ref[i, rw, :] = qs_ + av
            s_ref[i] = s * el_ref[i, er, :][0:1, :] + kv
        return carry

    lax.fori_loop(0, n_chunks, phase_b, 0)

    for hl in range(hp):
        for direction in range(2):
            sfin_ref[direction, hl] = s_ref[hl * 2 + direction]

    for t in range(seq // rt):
        r0 = t * rt
        for hl in range(hp):
            hs = slice(hl * hd, (hl + 1) * hd)
            o = od_ref[hl * 2, r0:r0 + rt, :] + od_ref[hl * 2 + 1, r0:r0 + rt, :]
            o = o * lax.rsqrt(jnp.mean(o * o, axis=-1, keepdims=True) + EPS) * nw_ref[...]
            o_ref[r0:r0 + rt, hs] = (o * _silu(z_ref[r0:r0 + rt, hs].astype(F32))).astype(o_ref.dtype)


def _deltanet(proj, col, conv_w, norm_w, s0, *, batch, seq, n_heads, col_q, col_z):
    t = proj.shape[0]
    hd = DN_HEAD_DIM
    hp = 2
    d = n_heads * hd
    wd = hp * hd
    n_chunks = seq // DN_CHUNK
    ca = min(4, n_chunks)
    assert n_heads % hp == 0 and col_q % hp == 0 and col_z % hp == 0 and n_chunks % ca == 0
    blk = lambda off: pl.BlockSpec((seq, wd), lambda b, h: (b, off // hp + h))
    cw = lambda off: pl.BlockSpec((3, wd), lambda b, h: (0, off // hp + h))
    st = pl.BlockSpec((None, 2, hp, hd, hd), lambda b, h: (b, 0, h, 0, 0))
    nst = 2 * hp
    return pl.pallas_call(
        functools.partial(_dn_kernel, seq=seq, n_heads=n_heads, hp=hp, ca=ca),
        grid=(batch, n_heads // hp),
        in_specs=[blk(col_q), blk(col_q + n_heads), blk(col_q + 2 * n_heads), blk(col_z),
                  pl.BlockSpec((seq, LANE), lambda b, h: (b, 0)),
                  cw(0), cw(n_heads), cw(2 * n_heads),
                  pl.BlockSpec((1, hd), lambda b, h: (0, 0)),
                  st],
        out_specs=[pl.BlockSpec((seq, wd), lambda b, h: (b, h)), st],
        out_shape=[jax.ShapeDtypeStruct((t, d), BF16),
                   jax.ShapeDtypeStruct(s0.shape, F32)],
        scratch_shapes=[pltpu.VMEM((seq + 2 * SUBLANE, wd), F32),
                        pltpu.VMEM((seq, wd), F32), pltpu.VMEM((seq, wd), F32),
                        pltpu.VMEM((seq, wd), F32),
                        pltpu.VMEM((nst, seq, hd), F32),
                        pltpu.VMEM((nst, seq, hd), BF16),
                        pltpu.VMEM((nst, seq, DN_CHUNK), BF16),
                        pltpu.VMEM((nst, seq, hd), BF16),
                        pltpu.VMEM((nst, seq, hd), BF16),
                        pltpu.VMEM((nst, n_chunks * SUBLANE, LANE), F32),
                        pltpu.VMEM((nst, seq, hd), F32),
                        pltpu.VMEM((nst, hd, hd), F32)],
        compiler_params=_cparams(("parallel", "parallel")),
        name="deltanet",
    )(proj, proj, proj, proj, col, conv_w, conv_w, conv_w, norm_w.reshape(1, hd), s0)


def _conv_module_kernel(a_ref, g_ref, w_ref, b_ref, lnw_ref, lnb_ref, o_ref, yp_ref, tmp_ref,
                        sh_ref, *, seq, taps):
    d = a_ref.shape[1]
    half = taps // 2
    pad = 2 * SUBLANE
    assert half < pad
    rt = CV_ROW_TILE
    ct = CV_COL_TILE
    zeros = jnp.zeros((pad, d), F32)
    yp_ref[0:pad, :] = zeros
    yp_ref[seq + pad:seq + 2 * pad, :] = zeros

    def fill(t, carry):
        r0 = pl.multiple_of(t * rt, rt)
        a = a_ref[pl.ds(r0, rt), :].astype(F32)
        g = g_ref[pl.ds(r0, rt), :].astype(F32)
        yp_ref[pl.ds(r0 + pad, rt), :] = a * jax.nn.sigmoid(g)
        return carry

    lax.fori_loop(0, seq // rt, fill, 0)

    def tile(t, carry):
        r0 = pl.multiple_of(t * rt, rt)
        for cc in range(d // ct):
            cs = slice(cc * ct, (cc + 1) * ct)
            win = yp_ref[pl.ds(r0, rt + 2 * pad), cs]
            span = rt + 2 * pad - SUBLANE
            for m in range(1, SUBLANE):
                sh_ref[m - 1] = win[m:m + span]
            acc = jnp.zeros((rt, ct), F32) + b_ref[:, cs]
            for j in range(taps):
                off = pad - half + j
                lo = (off // SUBLANE) * SUBLANE
                m = off - lo
                if m == 0:
                    rows = yp_ref[pl.ds(pl.multiple_of(r0 + lo, SUBLANE), rt), cs]
                else:
                    rows = sh_ref[m - 1, lo:lo + rt, :]
                acc = acc + w_ref[j:j + 1, cs] * rows
            tmp_ref[:, cs] = acc
        y = tmp_ref[...]
        mu = jnp.mean(y, axis=-1, keepdims=True)
        yc = y - mu
        var = jnp.mean(yc * yc, axis=-1, keepdims=True)
        y = yc * lax.rsqrt(var + EPS) * lnw_ref[...] + lnb_ref[...]
        o_ref[pl.ds(r0, rt), :] = _silu(y).astype(o_ref.dtype)
        return carry

    lax.fori_loop(0, seq // rt, tile, 0)


def _conv_module(proj, w, b, lnw, lnb, *, batch, seq, col_a):
    t = proj.shape[0]
    taps, d = w.shape
    return pl.pallas_call(
        functools.partial(_conv_module_kernel, seq=seq, taps=taps),
        grid=(batch,),
        in_specs=[pl.BlockSpec((seq, d), lambda i: (i, col_a)),
                  pl.BlockSpec((seq, d), lambda i: (i, col_a + 1)),
                  pl.BlockSpec((taps, d), lambda i: (0, 0)),
                  pl.BlockSpec((1, d), lambda i: (0, 0)),
                  pl.BlockSpec((1, d), lambda i: (0, 0)),
                  pl.BlockSpec((1, d), lambda i: (0, 0))],
        out_specs=pl.BlockSpec((seq, d), lambda i: (i, 0)),
        out_shape=jax.ShapeDtypeStruct((t, d), BF16),
        scratch_shapes=[pltpu.VMEM((seq + 4 * SUBLANE, d), F32),
                        pltpu.VMEM((CV_ROW_TILE, d), F32),
                        pltpu.VMEM((SUBLANE - 1, CV_ROW_TILE + 3 * SUBLANE, CV_COL_TILE), F32)],
        compiler_params=_cparams(("parallel",)),
        name="conv_module",
    )(proj, proj, w, b.reshape(1, d), lnw.reshape(1, d), lnb.reshape(1, d))


def _merge_kernel(s5_ref, dn_ref, cv_ref, g0_ref, g1_ref, g2_ref, w0_ref, w1_ref, w2_ref,
                  wo_ref, x_ref, gate_ref, o_ref):
    merged = (jax.nn.sigmoid(g0_ref[...].astype(F32)) * _dot(s5_ref[...], w0_ref[...])
              + jax.nn.sigmoid(g1_ref[...].astype(F32)) * _dot(dn_ref[...], w1_ref[...])
              + jax.nn.sigmoid(g2_ref[...].astype(F32)) * _dot(cv_ref[...], w2_ref[...]))
    y = _dot(merged.astype(BF16), wo_ref[...])
    o_ref[...] = x_ref[...] + gate_ref[0] * y


def _merge(s5_out, dn_out, cv_out, proj, w_s5, w_dn, w_cv, w_out, x2, mod3, gate_idx,
           *, rows_per_batch, fixed_row, col_gate):
    t, d = x2.shape
    tm = min(512, t if fixed_row is not None else rows_per_batch)
    assert t % tm == 0
    row = _row_of_tile(tm, rows_per_batch, fixed_row)
    act = pl.BlockSpec((tm, d), lambda i: (i, 0))
    gat = lambda k: pl.BlockSpec((tm, d), lambda i: (i, col_gate + k))
    wsp = pl.BlockSpec((d, d), lambda i: (0, 0))
    return pl.pallas_call(
        _merge_kernel,
        grid=(t // tm,),
        in_specs=[act, act, act, gat(0), gat(1), gat(2), wsp, wsp, wsp, wsp, act,
                  pl.BlockSpec((None, 1, d), lambda i: (row(i), 0, gate_idx))],
        out_specs=act,
        out_shape=jax.ShapeDtypeStruct((t, d), F32),
        compiler_params=_cparams(("parallel",)),
        name="merge",
    )(s5_out, dn_out, cv_out, proj, proj, proj, w_s5, w_dn, w_cv, w_out, x2, mod3)


def _ffn_conv_kernel(a_ref, v_ref, w_ref, b_ref, o_ref, a0_ref, al_ref, ar_ref, *, seq, grid2d):
    ct = a_ref.shape[1]
    pad = 72 if grid2d else SUBLANE
    rt = 128
    zeros = jnp.zeros((pad, ct), F32)
    for buf in (a0_ref, al_ref, ar_ref):
        buf[0:pad, :] = zeros
        buf[seq + pad:seq + 2 * pad, :] = zeros

    def fill(t, carry):
        r0 = pl.multiple_of(t * rt, rt)
        a0_ref[pl.ds(r0 + pad, rt), :] = a_ref[pl.ds(r0, rt), :].astype(F32)
        return carry

    lax.fori_loop(0, seq // rt, fill, 0)

    def neighbours(t, carry):
        r0 = pl.multiple_of(t * rt, rt)
        left = _shifted_rows(a0_ref, r0 + pad, -1, rt, slice(None))
        right = _shifted_rows(a0_ref, r0 + pad, 1, rt, slice(None))
        if grid2d:
            colpos = (_iota((rt, ct), 0) + r0) & (GRID_W - 1)
            left = jnp.where(colpos == 0, 0.0, left)
            right = jnp.where(colpos == GRID_W - 1, 0.0, right)
        al_ref[pl.ds(r0 + pad, rt), :] = left
        ar_ref[pl.ds(r0 + pad, rt), :] = right
        return carry

    lax.fori_loop(0, seq // rt, neighbours, 0)

    if grid2d:
        taps = [(dr, dc) for dr in (-1, 0, 1) for dc in (-1, 0, 1)]
    else:
        taps = [(0, dc) for dc in (-1, 0, 1)]
    src = {-1: al_ref, 0: a0_ref, 1: ar_ref}

    def tile(t, carry):
        r0 = pl.multiple_of(t * rt, rt)
        acc = jnp.zeros((rt, ct), F32) + b_ref[...]
        for dr, dc in taps:
            widx = (dr + 1) * 3 + (dc + 1)
            rows = pl.ds(pl.multiple_of(r0 + pad + dr * GRID_W, SUBLANE), rt)
            acc = acc + w_ref[widx:widx + 1, :] * src[dc][rows, :]
        o_ref[pl.ds(r0, rt), :] = (_silu(acc) * v_ref[pl.ds(r0, rt), :].astype(F32)).astype(o_ref.dtype)
        return carry

    lax.fori_loop(0, seq // rt, tile, 0)


def _ffn_conv(av, w9, bias, *, batch, seq, grid2d):
    t = av.shape[0]
    f = w9.shape[1]
    ct = 256
    assert f % ct == 0 and seq % 128 == 0
    nct = f // ct
    pad = 72 if grid2d else SUBLANE
    return pl.pallas_call(
        functools.partial(_ffn_conv_kernel, seq=seq, grid2d=grid2d),
        grid=(batch, nct),
        in_specs=[pl.BlockSpec((seq, ct), lambda b, c: (b, c)),
                  pl.BlockSpec((seq, ct), lambda b, c: (b, nct + c)),
                  pl.BlockSpec((9, ct), lambda b, c: (0, c)),
                  pl.BlockSpec((1, ct), lambda b, c: (0, c))],
        out_specs=pl.BlockSpec((seq, ct), lambda b, c: (b, c)),
        out_shape=jax.ShapeDtypeStruct((t, f), BF16),
        scratch_shapes=[pltpu.VMEM((seq + 2 * pad, ct), F32)] * 3,
        compiler_params=_cparams(("parallel", "parallel")),
        name="ffn_conv",
    )(av, av, w9, bias.reshape(1, f))


def _resid_matmul_kernel(*refs, final_norm):
    if final_norm:
        a_ref, w_ref, x_ref, gate_ref, fw_ref, o_ref = refs
    else:
        a_ref, w_ref, x_ref, gate_ref, o_ref = refs
    y = x_ref[...] + gate_ref[0] * _dot(a_ref[...], w_ref[...])
    if final_norm:
        y = y * lax.rsqrt(jnp.mean(y * y, axis=-1, keepdims=True) + EPS) * fw_ref[...]
    o_ref[...] = y


def _resid_matmul(a, w, x2, mod3, gate_idx, *, rows_per_batch, fixed_row, final_w=None):
    t, d = x2.shape
    k = a.shape[1]
    tm = min(512, t if fixed_row is not None else rows_per_batch)
    assert t % tm == 0
    row = _row_of_tile(tm, rows_per_batch, fixed_row)
    final_norm = final_w is not None
    in_specs = [pl.BlockSpec((tm, k), lambda i: (i, 0)),
                pl.BlockSpec((k, d), lambda i: (0, 0)),
                pl.BlockSpec((tm, d), lambda i: (i, 0)),
                pl.BlockSpec((None, 1, d), lambda i: (row(i), 0, gate_idx))]
    args = [a, w, x2, mod3]
    if final_norm:
        in_specs.append(pl.BlockSpec((1, d), lambda i: (0, 0)))
        args.append(final_w.reshape(1, d))
    return pl.pallas_call(
        functools.partial(_resid_matmul_kernel, final_norm=final_norm),
        grid=(t // tm,),
        in_specs=in_specs,
        out_specs=pl.BlockSpec((tm, d), lambda i: (i, 0)),
        out_shape=jax.ShapeDtypeStruct((t, d), F32),
        compiler_params=_cparams(("parallel",)),
        name="resid_matmul",
    )(*args)


def _chunk_rows(u8, batch, seq):
    nt = u8.shape[0]
    nk = seq // S5_CHUNK
    u5 = u8.reshape(nt, batch, nk, S5_CHUNK, LANE)
    return jnp.transpose(u5, (0, 2, 1, 3, 4)).reshape(nt, nk * batch, S5_CHUNK * LANE)


def _token_rows(y_rows, batch, seq):
    nt = y_rows.shape[0]
    nk = seq // S5_CHUNK
    y5 = y_rows.reshape(nt, nk, batch, S5_CHUNK, LANE)
    return jnp.transpose(y5, (0, 2, 1, 3, 4)).reshape(nt, batch * seq, LANE)


def kernel(x, c, ctx, c_ctx, ada_w, ada_b, norm1_w, norm2_w, w_in, s5_a_re, s5_a_im, s5_log_dt, s5_b_re, s5_b_im, s5_c_re, s5_c_im, s5_d, s5_w_glu, dn_conv_w, dn_a_log, dn_dt_bias, dn_norm_w, cv_dw_w, cv_dw_b, cv_ln_w, cv_ln_b, w_br_s5, w_br_dn, w_br_cv, w_out, ffn_w_up, ffn_dw_w, ffn_dw_b, ffn_w_down, final_norm_w):
    batch, seq, d = x.shape
    ctx_len = ctx.shape[1]
    depth = ada_w.shape[0]
    groups = s5_a_re.shape[2]
    n_heads = dn_a_log.shape[2]
    dn_width = n_heads * DN_HEAD_DIM
    ffn_hidden = ffn_dw_b.shape[1]
    assert batch == SUBLANE and batch < MOD_ROWS and d == groups * S5_GROUP == dn_width
    assert d % 1024 == 0 and 4 * n_heads <= LANE

    col_qkv = d
    col_beta = col_qkv + 3 * dn_width
    n_state = col_beta + 4 * n_heads
    col_cv = n_state + dn_width
    col_gate = col_cv + 2 * d
    pk_q = d // LANE
    pk_z = (d + 3 * dn_width) // LANE
    pk_cv = (d + 4 * dn_width) // d
    pk_gate = pk_cv + 2

    cmat = jnp.zeros((MOD_ROWS, d), F32).at[:batch].set(c).at[batch].set(c_ctx)
    xl = x.reshape(batch * seq, d)
    xc = ctx.reshape(batch * ctx_len, d)

    for i in range(depth):
        last = i == depth - 1
        w_main = jnp.concatenate([w_in[i][:, :col_beta], w_in[i][:, n_state:]], axis=1).astype(BF16)
        w_bd = jnp.pad(w_in[i][:, col_beta:n_state], ((0, 0), (0, LANE - 4 * n_heads)))
        aneg = jnp.pad(-jnp.exp(dn_a_log[i].reshape(1, -1)), ((0, 0), (2 * n_heads, LANE - 4 * n_heads)))
        dtb = jnp.pad(dn_dt_bias[i].reshape(1, -1), ((0, 0), (2 * n_heads, LANE - 4 * n_heads)))
        wt8, win8, wst8, a8 = _s5_slab_operators(*_s5_matrices(
            s5_a_re[i], s5_a_im[i], s5_log_dt[i], s5_b_re[i], s5_b_im[i], s5_c_re[i], s5_c_im[i]))
        w_glu = s5_w_glu[i].astype(BF16)
        w_brs = w_br_s5[i].astype(BF16)
        w_brd = w_br_dn[i].astype(BF16)
        w_brc = w_br_cv[i].astype(BF16)
        w_o = w_out[i].astype(BF16)
        w_up = ffn_w_up[i].astype(BF16)
        w_dn = ffn_w_down[i].astype(BF16)
        w9 = ffn_dw_w[i].reshape(9, ffn_hidden)

        mod3 = _modulation(cmat, ada_w[i], ada_b[i]).reshape(MOD_ROWS, 1, N_MOD * d)

        rows_c = dict(rows_per_batch=ctx_len, fixed_row=batch)
        rows_l = dict(rows_per_batch=seq, fixed_row=None)
        proj_c, bd_c, u8_c = _norm_matmul(xc, norm1_w[i], mod3, 0, 1, w_main, wbd=w_bd, tn=d,
                                          **rows_c)
        proj_l, bd_l, u8_l = _norm_matmul(xl, norm1_w[i], mod3, 0, 1, w_main, wbd=w_bd, tn=d,
                                          **rows_l)

        u_rows = jnp.concatenate([_chunk_rows(u8_c, batch, ctx_len),
                                  _chunk_rows(u8_l, batch, seq)], axis=1)
        rows_ctx = (ctx_len // S5_CHUNK) * batch
        segments = ((0, ctx_len // S5_CHUNK), (rows_ctx, seq // S5_CHUNK))
        hp = _s5_state(u_rows, win8, a8, segments=segments, batch=batch)
        y_rows = _s5_out(u_rows, hp, wt8, wst8)

        zero_dn = jnp.zeros((batch, 2, n_heads, DN_HEAD_DIM, DN_HEAD_DIM), F32)
        dn_args = dict(batch=batch, n_heads=n_heads, col_q=pk_q, col_z=pk_z)
        dn_c, dn_state = _deltanet(proj_c, _dn_gates(bd_c, aneg, dtb, n_heads), dn_conv_w[i],
                                   dn_norm_w[i], zero_dn, seq=ctx_len, **dn_args)
        dn_l, _ = _deltanet(proj_l, _dn_gates(bd_l, aneg, dtb, n_heads), dn_conv_w[i],
                            dn_norm_w[i], dn_state, seq=seq, **dn_args)

        def finish(x2, proj, y8, dn_out, length, rows, grid2d, final_w):
            s5_out = _s5_post(y8, proj, s5_d[i], w_glu)
            cv_out = _conv_module(proj, cv_dw_w[i], cv_dw_b[i], cv_ln_w[i], cv_ln_b[i],
                                  batch=batch, seq=length, col_a=pk_cv)
            x2 = _merge(s5_out, dn_out, cv_out, proj, w_brs, w_brd, w_brc, w_o, x2, mod3, 2,
                        col_gate=pk_gate, **rows)
            av = _norm_matmul(x2, norm2_w[i], mod3, 3, 4, w_up, tn=w_up.shape[1] // 4, **rows)
            hidden = _ffn_conv(av, w9, ffn_dw_b[i], batch=batch, seq=length, grid2d=grid2d)
            return _resid_matmul(hidden, w_dn, x2, mod3, 5, final_w=final_w, **rows)

        if not last:
            xc = finish(xc, proj_c, _token_rows(y_rows[:, :rows_ctx], batch, ctx_len), dn_c,
                        ctx_len, rows_c, False, None)
        xl = finish(xl, proj_l, _token_rows(y_rows[:, rows_ctx:], batch, seq), dn_l, seq, rows_l,
                    True, final_norm_w if last else None)

    return xl.reshape(batch, seq, d)
```

```python
import functools
import math

import jax
import jax.numpy as jnp
from jax import lax
from jax.experimental import pallas as pl
from jax.experimental.pallas import tpu as pltpu

F32 = jnp.float32
BF16 = jnp.bfloat16
HIGHEST = lax.Precision.HIGHEST

EPS = 1e-6
GRID_W = 64
S5_GROUP = 16
DN_HEAD_DIM = 128
N_MOD = 6
N_BRANCH = 3

LANE = 128
SUBLANE = 8
S5_CHUNK = 16
DN_CHUNK = 64
MOD_ROWS = 16
CV_ROW_TILE = 64
CV_COL_TILE = 256
VMEM_LIMIT = 56 * 1024 * 1024


def _cparams(sem, vmem=VMEM_LIMIT):
    return pltpu.CompilerParams(dimension_semantics=sem, vmem_limit_bytes=vmem)


def _dot(a, b, precision=None):
    return jnp.dot(a, b, preferred_element_type=F32, precision=precision)


def _dot_nt(a, b, precision=None):
    return lax.dot_general(a, b, (((1,), (1,)), ((), ())), preferred_element_type=F32,
                           precision=precision)


def _dot_tn(a, b, precision=None):
    return lax.dot_general(a, b, (((0,), (0,)), ((), ())), preferred_element_type=F32,
                           precision=precision)


def _silu(x):
    return x * jax.nn.sigmoid(x)


def _iota(shape, dim):
    return lax.broadcasted_iota(jnp.int32, shape, dim)


def _shifted_rows(ref, base, off, rows, cols):
    lo = (off // SUBLANE) * SUBLANE
    shift = off - lo
    start = pl.multiple_of(base + lo, SUBLANE)
    if shift == 0:
        return ref[pl.ds(start, rows), cols]
    return ref[pl.ds(start, rows + SUBLANE), cols][shift:shift + rows]


def _mod_kernel(c_ref, w_ref, b_ref, o_ref):
    o_ref[...] = _dot(_silu(c_ref[...]), w_ref[...], HIGHEST) + b_ref[...]


def _modulation(cmat, ada_w, ada_b):
    rows, d = cmat.shape
    n = ada_w.shape[1]
    tn = 1024
    return pl.pallas_call(
        _mod_kernel,
        grid=(n // tn,),
        in_specs=[pl.BlockSpec((rows, d), lambda j: (0, 0)),
                  pl.BlockSpec((d, tn), lambda j: (0, j)),
                  pl.BlockSpec((1, tn), lambda j: (0, j))],
        out_specs=pl.BlockSpec((rows, tn), lambda j: (0, j)),
        out_shape=jax.ShapeDtypeStruct((rows, n), F32),
        compiler_params=_cparams(("parallel",)),
        name="modulation",
    )(cmat, ada_w, ada_b.reshape(1, n))


def _norm_matmul_kernel(*refs, with_bd):
    if with_bd:
        x_ref, nw_ref, sh_ref, sc_ref, w_ref, wbd_ref, o_ref, bd_ref, h_ref = refs
    else:
        x_ref, nw_ref, sh_ref, sc_ref, w_ref, o_ref, h_ref = refs

    @pl.when(pl.program_id(1) == 0)
    def _():
        x = x_ref[...]
        h = x * lax.rsqrt(jnp.mean(x * x, axis=-1, keepdims=True) + EPS) * nw_ref[...]
        h = h * (1.0 + sc_ref[0]) + sh_ref[0]
        h_ref[...] = h.astype(BF16)
        if with_bd:
            bd_ref[...] = _dot(h, wbd_ref[...], HIGHEST)

    o_ref[...] = _dot(h_ref[...], w_ref[...]).astype(o_ref.dtype)


def _row_of_tile(tm, rows_per_batch, fixed_row):
    if fixed_row is not None:
        return lambda i: fixed_row
    return lambda i: (i * tm) // rows_per_batch


def _norm_matmul(x2, norm_w, mod3, sh_idx, sc_idx, w, *, rows_per_batch, fixed_row, wbd=None,
                 tn):
    t, d = x2.shape
    n = w.shape[1]
    tm = min(1024, t if fixed_row is not None else rows_per_batch)
    assert t % tm == 0 and n % tn == 0
    row = _row_of_tile(tm, rows_per_batch, fixed_row)
    with_bd = wbd is not None
    in_specs = [pl.BlockSpec((tm, d), lambda i, j: (i, 0)),
                pl.BlockSpec((1, d), lambda i, j: (0, 0)),
                pl.BlockSpec((None, 1, d), lambda i, j: (row(i), 0, sh_idx)),
                pl.BlockSpec((None, 1, d), lambda i, j: (row(i), 0, sc_idx)),
                pl.BlockSpec((d, tn), lambda i, j: (0, j))]
    args = [x2, norm_w.reshape(1, d), mod3, mod3, w]
    out_specs = [pl.BlockSpec((tm, tn), lambda i, j: (i, j))]
    out_shape = [jax.ShapeDtypeStruct((t, n), BF16)]
    if with_bd:
        in_specs.append(pl.BlockSpec((d, LANE), lambda i, j: (0, 0)))
        args.append(wbd)
        out_specs.append(pl.BlockSpec((tm, LANE), lambda i, j: (i, 0)))
        out_shape.append(jax.ShapeDtypeStruct((t, LANE), F32))
    res = pl.pallas_call(
        functools.partial(_norm_matmul_kernel, with_bd=with_bd),
        grid=(t // tm, n // tn),
        in_specs=in_specs,
        out_specs=out_specs,
        out_shape=out_shape,
        scratch_shapes=[pltpu.VMEM((tm, d), BF16)],
        compiler_params=_cparams(("parallel", "arbitrary")),
        name="norm_matmul",
    )(*args)
    return res if with_bd else res[0]


def _dn_gates_kernel(bd_ref, aneg_ref, dtb_ref, o_ref, *, n_heads):
    x = bd_ref[...]
    tm = x.shape[0]
    y = x + dtb_ref[...]
    g = aneg_ref[...] * (jnp.maximum(y, 0.0) + jnp.log1p(jnp.exp(-jnp.abs(y))))
    r = _iota((tm, tm), 0)
    c = _iota((tm, tm), 1)
    shift = DN_CHUNK.bit_length() - 1
    same = (r >> shift) == (c >> shift)
    prefix = _dot(jnp.where(same & (c <= r), 1.0, 0.0), g, HIGHEST)
    suffix = _dot(jnp.where(same & (c >= r), 1.0, 0.0), g, HIGHEST)
    lane = _iota(x.shape, 1)
    o_ref[...] = jnp.where(lane < 2 * n_heads, jax.nn.sigmoid(x),
                           jnp.where(lane < 3 * n_heads, prefix, suffix))


def _dn_gates(bd, aneg, dtb, n_heads):
    t = bd.shape[0]
    tm = min(512, t)
    assert t % tm == 0 and tm % DN_CHUNK == 0
    return pl.pallas_call(
        functools.partial(_dn_gates_kernel, n_heads=n_heads),
        grid=(t // tm,),
        in_specs=[pl.BlockSpec((tm, LANE), lambda i: (i, 0)),
                  pl.BlockSpec((1, LANE), lambda i: (0, 0)),
                  pl.BlockSpec((1, LANE), lambda i: (0, 0))],
        out_specs=pl.BlockSpec((tm, LANE), lambda i: (i, 0)),
        out_shape=jax.ShapeDtypeStruct((t, LANE), F32),
        compiler_params=_cparams(("parallel",)),
        name="dn_gates",
    )(bd, aneg, dtb)


def _s5_kernel(u_ref, wt_ref, win_ref, wst_ref, a_ref, y_ref, s_ref, hp_ref, *, segments, batch):
    half = LANE // 2
    n_grp = u_ref.shape[0]
    for g in range(n_grp):
        s_ref[g] = _dot(u_ref[g], win_ref[g])
    ars = [a_ref[g, 0:1, :] for g in range(n_grp)]
    ais = [a_ref[g, 1:2, :] for g in range(n_grp)]
    is_fwd = _iota((batch, LANE), 1) < half
    carry = tuple(jnp.zeros((batch, LANE), F32) for _ in range(2 * n_grp))
    for row0, n_chunks in segments:
        def body(s, carry, row0=row0, n_chunks=n_chunks):
            rf = pl.ds(pl.multiple_of(row0 + s * batch, batch), batch)
            rb = pl.ds(pl.multiple_of(row0 + (n_chunks - 1 - s) * batch, batch), batch)
            new = []
            for g in range(n_grp):
                hre, him = carry[2 * g], carry[2 * g + 1]
                hp_ref[g, rf, 0:half] = hre[:, 0:half]
                hp_ref[g, rb, half:LANE] = hre[:, half:LANE]
                hp_ref[g, rf, LANE:LANE + half] = him[:, 0:half]
                hp_ref[g, rb, LANE + half:2 * LANE] = him[:, half:LANE]
                sf = s_ref[g, rf, :]
                sb = s_ref[g, rb, :]
                sre = jnp.where(is_fwd, sf[:, 0:LANE], sb[:, 0:LANE])
                sim = jnp.where(is_fwd, sf[:, LANE:2 * LANE], sb[:, LANE:2 * LANE])
                new.append(ars[g] * hre - ais[g] * him + sre)
                new.append(ars[g] * him + ais[g] * hre + sim)
            return tuple(new)

        carry = lax.fori_loop(0, n_chunks, body, carry)
    for g in range(n_grp):
        y = _dot(u_ref[g], wt_ref[g]) + _dot(hp_ref[g].astype(BF16), wst_ref[g])
        y_ref[g] = y.astype(y_ref.dtype)


S5_GROUPS_PER_STEP = 4


def _s5_core(ug, wt, win, wst, avec, *, segments, batch):
    g, rows, k = ug.shape
    gb = S5_GROUPS_PER_STEP
    assert batch == SUBLANE and k == 2 * LANE and g % gb == 0
    mat = pl.BlockSpec((gb, k, k), lambda i: (i, 0, 0))
    act = pl.BlockSpec((gb, rows, k), lambda i: (i, 0, 0))
    return pl.pallas_call(
        functools.partial(_s5_kernel, segments=segments, batch=batch),
        grid=(g // gb,),
        in_specs=[act, mat, mat, mat, pl.BlockSpec((gb, 2, LANE), lambda i: (i, 0, 0))],
        out_specs=act,
        out_shape=jax.ShapeDtypeStruct((g, rows, k), BF16),
        scratch_shapes=[pltpu.VMEM((gb, rows, k), F32), pltpu.VMEM((gb, rows, k), F32)],
        compiler_params=_cparams(("parallel",)),
        name="s5_core",
    )(ug, wt, win, wst, avec)


def _s5_matrices(a_re, a_im, log_dt, b_re, b_im, c_re, c_im):
    tc = S5_CHUNK
    dt = jnp.exp(log_dt)[..., None]
    mag = jnp.exp(a_re * dt)
    abr, abi = mag * jnp.cos(a_im * dt), mag * jnp.sin(a_im * dt)
    den = a_re * a_re + a_im * a_im
    cr = ((abr - 1.0) * a_re + abi * a_im) / den
    ci = (abi * a_re - (abr - 1.0) * a_im) / den
    bbr = cr[..., None] * b_re - ci[..., None] * b_im
    bbi = cr[..., None] * b_im + ci[..., None] * b_re
    n = jnp.arange(tc + 1, dtype=F32)[:, None, None, None]
    pmag = jnp.exp(a_re * dt * n)
    pr, pi = pmag * jnp.cos(a_im * dt * n), pmag * jnp.sin(a_im * dt * n)
    car = c_re[None, None] * pr[:, :, :, None, :] - c_im[None, None] * pi[:, :, :, None, :]
    cai = c_re[None, None] * pi[:, :, :, None, :] + c_im[None, None] * pr[:, :, :, None, :]
    kern = (jnp.einsum("ndgcp,dgpe->ndgce", car[:tc], bbr, precision=HIGHEST)
            - jnp.einsum("ndgcp,dgpe->ndgce", cai[:tc], bbi, precision=HIGHEST))
    i_idx = jnp.arange(tc)[:, None]
    j_idx = jnp.arange(tc)[None, :]
    kf = jnp.where((j_idx >= i_idx)[:, :, None, None, None],
                   kern[jnp.clip(j_idx - i_idx, 0, tc - 1), 0], 0.0)
    kb = jnp.where((i_idx >= j_idx)[:, :, None, None, None],
                   kern[jnp.clip(i_idx - j_idx, 0, tc - 1), 1], 0.0)
    g_, c_ = c_re.shape[0], c_re.shape[1]
    wt = jnp.transpose(kf + kb, (2, 0, 4, 1, 3)).reshape(g_, tc * c_, tc * c_)
    ii = jnp.arange(tc)
    pf_r, pf_i = pr[tc - 1 - ii, 0], pi[tc - 1 - ii, 0]
    pb_r, pb_i = pr[ii, 1], pi[ii, 1]

    def in_block(p_r, p_i, d):
        re = p_r[..., None] * bbr[d][None] - p_i[..., None] * bbi[d][None]
        im = p_r[..., None] * bbi[d][None] + p_i[..., None] * bbr[d][None]
        to = lambda t_: jnp.transpose(t_, (1, 0, 3, 2)).reshape(g_, tc * c_, -1)
        return to(re), to(im)

    f_re, f_im = in_block(pf_r, pf_i, 0)
    b_re2, b_im2 = in_block(pb_r, pb_i, 1)
    win = jnp.concatenate([f_re, b_re2, f_im, b_im2], axis=-1)
    jj = jnp.arange(tc)
    to_rows = lambda t_: jnp.transpose(t_, (1, 3, 0, 2)).reshape(g_, -1, tc * c_)
    wst = jnp.concatenate([to_rows(car[jj + 1, 0]), to_rows(car[tc - jj, 1]),
                           to_rows(-cai[jj + 1, 0]), to_rows(-cai[tc - jj, 1])], axis=1)
    avec = jnp.stack([jnp.concatenate([pr[tc, 0], pr[tc, 1]], axis=-1),
                      jnp.concatenate([pi[tc, 0], pi[tc, 1]], axis=-1)], axis=1)
    return wt, win, wst, avec


def _s5_post_kernel(y_ref, u_ref, d_ref, w_ref, o_ref):
    y = y_ref[...].astype(F32) + d_ref[...] * u_ref[...].astype(F32)
    k0 = math.sqrt(2.0 / math.pi)
    g = 0.5 * y * (1.0 + jnp.tanh(k0 * (y + 0.044715 * (y * y * y))))
    o_ref[...] = (g * jax.nn.sigmoid(_dot(g.astype(BF16), w_ref[...]))).astype(o_ref.dtype)


def _s5_post(y, proj, s5_d, w_glu):
    t, d = y.shape
    tm = min(512, t)
    return pl.pallas_call(
        _s5_post_kernel,
        grid=(t // tm,),
        in_specs=[pl.BlockSpec((tm, d), lambda i: (i, 0)),
                  pl.BlockSpec((tm, d), lambda i: (i, 0)),
                  pl.BlockSpec((1, d), lambda i: (0, 0)),
                  pl.BlockSpec((d, d), lambda i: (0, 0))],
        out_specs=pl.BlockSpec((tm, d), lambda i: (i, 0)),
        out_shape=jax.ShapeDtypeStruct((t, d), BF16),
        compiler_params=_cparams(("parallel",)),
        name="s5_post",
    )(y, proj, s5_d.reshape(1, d), w_glu)


def _unit_tri_inverses(ms, r, c):
    mm = lambda a, b: _dot(a.astype(BF16), b.astype(BF16))
    eye = jnp.where(r == c, 1.0, 0.0)
    blk = lambda s: (r >> s) == (c >> s)
    mds = [jnp.where(blk(3), m, 0.0) for m in ms]
    m2s = [mm(md, md) for md in mds]
    m4s = [mm(m2, m2) for m2 in m2s]
    ts = [eye - md for md in mds]
    ts = [t + mm(t, m2) for t, m2 in zip(ts, m2s)]
    ts = [t + mm(t, m4) for t, m4 in zip(ts, m4s)]
    for s in (3, 4, 5):
        off = blk(s + 1) & jnp.logical_not(blk(s))
        xs = [mm(jnp.where(off, m, 0.0), t) for m, t in zip(ms, ts)]
        ts = [t - mm(t, x) for t, x in zip(ts, xs)]
    return ts


def _dn_kernel(q_ref, k_ref, v_ref, z_ref, col_ref, cwq_ref, cwk_ref, cwv_ref, nw_ref, s0_ref,
               o_ref, sfin_ref, xp_ref, qs_ref, ks_ref, vs_ref, u_ref, w_ref, a_ref, qg_ref,
               kt_ref, el_ref, od_ref, s_ref, *, seq, n_heads, hp, ca):
    c_sz = DN_CHUNK
    hd = DN_HEAD_DIM
    pad = SUBLANE
    rt = min(256, seq)
    n_chunks = seq // c_sz
    head0 = pl.program_id(1) * hp

    def prep(x_ref, cw_ref, dst_ref, normalise, scale):
        zeros = jnp.zeros((pad, hp * hd), F32)
        xp_ref[0:pad, :] = zeros
        xp_ref[seq + pad:seq + 2 * pad, :] = zeros
        xp_ref[pad:seq + pad, :] = x_ref[...].astype(F32)
        w = cw_ref[...]
        for t in range(seq // rt):
            r0 = t * rt
            acc = (w[0:1, :] * xp_ref[r0 + pad - 1:r0 + pad - 1 + rt, :]
                   + w[1:2, :] * xp_ref[r0 + pad:r0 + pad + rt, :]
                   + w[2:3, :] * xp_ref[r0 + pad + 1:r0 + pad + 1 + rt, :])
            y = _silu(acc)
            if normalise:
                parts = []
                for hl in range(hp):
                    yh = y[:, hl * hd:(hl + 1) * hd]
                    parts.append(yh * (lax.rsqrt(jnp.sum(yh * yh, axis=-1, keepdims=True) + EPS)
                                       * scale))
                y = jnp.concatenate(parts, axis=1)
            dst_ref[r0:r0 + rt, :] = y

    prep(q_ref, cwq_ref, qs_ref, True, hd ** -0.5)
    prep(k_ref, cwk_ref, ks_ref, True, 1.0)
    prep(v_ref, cwv_ref, vs_ref, False, 1.0)

    r = _iota((c_sz, c_sz), 0)
    c = _iota((c_sz, c_sz), 1)
    lane = _iota((c_sz, LANE), 1)

    def pick(col, idx):
        v = jnp.sum(jnp.where(lane == idx, col, 0.0), axis=-1, keepdims=True)
        return jnp.broadcast_to(v, (c_sz, LANE))

    def phase_a(it, carry):
        pairs = []
        for cc in range(ca):
            chunk = it * ca + cc
            rows = pl.ds(pl.multiple_of(chunk * c_sz, c_sz), c_sz)
            col = col_ref[rows, :]
            for hl in range(hp):
                hs = slice(hl * hd, (hl + 1) * hd)
                pairs.append((chunk, rows, hl, col, ks_ref[rows, hs], qs_ref[rows, hs],
                              vs_ref[rows, hs]))
        k16s = [p[4].astype(BF16) for p in pairs]
        kkts = [_dot_nt(k16, k16) for k16 in k16s]
        qkts = [_dot_nt(p[5].astype(BF16), k16) for p, k16 in zip(pairs, k16s)]
        chains = []
        for pi, (chunk, rows, hl, col, kc, qc, vc) in enumerate(pairs):
            for direction in range(2):
                lower = direction == 0
                bc = pick(col, head0 + hl + direction * n_heads)
                gc = pick(col, head0 + hl + (2 + direction) * n_heads)
                grow = jnp.concatenate([gc, gc], axis=0).T[0:c_sz, 0:c_sz]
                incl = (r >= c) if lower else (r <= c)
                strict = (r > c) if lower else (r < c)
                decay = jnp.where(incl, jnp.exp(jnp.minimum(gc[:, 0:c_sz] - grow, 0.0)), 0.0)
                m = jnp.where(strict, bc[:, 0:c_sz] * kkts[pi] * decay, 0.0)
                chains.append((pi, direction, bc, gc, decay, m))
        tinvs = _unit_tri_inverses([ch[5] for ch in chains], r, c)
        egs = [jnp.exp(ch[3]) for ch in chains]
        uws = []
        for (pi, direction, bc, gc, decay, m), tinv, eg in zip(chains, tinvs, egs):
            kc, vc = pairs[pi][4], pairs[pi][6]
            rhs = jnp.concatenate([vc * bc, kc * bc * eg], axis=1).astype(BF16)
            uws.append(_dot(tinv.astype(BF16), rhs))
        for (pi, direction, bc, gc, decay, m), eg, uw in zip(chains, egs, uws):
            chunk, rows, hl, col, kc, qc, vc = pairs[pi]
            idx = hl * 2 + direction
            g_last = gc[c_sz - 1:c_sz, :] if direction == 0 else gc[0:1, :]
            u_ref[idx, rows, :] = uw[:, 0:hd]
            w_ref[idx, rows, :] = uw[:, hd:2 * hd].astype(BF16)
            a_ref[idx, rows, :] = (qkts[pi] * decay).astype(BF16)
            qg_ref[idx, rows, :] = (qc * eg).astype(BF16)
            kt_ref[idx, rows, :] = (kc * jnp.exp(g_last - gc)).astype(BF16)
            el_ref[idx, pl.ds(pl.multiple_of(chunk * SUBLANE, SUBLANE), SUBLANE), :] = (
                jnp.broadcast_to(jnp.exp(g_last), (SUBLANE, LANE)))
        return carry

    lax.fori_loop(0, n_chunks // ca, phase_a, 0)

    for hl in range(hp):
        for direction in range(2):
            s_ref[hl * 2 + direction] = s0_ref[direction, hl]

    def phase_b(step, carry):
        ids, rows, erows = [], [], []
        for hl in range(hp):
            for direction in range(2):
                chunk = step if direction == 0 else n_chunks - 1 - step
                ids.append(hl * 2 + direction)
                rows.append(pl.ds(pl.multiple_of(chunk * c_sz, c_sz), c_sz))
                erows.append(pl.ds(pl.multiple_of(chunk * SUBLANE, SUBLANE), SUBLANE))
        ss = [s_ref[i] for i in ids]
        s16s = [s.astype(BF16) for s in ss]
        wss = [_dot(w_ref[i, rw, :], s16) for i, rw, s16 in zip(ids, rows, s16s)]
        qss = [_dot(qg_ref[i, rw, :], s16) for i, rw, s16 in zip(ids, rows, s16s)]
        vns = [(u_ref[i, rw, :] - ws).astype(BF16) for i, rw, ws in zip(ids, rows, wss)]
        avs = [_dot(a_ref[i, rw, :], vn) for i, rw, vn in zip(ids, rows, vns)]
        kvs = [_dot_tn(kt_ref[i, rw, :], vn) for i, rw, vn in zip(ids, rows, vns)]
        for i, rw, er, s, qs_, av, kv in zip(ids, rows, erows, ss, qss, avs, kvs):
            od_ref[i, rw, :] = qs_ + av
            s_ref[i] = s * el_ref[i, er, :][0:1, :] + kv
        return carry

    lax.fori_loop(0, n_chunks, phase_b, 0)

    for hl in range(hp):
        for direction in range(2):
            sfin_ref[direction, hl] = s_ref[hl * 2 + direction]

    for t in range(seq // rt):
        r0 = t * rt
        for hl in range(hp):
            hs = slice(hl * hd, (hl + 1) * hd)
            o = od_ref[hl * 2, r0:r0 + rt, :] + od_ref[hl * 2 + 1, r0:r0 + rt, :]
            o = o * lax.rsqrt(jnp.mean(o * o, axis=-1, keepdims=True) + EPS) * nw_ref[...]
            o_ref[r0:r0 + rt, hs] = (o * _silu(z_ref[r0:r0 + rt, hs].astype(F32))).astype(o_ref.dtype)


def _deltanet(proj, col, conv_w, norm_w, s0, *, batch, seq, n_heads, col_q, col_z):
    t = proj.shape[0]
    hd = DN_HEAD_DIM
    hp = 2
    d = n_heads * hd
    wd = hp * hd
    n_chunks = seq // DN_CHUNK
    ca = min(4, n_chunks)
    assert n_heads % hp == 0 and col_q % hp == 0 and col_z % hp == 0 and n_chunks % ca == 0
    blk = lambda off: pl.BlockSpec((seq, wd), lambda b, h: (b, off // hp + h))
    cw = lambda off: pl.BlockSpec((3, wd), lambda b, h: (0, off // hp + h))
    st = pl.BlockSpec((None, 2, hp, hd, hd), lambda b, h: (b, 0, h, 0, 0))
    nst = 2 * hp
    return pl.pallas_call(
        functools.partial(_dn_kernel, seq=seq, n_heads=n_heads, hp=hp, ca=ca),
        grid=(batch, n_heads // hp),
        in_specs=[blk(col_q), blk(col_q + n_heads), blk(col_q + 2 * n_heads), blk(col_z),
                  pl.BlockSpec((seq, LANE), lambda b, h: (b, 0)),
                  cw(0), cw(n_heads), cw(2 * n_heads),
                  pl.BlockSpec((1, hd), lambda b, h: (0, 0)),
                  st],
        out_specs=[pl.BlockSpec((seq, wd), lambda b, h: (b, h)), st],
        out_shape=[jax.ShapeDtypeStruct((t, d), BF16),
                   jax.ShapeDtypeStruct(s0.shape, F32)],
        scratch_shapes=[pltpu.VMEM((seq + 2 * SUBLANE, wd), F32),
                        pltpu.VMEM((seq, wd), F32), pltpu.VMEM((seq, wd), F32),
                        pltpu.VMEM((seq, wd), F32),
                        pltpu.VMEM((nst, seq, hd), F32),
                        pltpu.VMEM((nst, seq, hd), BF16),
                        pltpu.VMEM((nst, seq, DN_CHUNK), BF16),
                        pltpu.VMEM((nst, seq, hd), BF16),
                        pltpu.VMEM((nst, seq, hd), BF16),
                        pltpu.VMEM((nst, n_chunks * SUBLANE, LANE), F32),
                        pltpu.VMEM((nst, seq, hd), F32),
                        pltpu.VMEM((nst, hd, hd), F32)],
        compiler_params=_cparams(("parallel", "parallel")),
        name="deltanet",
    )(proj, proj, proj, proj, col, conv_w, conv_w, conv_w, norm_w.reshape(1, hd), s0)


def _conv_module_kernel(a_ref, g_ref, w_ref, b_ref, lnw_ref, lnb_ref, o_ref, yp_ref, tmp_ref,
                        sh_ref, *, seq, taps):
    d = a_ref.shape[1]
    half = taps // 2
    pad = 2 * SUBLANE
    assert half < pad
    rt = CV_ROW_TILE
    ct = CV_COL_TILE
    zeros = jnp.zeros((pad, d), F32)
    yp_ref[0:pad, :] = zeros
    yp_ref[seq + pad:seq + 2 * pad, :] = zeros

    def fill(t, carry):
        r0 = pl.multiple_of(t * rt, rt)
        a = a_ref[pl.ds(r0, rt), :].astype(F32)
        g = g_ref[pl.ds(r0, rt), :].astype(F32)
        yp_ref[pl.ds(r0 + pad, rt), :] = a * jax.nn.sigmoid(g)
        return carry

    lax.fori_loop(0, seq // rt, fill, 0)

    def tile(t, carry):
        r0 = pl.multiple_of(t * rt, rt)
        for cc in range(d // ct):
            cs = slice(cc * ct, (cc + 1) * ct)
            win = yp_ref[pl.ds(r0, rt + 2 * pad), cs]
            span = rt + 2 * pad - SUBLANE
            for m in range(1, SUBLANE):
                sh_ref[m - 1] = win[m:m + span]
            acc = jnp.zeros((rt, ct), F32) + b_ref[:, cs]
            for j in range(taps):
                off = pad - half + j
                lo = (off // SUBLANE) * SUBLANE
                m = off - lo
                if m == 0:
                    rows = yp_ref[pl.ds(pl.multiple_of(r0 + lo, SUBLANE), rt), cs]
                else:
                    rows = sh_ref[m - 1, lo:lo + rt, :]
                acc = acc + w_ref[j:j + 1, cs] * rows
            tmp_ref[:, cs] = acc
        y = tmp_ref[...]
        mu = jnp.mean(y, axis=-1, keepdims=True)
        yc = y - mu
        var = jnp.mean(yc * yc, axis=-1, keepdims=True)
        y = yc * lax.rsqrt(var + EPS) * lnw_ref[...] + lnb_ref[...]
        o_ref[pl.ds(r0, rt), :] = _silu(y).astype(o_ref.dtype)
        return carry

    lax.fori_loop(0, seq // rt, tile, 0)


def _conv_module(proj, w, b, lnw, lnb, *, batch, seq, col_a):
    t = proj.shape[0]
    taps, d = w.shape
    return pl.pallas_call(
        functools.partial(_conv_module_kernel, seq=seq, taps=taps),
        grid=(batch,),
        in_specs=[pl.BlockSpec((seq, d), lambda i: (i, col_a)),
                  pl.BlockSpec((seq, d), lambda i: (i, col_a + 1)),
                  pl.BlockSpec((taps, d), lambda i: (0, 0)),
                  pl.BlockSpec((1, d), lambda i: (0, 0)),
                  pl.BlockSpec((1, d), lambda i: (0, 0)),
                  pl.BlockSpec((1, d), lambda i: (0, 0))],
        out_specs=pl.BlockSpec((seq, d), lambda i: (i, 0)),
        out_shape=jax.ShapeDtypeStruct((t, d), BF16),
        scratch_shapes=[pltpu.VMEM((seq + 4 * SUBLANE, d), F32),
                        pltpu.VMEM((CV_ROW_TILE, d), F32),
                        pltpu.VMEM((SUBLANE - 1, CV_ROW_TILE + 3 * SUBLANE, CV_COL_TILE), F32)],
        compiler_params=_cparams(("parallel",)),
        name="conv_module",
    )(proj, proj, w, b.reshape(1, d), lnw.reshape(1, d), lnb.reshape(1, d))


def _merge_kernel(s5_ref, dn_ref, cv_ref, g0_ref, g1_ref, g2_ref, w0_ref, w1_ref, w2_ref,
                  wo_ref, x_ref, gate_ref, o_ref):
    merged = (jax.nn.sigmoid(g0_ref[...].astype(F32)) * _dot(s5_ref[...], w0_ref[...])
              + jax.nn.sigmoid(g1_ref[...].astype(F32)) * _dot(dn_ref[...], w1_ref[...])
              + jax.nn.sigmoid(g2_ref[...].astype(F32)) * _dot(cv_ref[...], w2_ref[...]))
    y = _dot(merged.astype(BF16), wo_ref[...])
    o_ref[...] = x_ref[...] + gate_ref[0] * y


def _merge(s5_out, dn_out, cv_out, proj, w_s5, w_dn, w_cv, w_out, x2, mod3, gate_idx,
           *, rows_per_batch, fixed_row, col_gate):
    t, d = x2.shape
    tm = min(512, t if fixed_row is not None else rows_per_batch)
    assert t % tm == 0
    row = _row_of_tile(tm, rows_per_batch, fixed_row)
    act = pl.BlockSpec((tm, d), lambda i: (i, 0))
    gat = lambda k: pl.BlockSpec((tm, d), lambda i: (i, col_gate + k))
    wsp = pl.BlockSpec((d, d), lambda i: (0, 0))
    return pl.pallas_call(
        _merge_kernel,
        grid=(t // tm,),
        in_specs=[act, act, act, gat(0), gat(1), gat(2), wsp, wsp, wsp, wsp, act,
                  pl.BlockSpec((None, 1, d), lambda i: (row(i), 0, gate_idx))],
        out_specs=act,
        out_shape=jax.ShapeDtypeStruct((t, d), F32),
        compiler_params=_cparams(("parallel",)),
        name="merge",
    )(s5_out, dn_out, cv_out, proj, proj, proj, w_s5, w_dn, w_cv, w_out, x2, mod3)


def _ffn_conv_kernel(a_ref, v_ref, w_ref, b_ref, o_ref, a0_ref, al_ref, ar_ref, *, seq, grid2d):
    ct = a_ref.shape[1]
    pad = 72 if grid2d else SUBLANE
    rt = 128
    zeros = jnp.zeros((pad, ct), F32)
    for buf in (a0_ref, al_ref, ar_ref):
        buf[0:pad, :] = zeros
        buf[seq + pad:seq + 2 * pad, :] = zeros

    def fill(t, carry):
        r0 = pl.multiple_of(t * rt, rt)
        a0_ref[pl.ds(r0 + pad, rt), :] = a_ref[pl.ds(r0, rt), :].astype(F32)
        return carry

    lax.fori_loop(0, seq // rt, fill, 0)

    def neighbours(t, carry):
        r0 = pl.multiple_of(t * rt, rt)
        left = _shifted_rows(a0_ref, r0 + pad, -1, rt, slice(None))
        right = _shifted_rows(a0_ref, r0 + pad, 1, rt, slice(None))
        if grid2d:
            colpos = (_iota((rt, ct), 0) + r0) & (GRID_W - 1)
            left = jnp.where(colpos == 0, 0.0, left)
            right = jnp.where(colpos == GRID_W - 1, 0.0, right)
        al_ref[pl.ds(r0 + pad, rt), :] = left
        ar_ref[pl.ds(r0 + pad, rt), :] = right
        return carry

    lax.fori_loop(0, seq // rt, neighbours, 0)

    if grid2d:
        taps = [(dr, dc) for dr in (-1, 0, 1) for dc in (-1, 0, 1)]
    else:
        taps = [(0, dc) for dc in (-1, 0, 1)]
    src = {-1: al_ref, 0: a0_ref, 1: ar_ref}

    def tile(t, carry):
        r0 = pl.multiple_of(t * rt, rt)
        acc = jnp.zeros((rt, ct), F32) + b_ref[...]
        for dr, dc in taps:
            widx = (dr + 1) * 3 + (dc + 1)
            rows = pl.ds(pl.multiple_of(r0 + pad + dr * GRID_W, SUBLANE), rt)
            acc = acc + w_ref[widx:widx + 1, :] * src[dc][rows, :]
        o_ref[pl.ds(r0, rt), :] = (_silu(acc) * v_ref[pl.ds(r0, rt), :].astype(F32)).astype(o_ref.dtype)
        return carry

    lax.fori_loop(0, seq // rt, tile, 0)


def _ffn_conv(av, w9, bias, *, batch, seq, grid2d):
    t = av.shape[0]
    f = w9.shape[1]
    ct = 256
    assert f % ct == 0 and seq % 128 == 0
    nct = f // ct
    pad = 72 if grid2d else SUBLANE
    return pl.pallas_call(
        functools.partial(_ffn_conv_kernel, seq=seq, grid2d=grid2d),
        grid=(batch, nct),
        in_specs=[pl.BlockSpec((seq, ct), lambda b, c: (b, c)),
                  pl.BlockSpec((seq, ct), lambda b, c: (b, nct + c)),
                  pl.BlockSpec((9, ct), lambda b, c: (0, c)),
                  pl.BlockSpec((1, ct), lambda b, c: (0, c))],
        out_specs=pl.BlockSpec((seq, ct), lambda b, c: (b, c)),
        out_shape=jax.ShapeDtypeStruct((t, f), BF16),
        scratch_shapes=[pltpu.VMEM((seq + 2 * pad, ct), F32)] * 3,
        compiler_params=_cparams(("parallel", "parallel")),
        name="ffn_conv",
    )(av, av, w9, bias.reshape(1, f))


def _resid_matmul_kernel(*refs, final_norm):
    if final_norm:
        a_ref, w_ref, x_ref, gate_ref, fw_ref, o_ref = refs
    else:
        a_ref, w_ref, x_ref, gate_ref, o_ref = refs
    y = x_ref[...] + gate_ref[0] * _dot(a_ref[...], w_ref[...])
    if final_norm:
        y = y * lax.rsqrt(jnp.mean(y * y, axis=-1, keepdims=True) + EPS) * fw_ref[...]
    o_ref[...] = y


def _resid_matmul(a, w, x2, mod3, gate_idx, *, rows_per_batch, fixed_row, final_w=None):
    t, d = x2.shape
    k = a.shape[1]
    tm = min(512, t if fixed_row is not None else rows_per_batch)
    assert t % tm == 0
    row = _row_of_tile(tm, rows_per_batch, fixed_row)
    final_norm = final_w is not None
    in_specs = [pl.BlockSpec((tm, k), lambda i: (i, 0)),
                pl.BlockSpec((k, d), lambda i: (0, 0)),
                pl.BlockSpec((tm, d), lambda i: (i, 0)),
                pl.BlockSpec((None, 1, d), lambda i: (row(i), 0, gate_idx))]
    args = [a, w, x2, mod3]
    if final_norm:
        in_specs.append(pl.BlockSpec((1, d), lambda i: (0, 0)))
        args.append(final_w.reshape(1, d))
    return pl.pallas_call(
        functools.partial(_resid_matmul_kernel, final_norm=final_norm),
        grid=(t // tm,),
        in_specs=in_specs,
        out_specs=pl.BlockSpec((tm, d), lambda i: (i, 0)),
        out_shape=jax.ShapeDtypeStruct((t, d), F32),
        compiler_params=_cparams(("parallel",)),
        name="resid_matmul",
    )(*args)


def _group_major(u2, batch, seq, groups):
    nk = seq // S5_CHUNK
    u5 = u2.reshape(batch, nk, S5_CHUNK, groups, S5_GROUP)
    return jnp.transpose(u5, (3, 1, 0, 2, 4)).reshape(groups, nk * batch, S5_CHUNK * S5_GROUP)


def _token_major(yg, batch, seq, groups):
    nk = seq // S5_CHUNK
    y5 = yg.reshape(groups, nk, batch, S5_CHUNK, S5_GROUP)
    return jnp.transpose(y5, (2, 1, 3, 0, 4)).reshape(batch * seq, groups * S5_GROUP)


def kernel(x, c, ctx, c_ctx, ada_w, ada_b, norm1_w, norm2_w, w_in, s5_a_re, s5_a_im, s5_log_dt, s5_b_re, s5_b_im, s5_c_re, s5_c_im, s5_d, s5_w_glu, dn_conv_w, dn_a_log, dn_dt_bias, dn_norm_w, cv_dw_w, cv_dw_b, cv_ln_w, cv_ln_b, w_br_s5, w_br_dn, w_br_cv, w_out, ffn_w_up, ffn_dw_w, ffn_dw_b, ffn_w_down, final_norm_w):
    batch, seq, d = x.shape
    ctx_len = ctx.shape[1]
    depth = ada_w.shape[0]
    groups = s5_a_re.shape[2]
    n_heads = dn_a_log.shape[2]
    dn_width = n_heads * DN_HEAD_DIM
    ffn_hidden = ffn_dw_b.shape[1]
    assert batch == SUBLANE and batch < MOD_ROWS and d == groups * S5_GROUP == dn_width
    assert d % 1024 == 0 and 4 * n_heads <= LANE

    col_qkv = d
    col_beta = col_qkv + 3 * dn_width
    n_state = col_beta + 4 * n_heads
    col_cv = n_state + dn_width
    col_gate = col_cv + 2 * d
    pk_q = d // LANE
    pk_z = (d + 3 * dn_width) // LANE
    pk_cv = (d + 4 * dn_width) // d
    pk_gate = pk_cv + 2

    cmat = jnp.zeros((MOD_ROWS, d), F32).at[:batch].set(c).at[batch].set(c_ctx)
    xl = x.reshape(batch * seq, d)
    xc = ctx.reshape(batch * ctx_len, d)

    for i in range(depth):
        last = i == depth - 1
        w_main = jnp.concatenate([w_in[i][:, :col_beta], w_in[i][:, n_state:]], axis=1).astype(BF16)
        w_bd = jnp.pad(w_in[i][:, col_beta:n_state], ((0, 0), (0, LANE - 4 * n_heads)))
        aneg = jnp.pad(-jnp.exp(dn_a_log[i].reshape(1, -1)), ((0, 0), (2 * n_heads, LANE - 4 * n_heads)))
        dtb = jnp.pad(dn_dt_bias[i].reshape(1, -1), ((0, 0), (2 * n_heads, LANE - 4 * n_heads)))
        wt, win, wst, avec = _s5_matrices(s5_a_re[i], s5_a_im[i], s5_log_dt[i], s5_b_re[i],
                                          s5_b_im[i], s5_c_re[i], s5_c_im[i])
        w_glu = s5_w_glu[i].astype(BF16)
        w_brs = w_br_s5[i].astype(BF16)
        w_brd = w_br_dn[i].astype(BF16)
        w_brc = w_br_cv[i].astype(BF16)
        w_o = w_out[i].astype(BF16)
        w_up = ffn_w_up[i].astype(BF16)
        w_dn = ffn_w_down[i].astype(BF16)
        w9 = ffn_dw_w[i].reshape(9, ffn_hidden)

        mod3 = _modulation(cmat, ada_w[i], ada_b[i]).reshape(MOD_ROWS, 1, N_MOD * d)

        rows_c = dict(rows_per_batch=ctx_len, fixed_row=batch)
        rows_l = dict(rows_per_batch=seq, fixed_row=None)
        proj_c, bd_c = _norm_matmul(xc, norm1_w[i], mod3, 0, 1, w_main, wbd=w_bd, tn=d, **rows_c)
        proj_l, bd_l = _norm_matmul(xl, norm1_w[i], mod3, 0, 1, w_main, wbd=w_bd, tn=d, **rows_l)

        u_rows = jnp.concatenate([_group_major(proj_c[:, :d], batch, ctx_len, groups),
                                  _group_major(proj_l[:, :d], batch, seq, groups)], axis=1)
        rows_ctx = (ctx_len // S5_CHUNK) * batch
        segments = ((0, ctx_len // S5_CHUNK), (rows_ctx, seq // S5_CHUNK))
        y_rows = _s5_core(u_rows, wt.astype(BF16), win.astype(BF16), wst.astype(BF16), avec,
                          segments=segments, batch=batch)

        zero_dn = jnp.zeros((batch, 2, n_heads, DN_HEAD_DIM, DN_HEAD_DIM), F32)
        dn_args = dict(batch=batch, n_heads=n_heads, col_q=pk_q, col_z=pk_z)
        dn_c, dn_state = _deltanet(proj_c, _dn_gates(bd_c, aneg, dtb, n_heads), dn_conv_w[i],
                                   dn_norm_w[i], zero_dn, seq=ctx_len, **dn_args)
        dn_l, _ = _deltanet(proj_l, _dn_gates(bd_l, aneg, dtb, n_heads), dn_conv_w[i],
                            dn_norm_w[i], dn_state, seq=seq, **dn_args)

        def finish(x2, proj, y, dn_out, length, rows, grid2d, final_w):
            s5_out = _s5_post(y, proj, s5_d[i], w_glu)
            cv_out = _conv_module(proj, cv_dw_w[i], cv_dw_b[i], cv_ln_w[i], cv_ln_b[i],
                                  batch=batch, seq=length, col_a=pk_cv)
            x2 = _merge(s5_out, dn_out, cv_out, proj, w_brs, w_brd, w_brc, w_o, x2, mod3, 2,
                        col_gate=pk_gate, **rows)
            av = _norm_matmul(x2, norm2_w[i], mod3, 3, 4, w_up, tn=w_up.shape[1] // 4, **rows)
            hidden = _ffn_conv(av, w9, ffn_dw_b[i], batch=batch, seq=length, grid2d=grid2d)
            return _resid_matmul(hidden, w_dn, x2, mod3, 5, final_w=final_w, **rows)

        if not last:
            xc = finish(xc, proj_c, _token_major(y_rows[:, :rows_ctx], batch, ctx_len, groups),
                        dn_c, ctx_len, rows_c, False, None)
        xl = finish(xl, proj_l, _token_major(y_rows[:, rows_ctx:], batch, seq, groups), dn_l, seq,
                    rows_l, True, final_norm_w if last else None)

    return xl.reshape(batch, seq, d)
```

```python
import functools
import math

import jax
import jax.numpy as jnp
from jax import lax
from jax.experimental import pallas as pl
from jax.experimental.pallas import tpu as pltpu

F32 = jnp.float32
BF16 = jnp.bfloat16
HIGHEST = lax.Precision.HIGHEST

EPS = 1e-6
GRID_W = 64
S5_GROUP = 16
DN_HEAD_DIM = 128
N_MOD = 6
N_BRANCH = 3

LANE = 128
SUBLANE = 8
S5_CHUNK = 16
DN_CHUNK = 64
MOD_ROWS = 16
CV_ROW_TILE = 64
CV_COL_TILE = 256
VMEM_LIMIT = 56 * 1024 * 1024


def _cparams(sem, vmem=VMEM_LIMIT):
    return pltpu.CompilerParams(dimension_semantics=sem, vmem_limit_bytes=vmem)


def _dot(a, b, precision=None):
    return jnp.dot(a, b, preferred_element_type=F32, precision=precision)


def _dot_nt(a, b, precision=None):
    return lax.dot_general(a, b, (((1,), (1,)), ((), ())), preferred_element_type=F32,
                           precision=precision)


def _dot_tn(a, b, precision=None):
    return lax.dot_general(a, b, (((0,), (0,)), ((), ())), preferred_element_type=F32,
                           precision=precision)


def _silu(x):
    return x * jax.nn.sigmoid(x)


def _iota(shape, dim):
    return lax.broadcasted_iota(jnp.int32, shape, dim)


def _shifted_rows(ref, base, off, rows, cols):
    lo = (off // SUBLANE) * SUBLANE
    shift = off - lo
    start = pl.multiple_of(base + lo, SUBLANE)
    if shift == 0:
        return ref[pl.ds(start, rows), cols]
    return ref[pl.ds(start, rows + SUBLANE), cols][shift:shift + rows]


def _mod_kernel(c_ref, w_ref, b_ref, o_ref):
    o_ref[...] = _dot(_silu(c_ref[...]), w_ref[...], HIGHEST) + b_ref[...]


def _modulation(cmat, ada_w, ada_b):
    rows, d = cmat.shape
    n = ada_w.shape[1]
    tn = 1024
    return pl.pallas_call(
        _mod_kernel,
        grid=(n // tn,),
        in_specs=[pl.BlockSpec((rows, d), lambda j: (0, 0)),
                  pl.BlockSpec((d, tn), lambda j: (0, j)),
                  pl.BlockSpec((1, tn), lambda j: (0, j))],
        out_specs=pl.BlockSpec((rows, tn), lambda j: (0, j)),
        out_shape=jax.ShapeDtypeStruct((rows, n), F32),
        compiler_params=_cparams(("parallel",)),
        name="modulation",
    )(cmat, ada_w, ada_b.reshape(1, n))


def _norm_matmul_kernel(*refs, with_bd):
    if with_bd:
        x_ref, nw_ref, sh_ref, sc_ref, w_ref, wbd_ref, o_ref, bd_ref, h_ref = refs
    else:
        x_ref, nw_ref, sh_ref, sc_ref, w_ref, o_ref, h_ref = refs

    @pl.when(pl.program_id(1) == 0)
    def _():
        x = x_ref[...]
        h = x * lax.rsqrt(jnp.mean(x * x, axis=-1, keepdims=True) + EPS) * nw_ref[...]
        h = h * (1.0 + sc_ref[0]) + sh_ref[0]
        h_ref[...] = h.astype(BF16)
        if with_bd:
            bd_ref[...] = _dot(h, wbd_ref[...], HIGHEST)

    o_ref[...] = _dot(h_ref[...], w_ref[...]).astype(o_ref.dtype)


def _row_of_tile(tm, rows_per_batch, fixed_row):
    if fixed_row is not None:
        return lambda i: fixed_row
    return lambda i: (i * tm) // rows_per_batch


def _norm_matmul(x2, norm_w, mod3, sh_idx, sc_idx, w, *, rows_per_batch, fixed_row, wbd=None,
                 tn):
    t, d = x2.shape
    n = w.shape[1]
    tm = min(1024, t if fixed_row is not None else rows_per_batch)
    assert t % tm == 0 and n % tn == 0
    row = _row_of_tile(tm, rows_per_batch, fixed_row)
    with_bd = wbd is not None
    in_specs = [pl.BlockSpec((tm, d), lambda i, j: (i, 0)),
                pl.BlockSpec((1, d), lambda i, j: (0, 0)),
                pl.BlockSpec((None, 1, d), lambda i, j: (row(i), 0, sh_idx)),
                pl.BlockSpec((None, 1, d), lambda i, j: (row(i), 0, sc_idx)),
                pl.BlockSpec((d, tn), lambda i, j: (0, j))]
    args = [x2, norm_w.reshape(1, d), mod3, mod3, w]
    out_specs = [pl.BlockSpec((tm, tn), lambda i, j: (i, j))]
    out_shape = [jax.ShapeDtypeStruct((t, n), BF16)]
    if with_bd:
        in_specs.append(pl.BlockSpec((d, LANE), lambda i, j: (0, 0)))
        args.append(wbd)
        out_specs.append(pl.BlockSpec((tm, LANE), lambda i, j: (i, 0)))
        out_shape.append(jax.ShapeDtypeStruct((t, LANE), F32))
    res = pl.pallas_call(
        functools.partial(_norm_matmul_kernel, with_bd=with_bd),
        grid=(t // tm, n // tn),
        in_specs=in_specs,
        out_specs=out_specs,
        out_shape=out_shape,
        scratch_shapes=[pltpu.VMEM((tm, d), BF16)],
        compiler_params=_cparams(("parallel", "arbitrary")),
        name="norm_matmul",
    )(*args)
    return res if with_bd else res[0]


def _dn_gates_kernel(bd_ref, aneg_ref, dtb_ref, o_ref, *, n_heads):
    x = bd_ref[...]
    tm = x.shape[0]
    y = x + dtb_ref[...]
    g = aneg_ref[...] * (jnp.maximum(y, 0.0) + jnp.log1p(jnp.exp(-jnp.abs(y))))
    r = _iota((tm, tm), 0)
    c = _iota((tm, tm), 1)
    shift = DN_CHUNK.bit_length() - 1
    same = (r >> shift) == (c >> shift)
    prefix = _dot(jnp.where(same & (c <= r), 1.0, 0.0), g, HIGHEST)
    suffix = _dot(jnp.where(same & (c >= r), 1.0, 0.0), g, HIGHEST)
    lane = _iota(x.shape, 1)
    o_ref[...] = jnp.where(lane < 2 * n_heads, jax.nn.sigmoid(x),
                           jnp.where(lane < 3 * n_heads, prefix, suffix))


def _dn_gates(bd, aneg, dtb, n_heads):
    t = bd.shape[0]
    tm = min(512, t)
    assert t % tm == 0 and tm % DN_CHUNK == 0
    return pl.pallas_call(
        functools.partial(_dn_gates_kernel, n_heads=n_heads),
        grid=(t // tm,),
        in_specs=[pl.BlockSpec((tm, LANE), lambda i: (i, 0)),
                  pl.BlockSpec((1, LANE), lambda i: (0, 0)),
                  pl.BlockSpec((1, LANE), lambda i: (0, 0))],
        out_specs=pl.BlockSpec((tm, LANE), lambda i: (i, 0)),
        out_shape=jax.ShapeDtypeStruct((t, LANE), F32),
        compiler_params=_cparams(("parallel",)),
        name="dn_gates",
    )(bd, aneg, dtb)


def _s5_kernel(u_ref, kx_ref, bt_ref, c_ref, pw_ref, a_ref, y_ref, s_ref, hp_ref,
               *, segments, batch):
    tc, cg = S5_CHUNK, S5_GROUP
    half = LANE // 2
    n_grp = u_ref.shape[0]

    def operators(g):
        kx = kx_ref[g]
        wt = jnp.concatenate([kx[:, cg * (tc - 1 - i):cg * (tc - 1 - i) + tc * cg]
                              for i in range(tc)], axis=0)
        btr = [bt_ref[g, d, 0] for d in range(2)]
        bti = [bt_ref[g, d, 1] for d in range(2)]
        cr, ci = c_ref[g, 0], c_ref[g, 1]
        pr = [pw_ref[g, d, 0] for d in range(2)]
        pi = [pw_ref[g, d, 1] for d in range(2)]
        win_rows, wst_rows = [], []
        for i in range(tc):
            pf_r, pf_i = pr[0][tc - 1 - i:tc - i, :], pi[0][tc - 1 - i:tc - i, :]
            pb_r, pb_i = pr[1][i:i + 1, :], pi[1][i:i + 1, :]
            win_rows.append(jnp.concatenate([
                pf_r * btr[0] - pf_i * bti[0], pb_r * btr[1] - pb_i * bti[1],
                pf_r * bti[0] + pf_i * btr[0], pb_r * bti[1] + pb_i * btr[1]], axis=1))
            qf_r, qf_i = pr[0][i + 1:i + 2, :], pi[0][i + 1:i + 2, :]
            qb_r, qb_i = pr[1][tc - i:tc - i + 1, :], pi[1][tc - i:tc - i + 1, :]
            wst_rows.append(jnp.concatenate([
                cr * qf_r - ci * qf_i, cr * qb_r - ci * qb_i,
                -(cr * qf_i + ci * qf_r), -(cr * qb_i + ci * qb_r)], axis=1))
        win = jnp.concatenate(win_rows, axis=0)
        wst_t = jnp.concatenate(wst_rows, axis=0)
        return wt.astype(BF16), win.astype(BF16), wst_t.astype(BF16)

    ops = [operators(g) for g in range(n_grp)]
    for g in range(n_grp):
        s_ref[g] = _dot(u_ref[g], ops[g][1])
    ars = [a_ref[g, 0:1, :] for g in range(n_grp)]
    ais = [a_ref[g, 1:2, :] for g in range(n_grp)]
    is_fwd = _iota((batch, LANE), 1) < half
    carry = tuple(jnp.zeros((batch, LANE), F32) for _ in range(2 * n_grp))
    for row0, n_chunks in segments:
        def body(s, carry, row0=row0, n_chunks=n_chunks):
            rf = pl.ds(pl.multiple_of(row0 + s * batch, batch), batch)
            rb = pl.ds(pl.multiple_of(row0 + (n_chunks - 1 - s) * batch, batch), batch)
            new = []
            for g in range(n_grp):
                hre, him = carry[2 * g], carry[2 * g + 1]
                hp_ref[g, rf, 0:half] = hre[:, 0:half]
                hp_ref[g, rb, half:LANE] = hre[:, half:LANE]
                hp_ref[g, rf, LANE:LANE + half] = him[:, 0:half]
                hp_ref[g, rb, LANE + half:2 * LANE] = him[:, half:LANE]
                sf = s_ref[g, rf, :]
                sb = s_ref[g, rb, :]
                sre = jnp.where(is_fwd, sf[:, 0:LANE], sb[:, 0:LANE])
                sim = jnp.where(is_fwd, sf[:, LANE:2 * LANE], sb[:, LANE:2 * LANE])
                new.append(ars[g] * hre - ais[g] * him + sre)
                new.append(ars[g] * him + ais[g] * hre + sim)
            return tuple(new)

        carry = lax.fori_loop(0, n_chunks, body, carry)
    for g in range(n_grp):
        y = _dot(u_ref[g], ops[g][0]) + _dot_nt(hp_ref[g].astype(BF16), ops[g][2])
        y_ref[g] = y.astype(y_ref.dtype)


S5_GROUPS_PER_STEP = 4


def _s5_core(ug, kx, bt, cc, pw, avec, *, segments, batch):
    g, rows, k = ug.shape
    gb = S5_GROUPS_PER_STEP
    assert batch == SUBLANE and k == 2 * LANE and g % gb == 0
    act = pl.BlockSpec((gb, rows, k), lambda i: (i, 0, 0))
    per_group = lambda a: pl.BlockSpec((gb,) + a.shape[1:],
                                       lambda i: (i,) + (0,) * (a.ndim - 1))
    return pl.pallas_call(
        functools.partial(_s5_kernel, segments=segments, batch=batch),
        grid=(g // gb,),
        in_specs=[act, per_group(kx), per_group(bt), per_group(cc), per_group(pw),
                  per_group(avec)],
        out_specs=act,
        out_shape=jax.ShapeDtypeStruct((g, rows, k), BF16),
        scratch_shapes=[pltpu.VMEM((gb, rows, k), F32), pltpu.VMEM((gb, rows, k), F32)],
        compiler_params=_cparams(("parallel",)),
        name="s5_core",
    )(ug, kx, bt, cc, pw, avec)


def _s5_params(a_re, a_im, log_dt, b_re, b_im, c_re, c_im):
    tc = S5_CHUNK
    dt = jnp.exp(log_dt)[..., None]
    mag = jnp.exp(a_re * dt)
    abr, abi = mag * jnp.cos(a_im * dt), mag * jnp.sin(a_im * dt)
    den = a_re * a_re + a_im * a_im
    cr = ((abr - 1.0) * a_re + abi * a_im) / den
    ci = (abi * a_re - (abr - 1.0) * a_im) / den
    bbr = cr[..., None] * b_re - ci[..., None] * b_im
    bbi = cr[..., None] * b_im + ci[..., None] * b_re
    n = jnp.arange(tc + 1, dtype=F32)[:, None, None, None]
    pmag = jnp.exp(a_re * dt * n)
    pr, pi = pmag * jnp.cos(a_im * dt * n), pmag * jnp.sin(a_im * dt * n)
    car = c_re[None, None] * pr[:tc, :, :, None, :] - c_im[None, None] * pi[:tc, :, :, None, :]
    cai = c_re[None, None] * pi[:tc, :, :, None, :] + c_im[None, None] * pr[:tc, :, :, None, :]
    kern = (jnp.einsum("ndgcp,dgpe->ndgec", car, bbr, precision=HIGHEST)
            - jnp.einsum("ndgcp,dgpe->ndgec", cai, bbi, precision=HIGHEST))
    blocks = jnp.concatenate([kern[:0:-1, 1], (kern[0, 0] + kern[0, 1])[None], kern[1:, 0],
                              jnp.zeros_like(kern[:1, 0])], axis=0)
    g_, c_ = c_re.shape[0], c_re.shape[1]
    kx = jnp.transpose(blocks, (1, 2, 0, 3)).reshape(g_, c_, 2 * tc * c_)
    bt = jnp.stack([jnp.swapaxes(bbr, 2, 3), jnp.swapaxes(bbi, 2, 3)], axis=2)
    bt = jnp.transpose(bt, (1, 0, 2, 3, 4))
    cc = jnp.stack([c_re, c_im], axis=1)
    pw = jnp.transpose(jnp.stack([pr, pi], axis=0), (3, 2, 0, 1, 4))
    pw = jnp.pad(pw, ((0, 0), (0, 0), (0, 0), (0, SUBLANE - 1), (0, 0)))
    avec = jnp.stack([jnp.concatenate([pr[tc, 0], pr[tc, 1]], axis=-1),
                      jnp.concatenate([pi[tc, 0], pi[tc, 1]], axis=-1)], axis=1)
    return kx, bt, cc, pw, avec


def _s5_post_kernel(y_ref, u_ref, d_ref, w_ref, o_ref):
    y = y_ref[...].astype(F32) + d_ref[...] * u_ref[...].astype(F32)
    k0 = math.sqrt(2.0 / math.pi)
    g = 0.5 * y * (1.0 + jnp.tanh(k0 * (y + 0.044715 * (y * y * y))))
    o_ref[...] = (g * jax.nn.sigmoid(_dot(g.astype(BF16), w_ref[...]))).astype(o_ref.dtype)


def _s5_post(y, proj, s5_d, w_glu):
    t, d = y.shape
    tm = min(512, t)
    return pl.pallas_call(
        _s5_post_kernel,
        grid=(t // tm,),
        in_specs=[pl.BlockSpec((tm, d), lambda i: (i, 0)),
                  pl.BlockSpec((tm, d), lambda i: (i, 0)),
                  pl.BlockSpec((1, d), lambda i: (0, 0)),
                  pl.BlockSpec((d, d), lambda i: (0, 0))],
        out_specs=pl.BlockSpec((tm, d), lambda i: (i, 0)),
        out_shape=jax.ShapeDtypeStruct((t, d), BF16),
        compiler_params=_cparams(("parallel",)),
        name="s5_post",
    )(y, proj, s5_d.reshape(1, d), w_glu)


def _unit_tri_inverses(ms, r, c):
    mm = lambda a, b: _dot(a.astype(BF16), b.astype(BF16))
    eye = jnp.where(r == c, 1.0, 0.0)
    blk = lambda s: (r >> s) == (c >> s)
    mds = [jnp.where(blk(3), m, 0.0) for m in ms]
    m2s = [mm(md, md) for md in mds]
    m4s = [mm(m2, m2) for m2 in m2s]
    ts = [eye - md for md in mds]
    ts = [t + mm(t, m2) for t, m2 in zip(ts, m2s)]
    ts = [t + mm(t, m4) for t, m4 in zip(ts, m4s)]
    for s in (3, 4, 5):
        off = blk(s + 1) & jnp.logical_not(blk(s))
        xs = [mm(jnp.where(off, m, 0.0), t) for m, t in zip(ms, ts)]
        ts = [t - mm(t, x) for t, x in zip(ts, xs)]
    return ts


def _dn_kernel(q_ref, k_ref, v_ref, z_ref, col_ref, cwq_ref, cwk_ref, cwv_ref, nw_ref, s0_ref,
               o_ref, sfin_ref, xp_ref, qs_ref, ks_ref, vs_ref, u_ref, w_ref, a_ref, qg_ref,
               kt_ref, el_ref, od_ref, s_ref, *, seq, n_heads, hp, ca):
    c_sz = DN_CHUNK
    hd = DN_HEAD_DIM
    pad = SUBLANE
    rt = min(256, seq)
    n_chunks = seq // c_sz
    head0 = pl.program_id(1) * hp

    def prep(x_ref, cw_ref, dst_ref, normalise, scale):
        zeros = jnp.zeros((pad, hp * hd), F32)
        xp_ref[0:pad, :] = zeros
        xp_ref[seq + pad:seq + 2 * pad, :] = zeros
        xp_ref[pad:seq + pad, :] = x_ref[...].astype(F32)
        w = cw_ref[...]
        for t in range(seq // rt):
            r0 = t * rt
            acc = (w[0:1, :] * xp_ref[r0 + pad - 1:r0 + pad - 1 + rt, :]
                   + w[1:2, :] * xp_ref[r0 + pad:r0 + pad + rt, :]
                   + w[2:3, :] * xp_ref[r0 + pad + 1:r0 + pad + 1 + rt, :])
            y = _silu(acc)
            if normalise:
                parts = []
                for hl in range(hp):
                    yh = y[:, hl * hd:(hl + 1) * hd]
                    parts.append(yh * (lax.rsqrt(jnp.sum(yh * yh, axis=-1, keepdims=True) + EPS)
                                       * scale))
                y = jnp.concatenate(parts, axis=1)
            dst_ref[r0:r0 + rt, :] = y

    prep(q_ref, cwq_ref, qs_ref, True, hd ** -0.5)
    prep(k_ref, cwk_ref, ks_ref, True, 1.0)
    prep(v_ref, cwv_ref, vs_ref, False, 1.0)

    r = _iota((c_sz, c_sz), 0)
    c = _iota((c_sz, c_sz), 1)
    lane = _iota((c_sz, LANE), 1)

    def pick(col, idx):
        v = jnp.sum(jnp.where(lane == idx, col, 0.0), axis=-1, keepdims=True)
        return jnp.broadcast_to(v, (c_sz, LANE))

    def phase_a(it, carry):
        pairs = []
        for cc in range(ca):
            chunk = it * ca + cc
            rows = pl.ds(pl.multiple_of(chunk * c_sz, c_sz), c_sz)
            col = col_ref[rows, :]
            for hl in range(hp):
                hs = slice(hl * hd, (hl + 1) * hd)
                pairs.append((chunk, rows, hl, col, ks_ref[rows, hs], qs_ref[rows, hs],
                              vs_ref[rows, hs]))
        k16s = [p[4].astype(BF16) for p in pairs]
        kkts = [_dot_nt(k16, k16) for k16 in k16s]
        qkts = [_dot_nt(p[5].astype(BF16), k16) for p, k16 in zip(pairs, k16s)]
        chains = []
        for pi, (chunk, rows, hl, col, kc, qc, vc) in enumerate(pairs):
            for direction in range(2):
                lower = direction == 0
                bc = pick(col, head0 + hl + direction * n_heads)
                gc = pick(col, head0 + hl + (2 + direction) * n_heads)
                grow = jnp.concatenate([gc, gc], axis=0).T[0:c_sz, 0:c_sz]
                incl = (r >= c) if lower else (r <= c)
                strict = (r > c) if lower else (r < c)
                decay = jnp.where(incl, jnp.exp(jnp.minimum(gc[:, 0:c_sz] - grow, 0.0)), 0.0)
                m = jnp.where(strict, bc[:, 0:c_sz] * kkts[pi] * decay, 0.0)
                chains.append((pi, direction, bc, gc, decay, m))
        tinvs = _unit_tri_inverses([ch[5] for ch in chains], r, c)
        egs = [jnp.exp(ch[3]) for ch in chains]
        uws = []
        for (pi, direction, bc, gc, decay, m), tinv, eg in zip(chains, tinvs, egs):
            kc, vc = pairs[pi][4], pairs[pi][6]
            rhs = jnp.concatenate([vc * bc, kc * bc * eg], axis=1).astype(BF16)
            uws.append(_dot(tinv.astype(BF16), rhs))
        for (pi, direction, bc, gc, decay, m), eg, uw in zip(chains, egs, uws):
            chunk, rows, hl, col, kc, qc, vc = pairs[pi]
            idx = hl * 2 + direction
            g_last = gc[c_sz - 1:c_sz, :] if direction == 0 else gc[0:1, :]
            u_ref[idx, rows, :] = uw[:, 0:hd]
            w_ref[idx, rows, :] = uw[:, hd:2 * hd].astype(BF16)
            a_ref[idx, rows, :] = (qkts[pi] * decay).astype(BF16)
            qg_ref[idx, rows, :] = (qc * eg).astype(BF16)
            kt_ref[idx, rows, :] = (kc * jnp.exp(g_last - gc)).astype(BF16)
            el_ref[idx, pl.ds(pl.multiple_of(chunk * SUBLANE, SUBLANE), SUBLANE), :] = (
                jnp.broadcast_to(jnp.exp(g_last), (SUBLANE, LANE)))
        return carry

    lax.fori_loop(0, n_chunks // ca, phase_a, 0)

    for hl in range(hp):
        for direction in range(2):
            s_ref[hl * 2 + direction] = s0_ref[direction, hl]

    def phase_b(step, carry):
        ids, rows, erows = [], [], []
        for hl in range(hp):
            for direction in range(2):
                chunk = step if direction == 0 else n_chunks - 1 - step
                ids.append(hl * 2 + direction)
                rows.append(pl.ds(pl.multiple_of(chunk * c_sz, c_sz), c_sz))
                erows.append(pl.ds(pl.multiple_of(chunk * SUBLANE, SUBLANE), SUBLANE))
        ss = [s_ref[i] for i in ids]
        s16s = [s.astype(BF16) for s in ss]
        wss = [_dot(w_ref[i, rw, :], s16) for i, rw, s16 in zip(ids, rows, s16s)]
        qss = [_dot(qg_ref[i, rw, :], s16) for i, rw, s16 in zip(ids, rows, s16s)]
        vns = [(u_ref[i, rw, :] - ws).astype(BF16) for i, rw, ws in zip(ids, rows, wss)]
        avs = [_dot(a_ref[i, rw, :], vn) for i, rw, vn in zip(ids, rows, vns)]
        kvs = [_dot_tn(kt_ref[i, rw, :], vn) for i, rw, vn in zip(ids, rows, vns)]
        for i, rw, er, s, qs_, av, kv in zip(ids, rows, erows, ss, qss, avs, kvs):
            od_ref[i, rw, :] = qs_ + av
            s_ref[i] = s * el_ref[i, er, :][0:1, :] + kv
        return carry

    lax.fori_loop(0, n_chunks, phase_b, 0)

    for hl in range(hp):
        for direction in range(2):
            sfin_ref[direction, hl] = s_ref[hl * 2 + direction]

    for t in range(seq // rt):
        r0 = t * rt
        for hl in range(hp):
            hs = slice(hl * hd, (hl + 1) * hd)
            o = od_ref[hl * 2, r0:r0 + rt, :] + od_ref[hl * 2 + 1, r0:r0 + rt, :]
            o = o * lax.rsqrt(jnp.mean(o * o, axis=-1, keepdims=True) + EPS) * nw_ref[...]
            o_ref[r0:r0 + rt, hs] = (o * _silu(z_ref[r0:r0 + rt, hs].astype(F32))).astype(o_ref.dtype)


def _deltanet(proj, col, conv_w, norm_w, s0, *, batch, seq, n_heads, col_q, col_z):
    t = proj.shape[0]
    hd = DN_HEAD_DIM
    hp = 2
    d = n_heads * hd
    wd = hp * hd
    n_chunks = seq // DN_CHUNK
    ca = min(8, n_chunks)
    assert n_heads % hp == 0 and col_q % hp == 0 and col_z % hp == 0 and n_chunks % ca == 0
    blk = lambda off: pl.BlockSpec((seq, wd), lambda b, h: (b, off // hp + h))
    cw = lambda off: pl.BlockSpec((3, wd), lambda b, h: (0, off // hp + h))
    st = pl.BlockSpec((None, 2, hp, hd, hd), lambda b, h: (b, 0, h, 0, 0))
    nst = 2 * hp
    return pl.pallas_call(
        functools.partial(_dn_kernel, seq=seq, n_heads=n_heads, hp=hp, ca=ca),
        grid=(batch, n_heads // hp),
        in_specs=[blk(col_q), blk(col_q + n_heads), blk(col_q + 2 * n_heads), blk(col_z),
                  pl.BlockSpec((seq, LANE), lambda b, h: (b, 0)),
                  cw(0), cw(n_heads), cw(2 * n_heads),
                  pl.BlockSpec((1, hd), lambda b, h: (0, 0)),
                  st],
        out_specs=[pl.BlockSpec((seq, wd), lambda b, h: (b, h)), st],
        out_shape=[jax.ShapeDtypeStruct((t, d), BF16),
                   jax.ShapeDtypeStruct(s0.shape, F32)],
        scratch_shapes=[pltpu.VMEM((seq + 2 * SUBLANE, wd), F32),
                        pltpu.VMEM((seq, wd), F32), pltpu.VMEM((seq, wd), F32),
                        pltpu.VMEM((seq, wd), F32),
                        pltpu.VMEM((nst, seq, hd), F32),
                        pltpu.VMEM((nst, seq, hd), BF16),
                        pltpu.VMEM((nst, seq, DN_CHUNK), BF16),
                        pltpu.VMEM((nst, seq, hd), BF16),
                        pltpu.VMEM((nst, seq, hd), BF16),
                        pltpu.VMEM((nst, n_chunks * SUBLANE, LANE), F32),
                        pltpu.VMEM((nst, seq, hd), F32),
                        pltpu.VMEM((nst, hd, hd), F32)],
        compiler_params=_cparams(("parallel", "parallel")),
        name="deltanet",
    )(proj, proj, proj, proj, col, conv_w, conv_w, conv_w, norm_w.reshape(1, hd), s0)


def _conv_module_kernel(a_ref, g_ref, w_ref, b_ref, lnw_ref, lnb_ref, o_ref, yp_ref, tmp_ref,
                        sh_ref, *, seq, taps):
    d = a_ref.shape[1]
    half = taps // 2
    pad = 2 * SUBLANE
    assert half < pad
    rt = CV_ROW_TILE
    ct = CV_COL_TILE
    zeros = jnp.zeros((pad, d), F32)
    yp_ref[0:pad, :] = zeros
    yp_ref[seq + pad:seq + 2 * pad, :] = zeros

    def fill(t, carry):
        r0 = pl.multiple_of(t * rt, rt)
        a = a_ref[pl.ds(r0, rt), :].astype(F32)
        g = g_ref[pl.ds(r0, rt), :].astype(F32)
        yp_ref[pl.ds(r0 + pad, rt), :] = a * jax.nn.sigmoid(g)
        return carry

    lax.fori_loop(0, seq // rt, fill, 0)

    def tile(t, carry):
        r0 = pl.multiple_of(t * rt, rt)
        for cc in range(d // ct):
            cs = slice(cc * ct, (cc + 1) * ct)
            win = yp_ref[pl.ds(r0, rt + 2 * pad), cs]
            span = rt + 2 * pad - SUBLANE
            for m in range(1, SUBLANE):
                sh_ref[m - 1] = win[m:m + span]
            acc = jnp.zeros((rt, ct), F32) + b_ref[:, cs]
            for j in range(taps):
                off = pad - half + j
                lo = (off // SUBLANE) * SUBLANE
                m = off - lo
                if m == 0:
                    rows = yp_ref[pl.ds(pl.multiple_of(r0 + lo, SUBLANE), rt), cs]
                else:
                    rows = sh_ref[m - 1, lo:lo + rt, :]
                acc = acc + w_ref[j:j + 1, cs] * rows
            tmp_ref[:, cs] = acc
        y = tmp_ref[...]
        mu = jnp.mean(y, axis=-1, keepdims=True)
        yc = y - mu
        var = jnp.mean(yc * yc, axis=-1, keepdims=True)
        y = yc * lax.rsqrt(var + EPS) * lnw_ref[...] + lnb_ref[...]
        o_ref[pl.ds(r0, rt), :] = _silu(y).astype(o_ref.dtype)
        return carry

    lax.fori_loop(0, seq // rt, tile, 0)


def _conv_module(proj, w, b, lnw, lnb, *, batch, seq, col_a):
    t = proj.shape[0]
    taps, d = w.shape
    return pl.pallas_call(
        functools.partial(_conv_module_kernel, seq=seq, taps=taps),
        grid=(batch,),
        in_specs=[pl.BlockSpec((seq, d), lambda i: (i, col_a)),
                  pl.BlockSpec((seq, d), lambda i: (i, col_a + 1)),
                  pl.BlockSpec((taps, d), lambda i: (0, 0)),
                  pl.BlockSpec((1, d), lambda i: (0, 0)),
                  pl.BlockSpec((1, d), lambda i: (0, 0)),
                  pl.BlockSpec((1, d), lambda i: (0, 0))],
        out_specs=pl.BlockSpec((seq, d), lambda i: (i, 0)),
        out_shape=jax.ShapeDtypeStruct((t, d), BF16),
        scratch_shapes=[pltpu.VMEM((seq + 4 * SUBLANE, d), F32),
                        pltpu.VMEM((CV_ROW_TILE, d), F32),
                        pltpu.VMEM((SUBLANE - 1, CV_ROW_TILE + 3 * SUBLANE, CV_COL_TILE), F32)],
        compiler_params=_cparams(("parallel",)),
        name="conv_module",
    )(proj, proj, w, b.reshape(1, d), lnw.reshape(1, d), lnb.reshape(1, d))


def _merge_kernel(s5_ref, dn_ref, cv_ref, g0_ref, g1_ref, g2_ref, w0_ref, w1_ref, w2_ref,
                  wo_ref, x_ref, gate_ref, o_ref):
    merged = (jax.nn.sigmoid(g0_ref[...].astype(F32)) * _dot(s5_ref[...], w0_ref[...])
              + jax.nn.sigmoid(g1_ref[...].astype(F32)) * _dot(dn_ref[...], w1_ref[...])
              + jax.nn.sigmoid(g2_ref[...].astype(F32)) * _dot(cv_ref[...], w2_ref[...]))
    y = _dot(merged.astype(BF16), wo_ref[...])
    o_ref[...] = x_ref[...] + gate_ref[0] * y


def _merge(s5_out, dn_out, cv_out, proj, w_s5, w_dn, w_cv, w_out, x2, mod3, gate_idx,
           *, rows_per_batch, fixed_row, col_gate):
    t, d = x2.shape
    tm = min(512, t if fixed_row is not None else rows_per_batch)
    assert t % tm == 0
    row = _row_of_tile(tm, rows_per_batch, fixed_row)
    act = pl.BlockSpec((tm, d), lambda i: (i, 0))
    gat = lambda k: pl.BlockSpec((tm, d), lambda i: (i, col_gate + k))
    wsp = pl.BlockSpec((d, d), lambda i: (0, 0))
    return pl.pallas_call(
        _merge_kernel,
        grid=(t // tm,),
        in_specs=[act, act, act, gat(0), gat(1), gat(2), wsp, wsp, wsp, wsp, act,
                  pl.BlockSpec((None, 1, d), lambda i: (row(i), 0, gate_idx))],
        out_specs=act,
        out_shape=jax.ShapeDtypeStruct((t, d), F32),
        compiler_params=_cparams(("parallel",)),
        name="merge",
    )(s5_out, dn_out, cv_out, proj, proj, proj, w_s5, w_dn, w_cv, w_out, x2, mod3)


def _ffn_conv_kernel(a_ref, v_ref, w_ref, b_ref, o_ref, a0_ref, al_ref, ar_ref, *, seq, grid2d):
    ct = a_ref.shape[1]
    pad = 72 if grid2d else SUBLANE
    rt = 128
    zeros = jnp.zeros((pad, ct), F32)
    for buf in (a0_ref, al_ref, ar_ref):
        buf[0:pad, :] = zeros
        buf[seq + pad:seq + 2 * pad, :] = zeros

    def fill(t, carry):
        r0 = pl.multiple_of(t * rt, rt)
        a0_ref[pl.ds(r0 + pad, rt), :] = a_ref[pl.ds(r0, rt), :].astype(F32)
        return carry

    lax.fori_loop(0, seq // rt, fill, 0)

    def neighbours(t, carry):
        r0 = pl.multiple_of(t * rt, rt)
        left = _shifted_rows(a0_ref, r0 + pad, -1, rt, slice(None))
        right = _shifted_rows(a0_ref, r0 + pad, 1, rt, slice(None))
        if grid2d:
            colpos = (_iota((rt, ct), 0) + r0) & (GRID_W - 1)
            left = jnp.where(colpos == 0, 0.0, left)
            right = jnp.where(colpos == GRID_W - 1, 0.0, right)
        al_ref[pl.ds(r0 + pad, rt), :] = left
        ar_ref[pl.ds(r0 + pad, rt), :] = right
        return carry

    lax.fori_loop(0, seq // rt, neighbours, 0)

    if grid2d:
        taps = [(dr, dc) for dr in (-1, 0, 1) for dc in (-1, 0, 1)]
    else:
        taps = [(0, dc) for dc in (-1, 0, 1)]
    src = {-1: al_ref, 0: a0_ref, 1: ar_ref}

    def tile(t, carry):
        r0 = pl.multiple_of(t * rt, rt)
        acc = jnp.zeros((rt, ct), F32) + b_ref[...]
        for dr, dc in taps:
            widx = (dr + 1) * 3 + (dc + 1)
            rows = pl.ds(pl.multiple_of(r0 + pad + dr * GRID_W, SUBLANE), rt)
            acc = acc + w_ref[widx:widx + 1, :] * src[dc][rows, :]
        o_ref[pl.ds(r0, rt), :] = (_silu(acc) * v_ref[pl.ds(r0, rt), :].astype(F32)).astype(o_ref.dtype)
        return carry

    lax.fori_loop(0, seq // rt, tile, 0)


def _ffn_conv(av, w9, bias, *, batch, seq, grid2d):
    t = av.shape[0]
    f = w9.shape[1]
    ct = 256
    assert f % ct == 0 and seq % 128 == 0
    nct = f // ct
    pad = 72 if grid2d else SUBLANE
    return pl.pallas_call(
        functools.partial(_ffn_conv_kernel, seq=seq, grid2d=grid2d),
        grid=(batch, nct),
        in_specs=[pl.BlockSpec((seq, ct), lambda b, c: (b, c)),
                  pl.BlockSpec((seq, ct), lambda b, c: (b, nct + c)),
                  pl.BlockSpec((9, ct), lambda b, c: (0, c)),
                  pl.BlockSpec((1, ct), lambda b, c: (0, c))],
        out_specs=pl.BlockSpec((seq, ct), lambda b, c: (b, c)),
        out_shape=jax.ShapeDtypeStruct((t, f), BF16),
        scratch_shapes=[pltpu.VMEM((seq + 2 * pad, ct), F32)] * 3,
        compiler_params=_cparams(("parallel", "parallel")),
        name="ffn_conv",
    )(av, av, w9, bias.reshape(1, f))


def _resid_matmul_kernel(*refs, final_norm):
    if final_norm:
        a_ref, w_ref, x_ref, gate_ref, fw_ref, o_ref = refs
    else:
        a_ref, w_ref, x_ref, gate_ref, o_ref = refs
    y = x_ref[...] + gate_ref[0] * _dot(a_ref[...], w_ref[...])
    if final_norm:
        y = y * lax.rsqrt(jnp.mean(y * y, axis=-1, keepdims=True) + EPS) * fw_ref[...]
    o_ref[...] = y


def _resid_matmul(a, w, x2, mod3, gate_idx, *, rows_per_batch, fixed_row, final_w=None):
    t, d = x2.shape
    k = a.shape[1]
    tm = min(512, t if fixed_row is not None else rows_per_batch)
    assert t % tm == 0
    row = _row_of_tile(tm, rows_per_batch, fixed_row)
    final_norm = final_w is not None
    in_specs = [pl.BlockSpec((tm, k), lambda i: (i, 0)),
                pl.BlockSpec((k, d), lambda i: (0, 0)),
                pl.BlockSpec((tm, d), lambda i: (i, 0)),
                pl.BlockSpec((None, 1, d), lambda i: (row(i), 0, gate_idx))]
    args = [a, w, x2, mod3]
    if final_norm:
        in_specs.append(pl.BlockSpec((1, d), lambda i: (0, 0)))
        args.append(final_w.reshape(1, d))
    return pl.pallas_call(
        functools.partial(_resid_matmul_kernel, final_norm=final_norm),
        grid=(t // tm,),
        in_specs=in_specs,
        out_specs=pl.BlockSpec((tm, d), lambda i: (i, 0)),
        out_shape=jax.ShapeDtypeStruct((t, d), F32),
        compiler_params=_cparams(("parallel",)),
        name="resid_matmul",
    )(*args)


def _group_major(u2, batch, seq, groups):
    nk = seq // S5_CHUNK
    u5 = u2.reshape(batch, nk, S5_CHUNK, groups, S5_GROUP)
    return jnp.transpose(u5, (3, 1, 0, 2, 4)).reshape(groups, nk * batch, S5_CHUNK * S5_GROUP)


def _token_major(yg, batch, seq, groups):
    nk = seq // S5_CHUNK
    y5 = yg.reshape(groups, nk, batch, S5_CHUNK, S5_GROUP)
    return jnp.transpose(y5, (2, 1, 3, 0, 4)).reshape(batch * seq, groups * S5_GROUP)


def kernel(x, c, ctx, c_ctx, ada_w, ada_b, norm1_w, norm2_w, w_in, s5_a_re, s5_a_im, s5_log_dt, s5_b_re, s5_b_im, s5_c_re, s5_c_im, s5_d, s5_w_glu, dn_conv_w, dn_a_log, dn_dt_bias, dn_norm_w, cv_dw_w, cv_dw_b, cv_ln_w, cv_ln_b, w_br_s5, w_br_dn, w_br_cv, w_out, ffn_w_up, ffn_dw_w, ffn_dw_b, ffn_w_down, final_norm_w):
    batch, seq, d = x.shape
    ctx_len = ctx.shape[1]
    depth = ada_w.shape[0]
    groups = s5_a_re.shape[2]
    n_heads = dn_a_log.shape[2]
    dn_width = n_heads * DN_HEAD_DIM
    ffn_hidden = ffn_dw_b.shape[1]
    assert batch == SUBLANE and batch < MOD_ROWS and d == groups * S5_GROUP == dn_width
    assert d % 1024 == 0 and 4 * n_heads <= LANE

    col_qkv = d
    col_beta = col_qkv + 3 * dn_width
    n_state = col_beta + 4 * n_heads
    col_cv = n_state + dn_width
    col_gate = col_cv + 2 * d
    pk_q = d // LANE
    pk_z = (d + 3 * dn_width) // LANE
    pk_cv = (d + 4 * dn_width) // d
    pk_gate = pk_cv + 2

    cmat = jnp.zeros((MOD_ROWS, d), F32).at[:batch].set(c).at[batch].set(c_ctx)
    xl = x.reshape(batch * seq, d)
    xc = ctx.reshape(batch * ctx_len, d)

    for i in range(depth):
        last = i == depth - 1
        w_main = jnp.concatenate([w_in[i][:, :col_beta], w_in[i][:, n_state:]], axis=1).astype(BF16)
        w_bd = jnp.pad(w_in[i][:, col_beta:n_state], ((0, 0), (0, LANE - 4 * n_heads)))
        aneg = jnp.pad(-jnp.exp(dn_a_log[i].reshape(1, -1)), ((0, 0), (2 * n_heads, LANE - 4 * n_heads)))
        dtb = jnp.pad(dn_dt_bias[i].reshape(1, -1), ((0, 0), (2 * n_heads, LANE - 4 * n_heads)))
        s5_par = _s5_params(s5_a_re[i], s5_a_im[i], s5_log_dt[i], s5_b_re[i], s5_b_im[i],
                            s5_c_re[i], s5_c_im[i])
        w_glu = s5_w_glu[i].astype(BF16)
        w_brs = w_br_s5[i].astype(BF16)
        w_brd = w_br_dn[i].astype(BF16)
        w_brc = w_br_cv[i].astype(BF16)
        w_o = w_out[i].astype(BF16)
        w_up = ffn_w_up[i].astype(BF16)
        w_dn = ffn_w_down[i].astype(BF16)
        w9 = ffn_dw_w[i].reshape(9, ffn_hidden)

        mod3 = _modulation(cmat, ada_w[i], ada_b[i]).reshape(MOD_ROWS, 1, N_MOD * d)

        rows_c = dict(rows_per_batch=ctx_len, fixed_row=batch)
        rows_l = dict(rows_per_batch=seq, fixed_row=None)
        proj_c, bd_c = _norm_matmul(xc, norm1_w[i], mod3, 0, 1, w_main, wbd=w_bd, tn=2 * d,
                                    **rows_c)
        proj_l, bd_l = _norm_matmul(xl, norm1_w[i], mod3, 0, 1, w_main, wbd=w_bd, tn=2 * d,
                                    **rows_l)

        u_rows = jnp.concatenate([_group_major(proj_c[:, :d], batch, ctx_len, groups),
                                  _group_major(proj_l[:, :d], batch, seq, groups)], axis=1)
        rows_ctx = (ctx_len // S5_CHUNK) * batch
        segments = ((0, ctx_len // S5_CHUNK), (rows_ctx, seq // S5_CHUNK))
        y_rows = _s5_core(u_rows, *s5_par, segments=segments, batch=batch)

        zero_dn = jnp.zeros((batch, 2, n_heads, DN_HEAD_DIM, DN_HEAD_DIM), F32)
        dn_args = dict(batch=batch, n_heads=n_heads, col_q=pk_q, col_z=pk_z)
        dn_c, dn_state = _deltanet(proj_c, _dn_gates(bd_c, aneg, dtb, n_heads), dn_conv_w[i],
                                   dn_norm_w[i], zero_dn, seq=ctx_len, **dn_args)
        dn_l, _ = _deltanet(proj_l, _dn_gates(bd_l, aneg, dtb, n_heads), dn_conv_w[i],
                            dn_norm_w[i], dn_state, seq=seq, **dn_args)

        def finish(x2, proj, y, dn_out, length, rows, grid2d, final_w):
            s5_out = _s5_post(y, proj, s5_d[i], w_glu)
            cv_out = _conv_module(proj, cv_dw_w[i], cv_dw_b[i], cv_ln_w[i], cv_ln_b[i],
                                  batch=batch, seq=length, col_a=pk_cv)
            x2 = _merge(s5_out, dn_out, cv_out, proj, w_brs, w_brd, w_brc, w_o, x2, mod3, 2,
                        col_gate=pk_gate, **rows)
            av = _norm_matmul(x2, norm2_w[i], mod3, 3, 4, w_up, tn=w_up.shape[1] // 4, **rows)
            hidden = _ffn_conv(av, w9, ffn_dw_b[i], batch=batch, seq=length, grid2d=grid2d)
            return _resid_matmul(hidden, w_dn, x2, mod3, 5, final_w=final_w, **rows)

        if not last:
            xc = finish(xc, proj_c, _token_major(y_rows[:, :rows_ctx], batch, ctx_len, groups),
                        dn_c, ctx_len, rows_c, False, None)
        xl = finish(xl, proj_l, _token_major(y_rows[:, rows_ctx:], batch, seq, groups), dn_l, seq,
                    rows_l, True, final_norm_w if last else None)

    return xl.reshape(batch, seq, d)
```

```python
import functools
import math

import jax
import jax.numpy as jnp
from jax import lax
from jax.experimental import pallas as pl
from jax.experimental.pallas import tpu as pltpu

F32 = jnp.float32
BF16 = jnp.bfloat16
HIGHEST = lax.Precision.HIGHEST

EPS = 1e-6
GRID_W = 64
S5_GROUP = 16
DN_HEAD_DIM = 128
N_MOD = 6
N_BRANCH = 3

LANE = 128
SUBLANE = 8
S5_CHUNK = 16
DN_CHUNK = 64
MOD_ROWS = 16
CV_ROW_TILE = 64
CV_COL_TILE = 256
VMEM_LIMIT = 56 * 1024 * 1024


def _cparams(sem, vmem=VMEM_LIMIT):
    return pltpu.CompilerParams(dimension_semantics=sem, vmem_limit_bytes=vmem)


def _dot(a, b, precision=None):
    return jnp.dot(a, b, preferred_element_type=F32, precision=precision)


def _dot_nt(a, b, precision=None):
    return lax.dot_general(a, b, (((1,), (1,)), ((), ())), preferred_element_type=F32,
                           precision=precision)


def _dot_tn(a, b, precision=None):
    return lax.dot_general(a, b, (((0,), (0,)), ((), ())), preferred_element_type=F32,
                           precision=precision)


def _silu(x):
    return x * jax.nn.sigmoid(x)


def _iota(shape, dim):
    return lax.broadcasted_iota(jnp.int32, shape, dim)


def _shifted_rows(ref, base, off, rows, cols):
    lo = (off // SUBLANE) * SUBLANE
    shift = off - lo
    start = pl.multiple_of(base + lo, SUBLANE)
    if shift == 0:
        return ref[pl.ds(start, rows), cols]
    return ref[pl.ds(start, rows + SUBLANE), cols][shift:shift + rows]


def _mod_kernel(c_ref, w_ref, b_ref, o_ref):
    o_ref[...] = _dot(_silu(c_ref[...]), w_ref[...], HIGHEST) + b_ref[...]


def _modulation(cmat, ada_w, ada_b):
    rows, d = cmat.shape
    n = ada_w.shape[1]
    tn = 1024
    return pl.pallas_call(
        _mod_kernel,
        grid=(n // tn,),
        in_specs=[pl.BlockSpec((rows, d), lambda j: (0, 0)),
                  pl.BlockSpec((d, tn), lambda j: (0, j)),
                  pl.BlockSpec((1, tn), lambda j: (0, j))],
        out_specs=pl.BlockSpec((rows, tn), lambda j: (0, j)),
        out_shape=jax.ShapeDtypeStruct((rows, n), F32),
        compiler_params=_cparams(("parallel",)),
        name="modulation",
    )(cmat, ada_w, ada_b.reshape(1, n))


def _norm_matmul_kernel(*refs, with_bd):
    if with_bd:
        x_ref, nw_ref, sh_ref, sc_ref, w_ref, wbd_ref, o_ref, bd_ref, h_ref = refs
    else:
        x_ref, nw_ref, sh_ref, sc_ref, w_ref, o_ref, h_ref = refs

    @pl.when(pl.program_id(1) == 0)
    def _():
        x = x_ref[...]
        h = x * lax.rsqrt(jnp.mean(x * x, axis=-1, keepdims=True) + EPS) * nw_ref[...]
        h = h * (1.0 + sc_ref[0]) + sh_ref[0]
        h_ref[...] = h.astype(BF16)
        if with_bd:
            bd_ref[...] = _dot(h, wbd_ref[...], HIGHEST)

    o_ref[...] = _dot(h_ref[...], w_ref[...]).astype(o_ref.dtype)


def _row_of_tile(tm, rows_per_batch, fixed_row):
    if fixed_row is not None:
        return lambda i: fixed_row
    return lambda i: (i * tm) // rows_per_batch


def _norm_matmul(x2, norm_w, mod3, sh_idx, sc_idx, w, *, rows_per_batch, fixed_row, wbd=None,
                 tn):
    t, d = x2.shape
    n = w.shape[1]
    tm = min(1024, t if fixed_row is not None else rows_per_batch)
    assert t % tm == 0 and n % tn == 0
    row = _row_of_tile(tm, rows_per_batch, fixed_row)
    with_bd = wbd is not None
    in_specs = [pl.BlockSpec((tm, d), lambda i, j: (i, 0)),
                pl.BlockSpec((1, d), lambda i, j: (0, 0)),
                pl.BlockSpec((None, 1, d), lambda i, j: (row(i), 0, sh_idx)),
                pl.BlockSpec((None, 1, d), lambda i, j: (row(i), 0, sc_idx)),
                pl.BlockSpec((d, tn), lambda i, j: (0, j))]
    args = [x2, norm_w.reshape(1, d), mod3, mod3, w]
    out_specs = [pl.BlockSpec((tm, tn), lambda i, j: (i, j))]
    out_shape = [jax.ShapeDtypeStruct((t, n), BF16)]
    if with_bd:
        in_specs.append(pl.BlockSpec((d, LANE), lambda i, j: (0, 0)))
        args.append(wbd)
        out_specs.append(pl.BlockSpec((tm, LANE), lambda i, j: (i, 0)))
        out_shape.append(jax.ShapeDtypeStruct((t, LANE), F32))
    res = pl.pallas_call(
        functools.partial(_norm_matmul_kernel, with_bd=with_bd),
        grid=(t // tm, n // tn),
        in_specs=in_specs,
        out_specs=out_specs,
        out_shape=out_shape,
        scratch_shapes=[pltpu.VMEM((tm, d), BF16)],
        compiler_params=_cparams(("parallel", "arbitrary")),
        name="norm_matmul",
    )(*args)
    return res if with_bd else res[0]


def _dn_gates_kernel(bd_ref, aneg_ref, dtb_ref, o_ref, *, n_heads):
    x = bd_ref[...]
    tm = x.shape[0]
    y = x + dtb_ref[...]
    g = aneg_ref[...] * (jnp.maximum(y, 0.0) + jnp.log1p(jnp.exp(-jnp.abs(y))))
    blk = 2 * DN_CHUNK
    r = _iota((blk, blk), 0)
    c = _iota((blk, blk), 1)
    shift = DN_CHUNK.bit_length() - 1
    same = (r >> shift) == (c >> shift)
    tri_lo = jnp.where(same & (c <= r), 1.0, 0.0)
    tri_hi = jnp.where(same & (c >= r), 1.0, 0.0)
    parts = [g[t * blk:(t + 1) * blk] for t in range(tm // blk)]
    prefix = jnp.concatenate([_dot(tri_lo, p, HIGHEST) for p in parts], axis=0)
    suffix = jnp.concatenate([_dot(tri_hi, p, HIGHEST) for p in parts], axis=0)
    lane = _iota(x.shape, 1)
    o_ref[...] = jnp.where(lane < 2 * n_heads, jax.nn.sigmoid(x),
                           jnp.where(lane < 3 * n_heads, prefix, suffix))


def _dn_gates(bd, aneg, dtb, n_heads):
    t = bd.shape[0]
    tm = min(512, t)
    assert t % tm == 0 and tm % (2 * DN_CHUNK) == 0
    return pl.pallas_call(
        functools.partial(_dn_gates_kernel, n_heads=n_heads),
        grid=(t // tm,),
        in_specs=[pl.BlockSpec((tm, LANE), lambda i: (i, 0)),
                  pl.BlockSpec((1, LANE), lambda i: (0, 0)),
                  pl.BlockSpec((1, LANE), lambda i: (0, 0))],
        out_specs=pl.BlockSpec((tm, LANE), lambda i: (i, 0)),
        out_shape=jax.ShapeDtypeStruct((t, LANE), F32),
        compiler_params=_cparams(("parallel",)),
        name="dn_gates",
    )(bd, aneg, dtb)


def _s5_kernel(u_ref, bt_ref, c_ref, pw_ref, a_ref, y_ref, s_ref, hp_ref, *, segments, batch):
    tc, cg = S5_CHUNK, S5_GROUP
    half = LANE // 2
    n_grp = u_ref.shape[0]

    def operators(g):
        btr = [bt_ref[g, d, 0] for d in range(2)]
        bti = [bt_ref[g, d, 1] for d in range(2)]
        cr, ci = c_ref[g, 0], c_ref[g, 1]
        pr = [pw_ref[g, d, 0] for d in range(2)]
        pi = [pw_ref[g, d, 1] for d in range(2)]
        l_re, l_im = [], []
        for m in range(2 * tc):
            tau = tc - 1 - m
            d = 0 if tau >= 0 else 1
            if tau == 0:
                l_re.append(btr[0] + btr[1])
                l_im.append(bti[0] + bti[1])
            elif tau == -tc:
                l_re.append(jnp.zeros_like(btr[0]))
                l_im.append(jnp.zeros_like(btr[0]))
            else:
                p_r, p_i = pr[d][abs(tau):abs(tau) + 1, :], pi[d][abs(tau):abs(tau) + 1, :]
                l_re.append(p_r * btr[d] - p_i * bti[d])
                l_im.append(p_r * bti[d] + p_i * btr[d])
        l_re = jnp.concatenate(l_re, axis=0)
        l_im = jnp.concatenate(l_im, axis=0)
        kxt = _dot_nt(cr, l_re, HIGHEST) - _dot_nt(ci, l_im, HIGHEST)
        wt_t = jnp.concatenate([kxt[:, cg * (tc - 1 - j):cg * (tc - 1 - j) + tc * cg]
                                for j in range(tc)], axis=0)
        win_rows, wst_rows = [], []
        for i in range(tc):
            pf_r, pf_i = pr[0][tc - 1 - i:tc - i, :], pi[0][tc - 1 - i:tc - i, :]
            pb_r, pb_i = pr[1][i:i + 1, :], pi[1][i:i + 1, :]
            win_rows.append(jnp.concatenate([
                pf_r * btr[0] - pf_i * bti[0], pb_r * btr[1] - pb_i * bti[1],
                pf_r * bti[0] + pf_i * btr[0], pb_r * bti[1] + pb_i * btr[1]], axis=1))
            qf_r, qf_i = pr[0][i + 1:i + 2, :], pi[0][i + 1:i + 2, :]
            qb_r, qb_i = pr[1][tc - i:tc - i + 1, :], pi[1][tc - i:tc - i + 1, :]
            wst_rows.append(jnp.concatenate([
                cr * qf_r - ci * qf_i, cr * qb_r - ci * qb_i,
                -(cr * qf_i + ci * qf_r), -(cr * qb_i + ci * qb_r)], axis=1))
        win = jnp.concatenate(win_rows, axis=0)
        wst_t = jnp.concatenate(wst_rows, axis=0)
        return wt_t.astype(BF16), win.astype(BF16), wst_t.astype(BF16)

    ops = [operators(g) for g in range(n_grp)]
    for g in range(n_grp):
        s_ref[g] = _dot(u_ref[g], ops[g][1])
    ars = [a_ref[g, 0:1, :] for g in range(n_grp)]
    ais = [a_ref[g, 1:2, :] for g in range(n_grp)]
    is_fwd = _iota((batch, LANE), 1) < half
    carry = tuple(jnp.zeros((batch, LANE), F32) for _ in range(2 * n_grp))
    for row0, n_chunks in segments:
        def body(s, carry, row0=row0, n_chunks=n_chunks):
            rf = pl.ds(pl.multiple_of(row0 + s * batch, batch), batch)
            rb = pl.ds(pl.multiple_of(row0 + (n_chunks - 1 - s) * batch, batch), batch)
            new = []
            for g in range(n_grp):
                hre, him = carry[2 * g], carry[2 * g + 1]
                hp_ref[g, rf, 0:half] = hre[:, 0:half]
                hp_ref[g, rb, half:LANE] = hre[:, half:LANE]
                hp_ref[g, rf, LANE:LANE + half] = him[:, 0:half]
                hp_ref[g, rb, LANE + half:2 * LANE] = him[:, half:LANE]
                sf = s_ref[g, rf, :]
                sb = s_ref[g, rb, :]
                sre = jnp.where(is_fwd, sf[:, 0:LANE], sb[:, 0:LANE])
                sim = jnp.where(is_fwd, sf[:, LANE:2 * LANE], sb[:, LANE:2 * LANE])
                new.append(ars[g] * hre - ais[g] * him + sre)
                new.append(ars[g] * him + ais[g] * hre + sim)
            return tuple(new)

        carry = lax.fori_loop(0, n_chunks, body, carry)
    for g in range(n_grp):
        y = _dot_nt(u_ref[g], ops[g][0]) + _dot_nt(hp_ref[g].astype(BF16), ops[g][2])
        y_ref[g] = y.astype(y_ref.dtype)


S5_GROUPS_PER_STEP = 4


def _s5_core(ug, bt, cc, pw, avec, *, segments, batch):
    g, rows, k = ug.shape
    gb = S5_GROUPS_PER_STEP
    assert batch == SUBLANE and k == 2 * LANE and g % gb == 0
    act = pl.BlockSpec((gb, rows, k), lambda i: (i, 0, 0))
    per_group = lambda a: pl.BlockSpec((gb,) + a.shape[1:],
                                       lambda i: (i,) + (0,) * (a.ndim - 1))
    return pl.pallas_call(
        functools.partial(_s5_kernel, segments=segments, batch=batch),
        grid=(g // gb,),
        in_specs=[act, per_group(bt), per_group(cc), per_group(pw), per_group(avec)],
        out_specs=act,
        out_shape=jax.ShapeDtypeStruct((g, rows, k), BF16),
        scratch_shapes=[pltpu.VMEM((gb, rows, k), F32), pltpu.VMEM((gb, rows, k), F32)],
        compiler_params=_cparams(("parallel",)),
        name="s5_core",
    )(ug, bt, cc, pw, avec)


def _s5_params(a_re, a_im, log_dt, b_re, b_im, c_re, c_im):
    tc = S5_CHUNK
    dt = jnp.exp(log_dt)[..., None]
    mag = jnp.exp(a_re * dt)
    abr, abi = mag * jnp.cos(a_im * dt), mag * jnp.sin(a_im * dt)
    den = a_re * a_re + a_im * a_im
    cr = ((abr - 1.0) * a_re + abi * a_im) / den
    ci = (abi * a_re - (abr - 1.0) * a_im) / den
    bbr = cr[..., None] * b_re - ci[..., None] * b_im
    bbi = cr[..., None] * b_im + ci[..., None] * b_re
    n = jnp.arange(tc + 1, dtype=F32)[:, None, None, None]
    pmag = jnp.exp(a_re * dt * n)
    pr, pi = pmag * jnp.cos(a_im * dt * n), pmag * jnp.sin(a_im * dt * n)
    bt = jnp.stack([jnp.swapaxes(bbr, 2, 3), jnp.swapaxes(bbi, 2, 3)], axis=2)
    bt = jnp.transpose(bt, (1, 0, 2, 3, 4))
    cc = jnp.stack([c_re, c_im], axis=1)
    pw = jnp.transpose(jnp.stack([pr, pi], axis=0), (3, 2, 0, 1, 4))
    pw = jnp.pad(pw, ((0, 0), (0, 0), (0, 0), (0, SUBLANE - 1), (0, 0)))
    avec = jnp.stack([jnp.concatenate([pr[tc, 0], pr[tc, 1]], axis=-1),
                      jnp.concatenate([pi[tc, 0], pi[tc, 1]], axis=-1)], axis=1)
    return bt, cc, pw, avec


def _s5_post_kernel(y_ref, u_ref, d_ref, w_ref, o_ref):
    y = y_ref[...].astype(F32) + d_ref[...] * u_ref[...].astype(F32)
    k0 = math.sqrt(2.0 / math.pi)
    g = 0.5 * y * (1.0 + jnp.tanh(k0 * (y + 0.044715 * (y * y * y))))
    o_ref[...] = (g * jax.nn.sigmoid(_dot(g.astype(BF16), w_ref[...]))).astype(o_ref.dtype)


def _s5_post(y, proj, s5_d, w_glu):
    t, d = y.shape
    tm = min(512, t)
    return pl.pallas_call(
        _s5_post_kernel,
        grid=(t // tm,),
        in_specs=[pl.BlockSpec((tm, d), lambda i: (i, 0)),
                  pl.BlockSpec((tm, d), lambda i: (i, 0)),
                  pl.BlockSpec((1, d), lambda i: (0, 0)),
                  pl.BlockSpec((d, d), lambda i: (0, 0))],
        out_specs=pl.BlockSpec((tm, d), lambda i: (i, 0)),
        out_shape=jax.ShapeDtypeStruct((t, d), BF16),
        compiler_params=_cparams(("parallel",)),
        name="s5_post",
    )(y, proj, s5_d.reshape(1, d), w_glu)


def _unit_tri_inverses(ms, r, c):
    mm = lambda a, b: _dot(a.astype(BF16), b.astype(BF16))
    eye = jnp.where(r == c, 1.0, 0.0)
    blk = lambda s: (r >> s) == (c >> s)
    mds = [jnp.where(blk(3), m, 0.0) for m in ms]
    m2s = [mm(md, md) for md in mds]
    m4s = [mm(m2, m2) for m2 in m2s]
    ts = [eye - md for md in mds]
    ts = [t + mm(t, m2) for t, m2 in zip(ts, m2s)]
    ts = [t + mm(t, m4) for t, m4 in zip(ts, m4s)]
    for s in (3, 4, 5):
        off = blk(s + 1) & jnp.logical_not(blk(s))
        xs = [mm(jnp.where(off, m, 0.0), t) for m, t in zip(ms, ts)]
        ts = [t - mm(t, x) for t, x in zip(ts, xs)]
    return ts


def _dn_kernel(q_ref, k_ref, v_ref, z_ref, col_ref, cwq_ref, cwk_ref, cwv_ref, nw_ref, s0_ref,
               o_ref, sfin_ref, xp_ref, qs_ref, ks_ref, vs_ref, u_ref, w_ref, a_ref, qg_ref,
               kt_ref, el_ref, od_ref, s_ref, *, seq, n_heads, hp, ca):
    c_sz = DN_CHUNK
    hd = DN_HEAD_DIM
    pad = SUBLANE
    rt = min(256, seq)
    n_chunks = seq // c_sz
    head0 = pl.program_id(1) * hp

    def prep(x_ref, cw_ref, dst_ref, normalise, scale):
        zeros = jnp.zeros((pad, hp * hd), F32)
        xp_ref[0:pad, :] = zeros
        xp_ref[seq + pad:seq + 2 * pad, :] = zeros
        xp_ref[pad:seq + pad, :] = x_ref[...].astype(F32)
        w = cw_ref[...]
        for t in range(seq // rt):
            r0 = t * rt
            acc = (w[0:1, :] * xp_ref[r0 + pad - 1:r0 + pad - 1 + rt, :]
                   + w[1:2, :] * xp_ref[r0 + pad:r0 + pad + rt, :]
                   + w[2:3, :] * xp_ref[r0 + pad + 1:r0 + pad + 1 + rt, :])
            y = _silu(acc)
            if normalise:
                parts = []
                for hl in range(hp):
                    yh = y[:, hl * hd:(hl + 1) * hd]
                    parts.append(yh * (lax.rsqrt(jnp.sum(yh * yh, axis=-1, keepdims=True) + EPS)
                                       * scale))
                y = jnp.concatenate(parts, axis=1)
            dst_ref[r0:r0 + rt, :] = y

    prep(q_ref, cwq_ref, qs_ref, True, hd ** -0.5)
    prep(k_ref, cwk_ref, ks_ref, True, 1.0)
    prep(v_ref, cwv_ref, vs_ref, False, 1.0)

    r = _iota((c_sz, c_sz), 0)
    c = _iota((c_sz, c_sz), 1)
    lane = _iota((c_sz, LANE), 1)

    def pick(col, idx):
        v = jnp.sum(jnp.where(lane == idx, col, 0.0), axis=-1, keepdims=True)
        return jnp.broadcast_to(v, (c_sz, LANE))

    def phase_a(it, carry):
        pairs = []
        for cc in range(ca):
            chunk = it * ca + cc
            rows = pl.ds(pl.multiple_of(chunk * c_sz, c_sz), c_sz)
            col = col_ref[rows, :]
            for hl in range(hp):
                hs = slice(hl * hd, (hl + 1) * hd)
                pairs.append((chunk, rows, hl, col, ks_ref[rows, hs], qs_ref[rows, hs],
                              vs_ref[rows, hs]))
        k16s = [p[4].astype(BF16) for p in pairs]
        kkts = [_dot_nt(k16, k16) for k16 in k16s]
        qkts = [_dot_nt(p[5].astype(BF16), k16) for p, k16 in zip(pairs, k16s)]
        chains = []
        for pi, (chunk, rows, hl, col, kc, qc, vc) in enumerate(pairs):
            for direction in range(2):
                lower = direction == 0
                bc = pick(col, head0 + hl + direction * n_heads)
                gc = pick(col, head0 + hl + (2 + direction) * n_heads)
                grow = jnp.concatenate([gc, gc], axis=0).T[0:c_sz, 0:c_sz]
                incl = (r >= c) if lower else (r <= c)
                strict = (r > c) if lower else (r < c)
                decay = jnp.where(incl, jnp.exp(jnp.minimum(gc[:, 0:c_sz] - grow, 0.0)), 0.0)
                m = jnp.where(strict, bc[:, 0:c_sz] * kkts[pi] * decay, 0.0)
                chains.append((pi, direction, bc, gc, decay, m))
        tinvs = _unit_tri_inverses([ch[5] for ch in chains], r, c)
        egs = [jnp.exp(ch[3]) for ch in chains]
        uws = []
        for (pi, direction, bc, gc, decay, m), tinv, eg in zip(chains, tinvs, egs):
            kc, vc = pairs[pi][4], pairs[pi][6]
            rhs = jnp.concatenate([vc * bc, kc * bc * eg], axis=1).astype(BF16)
            uws.append(_dot(tinv.astype(BF16), rhs))
        for (pi, direction, bc, gc, decay, m), eg, uw in zip(chains, egs, uws):
            chunk, rows, hl, col, kc, qc, vc = pairs[pi]
            idx = hl * 2 + direction
            g_last = gc[c_sz - 1:c_sz, :] if direction == 0 else gc[0:1, :]
            u_ref[idx, rows, :] = uw[:, 0:hd]
            w_ref[idx, rows, :] = uw[:, hd:2 * hd].astype(BF16)
            a_ref[idx, rows, :] = (qkts[pi] * decay).astype(BF16)
            qg_ref[idx, rows, :] = (qc * eg).astype(BF16)
            kt_ref[idx, rows, :] = (kc * jnp.exp(g_last - gc)).astype(BF16)
            el_ref[idx, pl.ds(pl.multiple_of(chunk * SUBLANE, SUBLANE), SUBLANE), :] = (
                jnp.broadcast_to(jnp.exp(g_last), (SUBLANE, LANE)))
        return carry

    lax.fori_loop(0, n_chunks // ca, phase_a, 0)

    for hl in range(hp):
        for direction in range(2):
            s_ref[hl * 2 + direction] = s0_ref[direction, hl]

    def phase_b(step, carry):
        ids, rows, erows = [], [], []
        for hl in range(hp):
            for direction in range(2):
                chunk = step if direction == 0 else n_chunks - 1 - step
                ids.append(hl * 2 + direction)
                rows.append(pl.ds(pl.multiple_of(chunk * c_sz, c_sz), c_sz))
                erows.append(pl.ds(pl.multiple_of(chunk * SUBLANE, SUBLANE), SUBLANE))
        ss = [s_ref[i] for i in ids]
        s16s = [s.astype(BF16) for s in ss]
        wss = [_dot(w_ref[i, rw, :], s16) for i, rw, s16 in zip(ids, rows, s16s)]
        qss = [_dot(qg_ref[i, rw, :], s16) for i, rw, s16 in zip(ids, rows, s16s)]
        vns = [(u_ref[i, rw, :] - ws).astype(BF16) for i, rw, ws in zip(ids, rows, wss)]
        avs = [_dot(a_ref[i, rw, :], vn) for i, rw, vn in zip(ids, rows, vns)]
        kvs = [_dot_tn(kt_ref[i, rw, :], vn) for i, rw, vn in zip(ids, rows, vns)]
        for i, rw, er, s, qs_, av, kv in zip(ids, rows, erows, ss, qss, avs, kvs):
            od_ref[i, rw, :] = qs_ + av
            s_ref[i] = s * el_ref[i, er, :][0:1, :] + kv
        return carry

    lax.fori_loop(0, n_chunks, phase_b, 0)

    for hl in range(hp):
        for direction in range(2):
            sfin_ref[direction, hl] = s_ref[hl * 2 + direction]

    for t in range(seq // rt):
        r0 = t * rt
        for hl in range(hp):
            hs = slice(hl * hd, (hl + 1) * hd)
            o = od_ref[hl * 2, r0:r0 + rt, :] + od_ref[hl * 2 + 1, r0:r0 + rt, :]
            o = o * lax.rsqrt(jnp.mean(o * o, axis=-1, keepdims=True) + EPS) * nw_ref[...]
            o_ref[r0:r0 + rt, hs] = (o * _silu(z_ref[r0:r0 + rt, hs].astype(F32))).astype(o_ref.dtype)


def _deltanet(proj, col, conv_w, norm_w, s0, *, batch, seq, n_heads, col_q, col_z):
    t = proj.shape[0]
    hd = DN_HEAD_DIM
    hp = 2
    d = n_heads * hd
    wd = hp * hd
    n_chunks = seq // DN_CHUNK
    ca = min(8, n_chunks)
    assert n_heads % hp == 0 and col_q % hp == 0 and col_z % hp == 0 and n_chunks % ca == 0
    blk = lambda off: pl.BlockSpec((seq, wd), lambda b, h: (b, off // hp + h))
    cw = lambda off: pl.BlockSpec((3, wd), lambda b, h: (0, off // hp + h))
    st = pl.BlockSpec((None, 2, hp, hd, hd), lambda b, h: (b, 0, h, 0, 0))
    nst = 2 * hp
    return pl.pallas_call(
        functools.partial(_dn_kernel, seq=seq, n_heads=n_heads, hp=hp, ca=ca),
        grid=(batch, n_heads // hp),
        in_specs=[blk(col_q), blk(col_q + n_heads), blk(col_q + 2 * n_heads), blk(col_z),
                  pl.BlockSpec((seq, LANE), lambda b, h: (b, 0)),
                  cw(0), cw(n_heads), cw(2 * n_heads),
                  pl.BlockSpec((1, hd), lambda b, h: (0, 0)),
                  st],
        out_specs=[pl.BlockSpec((seq, wd), lambda b, h: (b, h)), st],
        out_shape=[jax.ShapeDtypeStruct((t, d), BF16),
                   jax.ShapeDtypeStruct(s0.shape, F32)],
        scratch_shapes=[pltpu.VMEM((seq + 2 * SUBLANE, wd), F32),
                        pltpu.VMEM((seq, wd), F32), pltpu.VMEM((seq, wd), F32),
                        pltpu.VMEM((seq, wd), F32),
                        pltpu.VMEM((nst, seq, hd), F32),
                        pltpu.VMEM((nst, seq, hd), BF16),
                        pltpu.VMEM((nst, seq, DN_CHUNK), BF16),
                        pltpu.VMEM((nst, seq, hd), BF16),
                        pltpu.VMEM((nst, seq, hd), BF16),
                        pltpu.VMEM((nst, n_chunks * SUBLANE, LANE), F32),
                        pltpu.VMEM((nst, seq, hd), F32),
                        pltpu.VMEM((nst, hd, hd), F32)],
        compiler_params=_cparams(("parallel", "parallel")),
        name="deltanet",
    )(proj, proj, proj, proj, col, conv_w, conv_w, conv_w, norm_w.reshape(1, hd), s0)


def _conv_module_kernel(a_ref, g_ref, w_ref, b_ref, lnw_ref, lnb_ref, o_ref, yp_ref, tmp_ref,
                        sh_ref, *, seq, taps):
    d = a_ref.shape[1]
    half = taps // 2
    pad = 2 * SUBLANE
    assert half < pad
    rt = CV_ROW_TILE
    ct = CV_COL_TILE
    zeros = jnp.zeros((pad, d), F32)
    yp_ref[0:pad, :] = zeros
    yp_ref[seq + pad:seq + 2 * pad, :] = zeros

    def fill(t, carry):
        r0 = pl.multiple_of(t * rt, rt)
        a = a_ref[pl.ds(r0, rt), :].astype(F32)
        g = g_ref[pl.ds(r0, rt), :].astype(F32)
        yp_ref[pl.ds(r0 + pad, rt), :] = a * jax.nn.sigmoid(g)
        return carry

    lax.fori_loop(0, seq // rt, fill, 0)

    def tile(t, carry):
        r0 = pl.multiple_of(t * rt, rt)
        for cc in range(d // ct):
            cs = slice(cc * ct, (cc + 1) * ct)
            win = yp_ref[pl.ds(r0, rt + 2 * pad), cs]
            span = rt + 2 * pad - SUBLANE
            for m in range(1, SUBLANE):
                sh_ref[m - 1] = win[m:m + span]
            acc = jnp.zeros((rt, ct), F32) + b_ref[:, cs]
            for j in range(taps):
                off = pad - half + j
                lo = (off // SUBLANE) * SUBLANE
                m = off - lo
                if m == 0:
                    rows = yp_ref[pl.ds(pl.multiple_of(r0 + lo, SUBLANE), rt), cs]
                else:
                    rows = sh_ref[m - 1, lo:lo + rt, :]
                acc = acc + w_ref[j:j + 1, cs] * rows
            tmp_ref[:, cs] = acc
        y = tmp_ref[...]
        mu = jnp.mean(y, axis=-1, keepdims=True)
        yc = y - mu
        var = jnp.mean(yc * yc, axis=-1, keepdims=True)
        y = yc * lax.rsqrt(var + EPS) * lnw_ref[...] + lnb_ref[...]
        o_ref[pl.ds(r0, rt), :] = _silu(y).astype(o_ref.dtype)
        return carry

    lax.fori_loop(0, seq // rt, tile, 0)


def _conv_module(proj, w, b, lnw, lnb, *, batch, seq, col_a):
    t = proj.shape[0]
    taps, d = w.shape
    return pl.pallas_call(
        functools.partial(_conv_module_kernel, seq=seq, taps=taps),
        grid=(batch,),
        in_specs=[pl.BlockSpec((seq, d), lambda i: (i, col_a)),
                  pl.BlockSpec((seq, d), lambda i: (i, col_a + 1)),
                  pl.BlockSpec((taps, d), lambda i: (0, 0)),
                  pl.BlockSpec((1, d), lambda i: (0, 0)),
                  pl.BlockSpec((1, d), lambda i: (0, 0)),
                  pl.BlockSpec((1, d), lambda i: (0, 0))],
        out_specs=pl.BlockSpec((seq, d), lambda i: (i, 0)),
        out_shape=jax.ShapeDtypeStruct((t, d), BF16),
        scratch_shapes=[pltpu.VMEM((seq + 4 * SUBLANE, d), F32),
                        pltpu.VMEM((CV_ROW_TILE, d), F32),
                        pltpu.VMEM((SUBLANE - 1, CV_ROW_TILE + 3 * SUBLANE, CV_COL_TILE), F32)],
        compiler_params=_cparams(("parallel",)),
        name="conv_module",
    )(proj, proj, w, b.reshape(1, d), lnw.reshape(1, d), lnb.reshape(1, d))


def _merge_kernel(s5_ref, dn_ref, cv_ref, g0_ref, g1_ref, g2_ref, w0_ref, w1_ref, w2_ref,
                  wo_ref, x_ref, gate_ref, o_ref):
    merged = (jax.nn.sigmoid(g0_ref[...].astype(F32)) * _dot(s5_ref[...], w0_ref[...])
              + jax.nn.sigmoid(g1_ref[...].astype(F32)) * _dot(dn_ref[...], w1_ref[...])
              + jax.nn.sigmoid(g2_ref[...].astype(F32)) * _dot(cv_ref[...], w2_ref[...]))
    y = _dot(merged.astype(BF16), wo_ref[...])
    o_ref[...] = x_ref[...] + gate_ref[0] * y


def _merge(s5_out, dn_out, cv_out, proj, w_s5, w_dn, w_cv, w_out, x2, mod3, gate_idx,
           *, rows_per_batch, fixed_row, col_gate):
    t, d = x2.shape
    tm = min(512, t if fixed_row is not None else rows_per_batch)
    assert t % tm == 0
    row = _row_of_tile(tm, rows_per_batch, fixed_row)
    act = pl.BlockSpec((tm, d), lambda i: (i, 0))
    gat = lambda k: pl.BlockSpec((tm, d), lambda i: (i, col_gate + k))
    wsp = pl.BlockSpec((d, d), lambda i: (0, 0))
    return pl.pallas_call(
        _merge_kernel,
        grid=(t // tm,),
        in_specs=[act, act, act, gat(0), gat(1), gat(2), wsp, wsp, wsp, wsp, act,
                  pl.BlockSpec((None, 1, d), lambda i: (row(i), 0, gate_idx))],
        out_specs=act,
        out_shape=jax.ShapeDtypeStruct((t, d), F32),
        compiler_params=_cparams(("parallel",)),
        name="merge",
    )(s5_out, dn_out, cv_out, proj, proj, proj, w_s5, w_dn, w_cv, w_out, x2, mod3)


def _ffn_conv_kernel(a_ref, v_ref, w_ref, b_ref, o_ref, a0_ref, al_ref, ar_ref, *, seq, grid2d):
    ct = a_ref.shape[1]
    pad = 72 if grid2d else SUBLANE
    rt = 128
    zeros = jnp.zeros((pad, ct), F32)
    for buf in (a0_ref, al_ref, ar_ref):
        buf[0:pad, :] = zeros
        buf[seq + pad:seq + 2 * pad, :] = zeros

    def fill(t, carry):
        r0 = pl.multiple_of(t * rt, rt)
        a0_ref[pl.ds(r0 + pad, rt), :] = a_ref[pl.ds(r0, rt), :].astype(F32)
        return carry

    lax.fori_loop(0, seq // rt, fill, 0)

    def neighbours(t, carry):
        r0 = pl.multiple_of(t * rt, rt)
        left = _shifted_rows(a0_ref, r0 + pad, -1, rt, slice(None))
        right = _shifted_rows(a0_ref, r0 + pad, 1, rt, slice(None))
        if grid2d:
            colpos = (_iota((rt, ct), 0) + r0) & (GRID_W - 1)
            left = jnp.where(colpos == 0, 0.0, left)
            right = jnp.where(colpos == GRID_W - 1, 0.0, right)
        al_ref[pl.ds(r0 + pad, rt), :] = left
        ar_ref[pl.ds(r0 + pad, rt), :] = right
        return carry

    lax.fori_loop(0, seq // rt, neighbours, 0)

    if grid2d:
        taps = [(dr, dc) for dr in (-1, 0, 1) for dc in (-1, 0, 1)]
    else:
        taps = [(0, dc) for dc in (-1, 0, 1)]
    src = {-1: al_ref, 0: a0_ref, 1: ar_ref}

    def tile(t, carry):
        r0 = pl.multiple_of(t * rt, rt)
        acc = jnp.zeros((rt, ct), F32) + b_ref[...]
        for dr, dc in taps:
            widx = (dr + 1) * 3 + (dc + 1)
            rows = pl.ds(pl.multiple_of(r0 + pad + dr * GRID_W, SUBLANE), rt)
            acc = acc + w_ref[widx:widx + 1, :] * src[dc][rows, :]
        o_ref[pl.ds(r0, rt), :] = (_silu(acc) * v_ref[pl.ds(r0, rt), :].astype(F32)).astype(o_ref.dtype)
        return carry

    lax.fori_loop(0, seq // rt, tile, 0)


def _ffn_conv(av, w9, bias, *, batch, seq, grid2d):
    t = av.shape[0]
    f = w9.shape[1]
    ct = 256
    assert f % ct == 0 and seq % 128 == 0
    nct = f // ct
    pad = 72 if grid2d else SUBLANE
    return pl.pallas_call(
        functools.partial(_ffn_conv_kernel, seq=seq, grid2d=grid2d),
        grid=(batch, nct),
        in_specs=[pl.BlockSpec((seq, ct), lambda b, c: (b, c)),
                  pl.BlockSpec((seq, ct), lambda b, c: (b, nct + c)),
                  pl.BlockSpec((9, ct), lambda b, c: (0, c)),
                  pl.BlockSpec((1, ct), lambda b, c: (0, c))],
        out_specs=pl.BlockSpec((seq, ct), lambda b, c: (b, c)),
        out_shape=jax.ShapeDtypeStruct((t, f), BF16),
        scratch_shapes=[pltpu.VMEM((seq + 2 * pad, ct), F32)] * 3,
        compiler_params=_cparams(("parallel", "parallel")),
        name="ffn_conv",
    )(av, av, w9, bias.reshape(1, f))


def _resid_matmul_kernel(*refs, final_norm):
    if final_norm:
        a_ref, w_ref, x_ref, gate_ref, fw_ref, o_ref = refs
    else:
        a_ref, w_ref, x_ref, gate_ref, o_ref = refs
    y = x_ref[...] + gate_ref[0] * _dot(a_ref[...], w_ref[...])
    if final_norm:
        y = y * lax.rsqrt(jnp.mean(y * y, axis=-1, keepdims=True) + EPS) * fw_ref[...]
    o_ref[...] = y


def _resid_matmul(a, w, x2, mod3, gate_idx, *, rows_per_batch, fixed_row, final_w=None):
    t, d = x2.shape
    k = a.shape[1]
    tm = min(512, t if fixed_row is not None else rows_per_batch)
    assert t % tm == 0
    row = _row_of_tile(tm, rows_per_batch, fixed_row)
    final_norm = final_w is not None
    in_specs = [pl.BlockSpec((tm, k), lambda i: (i, 0)),
                pl.BlockSpec((k, d), lambda i: (0, 0)),
                pl.BlockSpec((tm, d), lambda i: (i, 0)),
                pl.BlockSpec((None, 1, d), lambda i: (row(i), 0, gate_idx))]
    args = [a, w, x2, mod3]
    if final_norm:
        in_specs.append(pl.BlockSpec((1, d), lambda i: (0, 0)))
        args.append(final_w.reshape(1, d))
    return pl.pallas_call(
        functools.partial(_resid_matmul_kernel, final_norm=final_norm),
        grid=(t // tm,),
        in_specs=in_specs,
        out_specs=pl.BlockSpec((tm, d), lambda i: (i, 0)),
        out_shape=jax.ShapeDtypeStruct((t, d), F32),
        compiler_params=_cparams(("parallel",)),
        name="resid_matmul",
    )(*args)


def _group_major(u2, batch, seq, groups):
    nk = seq // S5_CHUNK
    u5 = u2.reshape(batch, nk, S5_CHUNK, groups, S5_GROUP)
    return jnp.transpose(u5, (3, 1, 0, 2, 4)).reshape(groups, nk * batch, S5_CHUNK * S5_GROUP)


def _token_major(yg, batch, seq, groups):
    nk = seq // S5_CHUNK
    y5 = yg.reshape(groups, nk, batch, S5_CHUNK, S5_GROUP)
    return jnp.transpose(y5, (2, 1, 3, 0, 4)).reshape(batch * seq, groups * S5_GROUP)


def kernel(x, c, ctx, c_ctx, ada_w, ada_b, norm1_w, norm2_w, w_in, s5_a_re, s5_a_im, s5_log_dt, s5_b_re, s5_b_im, s5_c_re, s5_c_im, s5_d, s5_w_glu, dn_conv_w, dn_a_log, dn_dt_bias, dn_norm_w, cv_dw_w, cv_dw_b, cv_ln_w, cv_ln_b, w_br_s5, w_br_dn, w_br_cv, w_out, ffn_w_up, ffn_dw_w, ffn_dw_b, ffn_w_down, final_norm_w):
    batch, seq, d = x.shape
    ctx_len = ctx.shape[1]
    depth = ada_w.shape[0]
    groups = s5_a_re.shape[2]
    n_heads = dn_a_log.shape[2]
    dn_width = n_heads * DN_HEAD_DIM
    ffn_hidden = ffn_dw_b.shape[1]
    assert batch == SUBLANE and batch < MOD_ROWS and d == groups * S5_GROUP == dn_width
    assert d % 1024 == 0 and 4 * n_heads <= LANE

    col_qkv = d
    col_beta = col_qkv + 3 * dn_width
    n_state = col_beta + 4 * n_heads
    col_cv = n_state + dn_width
    col_gate = col_cv + 2 * d
    pk_q = d // LANE
    pk_z = (d + 3 * dn_width) // LANE
    pk_cv = (d + 4 * dn_width) // d
    pk_gate = pk_cv + 2

    cmat = jnp.zeros((MOD_ROWS, d), F32).at[:batch].set(c).at[batch].set(c_ctx)
    xl = x.reshape(batch * seq, d)
    xc = ctx.reshape(batch * ctx_len, d)

    for i in range(depth):
        last = i == depth - 1
        w_main = jnp.concatenate([w_in[i][:, :col_beta], w_in[i][:, n_state:]], axis=1).astype(BF16)
        w_bd = jnp.pad(w_in[i][:, col_beta:n_state], ((0, 0), (0, LANE - 4 * n_heads)))
        aneg = jnp.pad(-jnp.exp(dn_a_log[i].reshape(1, -1)), ((0, 0), (2 * n_heads, LANE - 4 * n_heads)))
        dtb = jnp.pad(dn_dt_bias[i].reshape(1, -1), ((0, 0), (2 * n_heads, LANE - 4 * n_heads)))
        s5_par = _s5_params(s5_a_re[i], s5_a_im[i], s5_log_dt[i], s5_b_re[i], s5_b_im[i],
                            s5_c_re[i], s5_c_im[i])
        w_glu = s5_w_glu[i].astype(BF16)
        w_brs = w_br_s5[i].astype(BF16)
        w_brd = w_br_dn[i].astype(BF16)
        w_brc = w_br_cv[i].astype(BF16)
        w_o = w_out[i].astype(BF16)
        w_up = ffn_w_up[i].astype(BF16)
        w_dn = ffn_w_down[i].astype(BF16)
        w9 = ffn_dw_w[i].reshape(9, ffn_hidden)

        mod3 = _modulation(cmat, ada_w[i], ada_b[i]).reshape(MOD_ROWS, 1, N_MOD * d)

        rows_c = dict(rows_per_batch=ctx_len, fixed_row=batch)
        rows_l = dict(rows_per_batch=seq, fixed_row=None)
        proj_c, bd_c = _norm_matmul(xc, norm1_w[i], mod3, 0, 1, w_main, wbd=w_bd, tn=2 * d,
                                    **rows_c)
        proj_l, bd_l = _norm_matmul(xl, norm1_w[i], mod3, 0, 1, w_main, wbd=w_bd, tn=2 * d,
                                    **rows_l)

        u_rows = jnp.concatenate([_group_major(proj_c[:, :d], batch, ctx_len, groups),
                                  _group_major(proj_l[:, :d], batch, seq, groups)], axis=1)
        rows_ctx = (ctx_len // S5_CHUNK) * batch
        segments = ((0, ctx_len // S5_CHUNK), (rows_ctx, seq // S5_CHUNK))
        y_rows = _s5_core(u_rows, *s5_par, segments=segments, batch=batch)

        zero_dn = jnp.zeros((batch, 2, n_heads, DN_HEAD_DIM, DN_HEAD_DIM), F32)
        dn_args = dict(batch=batch, n_heads=n_heads, col_q=pk_q, col_z=pk_z)
        dn_c, dn_state = _deltanet(proj_c, _dn_gates(bd_c, aneg, dtb, n_heads), dn_conv_w[i],
                                   dn_norm_w[i], zero_dn, seq=ctx_len, **dn_args)
        dn_l, _ = _deltanet(proj_l, _dn_gates(bd_l, aneg, dtb, n_heads), dn_conv_w[i],
                            dn_norm_w[i], dn_state, seq=seq, **dn_args)

        def finish(x2, proj, y, dn_out, length, rows, grid2d, final_w):
            s5_out = _s5_post(y, proj, s5_d[i], w_glu)
            cv_out = _conv_module(proj, cv_dw_w[i], cv_dw_b[i], cv_ln_w[i], cv_ln_b[i],
                                  batch=batch, seq=length, col_a=pk_cv)
            x2 = _merge(s5_out, dn_out, cv_out, proj, w_brs, w_brd, w_brc, w_o, x2, mod3, 2,
                        col_gate=pk_gate, **rows)
            av = _norm_matmul(x2, norm2_w[i], mod3, 3, 4, w_up, tn=w_up.shape[1] // 2, **rows)
            hidden = _ffn_conv(av, w9, ffn_dw_b[i], batch=batch, seq=length, grid2d=grid2d)
            return _resid_matmul(hidden, w_dn, x2, mod3, 5, final_w=final_w, **rows)

        if not last:
            xc = finish(xc, proj_c, _token_major(y_rows[:, :rows_ctx], batch, ctx_len, groups),
                        dn_c, ctx_len, rows_c, False, None)
        xl = finish(xl, proj_l, _token_major(y_rows[:, rows_ctx:], batch, seq, groups), dn_l, seq,
                    rows_l, True, final_norm_w if last else None)

    return xl.reshape(batch, seq, d)
```

```python
import functools
import math

import jax
import jax.numpy as jnp
from jax import lax
from jax.experimental import pallas as pl
from jax.experimental.pallas import tpu as pltpu

F32 = jnp.float32
BF16 = jnp.bfloat16
HIGHEST = lax.Precision.HIGHEST

EPS = 1e-6
GRID_W = 64
S5_GROUP = 16
DN_HEAD_DIM = 128
N_MOD = 6
N_BRANCH = 3

LANE = 128
SUBLANE = 8
S5_CHUNK = 16
DN_CHUNK = 64
MOD_ROWS = 16
CV_ROW_TILE = 64
CV_COL_TILE = 256
VMEM_LIMIT = 56 * 1024 * 1024


def _cparams(sem, vmem=VMEM_LIMIT):
    return pltpu.CompilerParams(dimension_semantics=sem, vmem_limit_bytes=vmem)


def _dot(a, b, precision=None):
    return jnp.dot(a, b, preferred_element_type=F32, precision=precision)


def _dot_nt(a, b, precision=None):
    return lax.dot_general(a, b, (((1,), (1,)), ((), ())), preferred_element_type=F32,
                           precision=precision)


def _dot_tn(a, b, precision=None):
    return lax.dot_general(a, b, (((0,), (0,)), ((), ())), preferred_element_type=F32,
                           precision=precision)


def _silu(x):
    return x * jax.nn.sigmoid(x)


def _iota(shape, dim):
    return lax.broadcasted_iota(jnp.int32, shape, dim)


def _aligned(index, multiple):
    return index if isinstance(index, int) else pl.multiple_of(index, multiple)


def _shifted_rows(ref, base, off, rows, cols):
    lo = (off // SUBLANE) * SUBLANE
    shift = off - lo
    start = pl.multiple_of(base + lo, SUBLANE)
    if shift == 0:
        return ref[pl.ds(start, rows), cols]
    return ref[pl.ds(start, rows + SUBLANE), cols][shift:shift + rows]


def _mod_kernel(c_ref, w_ref, b_ref, o_ref):
    o_ref[...] = _dot(_silu(c_ref[...]), w_ref[...], HIGHEST) + b_ref[...]


def _modulation(cmat, ada_w, ada_b):
    rows, d = cmat.shape
    n = ada_w.shape[1]
    tn = 1024
    return pl.pallas_call(
        _mod_kernel,
        grid=(n // tn,),
        in_specs=[pl.BlockSpec((rows, d), lambda j: (0, 0)),
                  pl.BlockSpec((d, tn), lambda j: (0, j)),
                  pl.BlockSpec((1, tn), lambda j: (0, j))],
        out_specs=pl.BlockSpec((rows, tn), lambda j: (0, j)),
        out_shape=jax.ShapeDtypeStruct((rows, n), F32),
        compiler_params=_cparams(("parallel",)),
        name="modulation",
    )(cmat, ada_w, ada_b.reshape(1, n))


def _norm_matmul_kernel(*refs, with_bd):
    if with_bd:
        x_ref, nw_ref, sh_ref, sc_ref, w_ref, wbd_ref, o_ref, bd_ref, h_ref = refs
    else:
        x_ref, nw_ref, sh_ref, sc_ref, w_ref, o_ref, h_ref = refs

    @pl.when(pl.program_id(1) == 0)
    def _():
        x = x_ref[...]
        h = x * lax.rsqrt(jnp.mean(x * x, axis=-1, keepdims=True) + EPS) * nw_ref[...]
        h = h * (1.0 + sc_ref[0]) + sh_ref[0]
        h_ref[...] = h.astype(BF16)
        if with_bd:
            bd_ref[...] = _dot(h, wbd_ref[...], HIGHEST)

    o_ref[...] = _dot(h_ref[...], w_ref[...]).astype(o_ref.dtype)


def _row_of_tile(tm, rows_per_batch, fixed_row):
    if fixed_row is not None:
        return lambda i: fixed_row
    return lambda i: (i * tm) // rows_per_batch


def _norm_matmul(x2, norm_w, mod3, sh_idx, sc_idx, w, *, rows_per_batch, fixed_row, wbd=None,
                 tn):
    t, d = x2.shape
    n = w.shape[1]
    tm = min(1024, t if fixed_row is not None else rows_per_batch)
    assert t % tm == 0 and n % tn == 0
    row = _row_of_tile(tm, rows_per_batch, fixed_row)
    with_bd = wbd is not None
    in_specs = [pl.BlockSpec((tm, d), lambda i, j: (i, 0)),
                pl.BlockSpec((1, d), lambda i, j: (0, 0)),
                pl.BlockSpec((None, 1, d), lambda i, j: (row(i), 0, sh_idx)),
                pl.BlockSpec((None, 1, d), lambda i, j: (row(i), 0, sc_idx)),
                pl.BlockSpec((d, tn), lambda i, j: (0, j))]
    args = [x2, norm_w.reshape(1, d), mod3, mod3, w]
    out_specs = [pl.BlockSpec((tm, tn), lambda i, j: (i, j))]
    out_shape = [jax.ShapeDtypeStruct((t, n), BF16)]
    if with_bd:
        in_specs.append(pl.BlockSpec((d, LANE), lambda i, j: (0, 0)))
        args.append(wbd)
        out_specs.append(pl.BlockSpec((tm, LANE), lambda i, j: (i, 0)))
        out_shape.append(jax.ShapeDtypeStruct((t, LANE), F32))
    res = pl.pallas_call(
        functools.partial(_norm_matmul_kernel, with_bd=with_bd),
        grid=(t // tm, n // tn),
        in_specs=in_specs,
        out_specs=out_specs,
        out_shape=out_shape,
        scratch_shapes=[pltpu.VMEM((tm, d), BF16)],
        compiler_params=_cparams(("parallel", "arbitrary")),
        name="norm_matmul",
    )(*args)
    return res if with_bd else res[0]


def _dn_gates_kernel(bd_ref, aneg_ref, dtb_ref, o_ref, *, n_heads):
    x = bd_ref[...]
    tm = x.shape[0]
    y = x + dtb_ref[...]
    g = aneg_ref[...] * (jnp.maximum(y, 0.0) + jnp.log1p(jnp.exp(-jnp.abs(y))))
    blk = 2 * DN_CHUNK
    r = _iota((blk, blk), 0)
    c = _iota((blk, blk), 1)
    shift = DN_CHUNK.bit_length() - 1
    same = (r >> shift) == (c >> shift)
    tri_lo = jnp.where(same & (c <= r), 1.0, 0.0)
    tri_hi = jnp.where(same & (c >= r), 1.0, 0.0)
    parts = [g[t * blk:(t + 1) * blk] for t in range(tm // blk)]
    prefix = jnp.concatenate([_dot(tri_lo, p, HIGHEST) for p in parts], axis=0)
    suffix = jnp.concatenate([_dot(tri_hi, p, HIGHEST) for p in parts], axis=0)
    lane = _iota(x.shape, 1)
    o_ref[...] = jnp.where(lane < 2 * n_heads, jax.nn.sigmoid(x),
                           jnp.where(lane < 3 * n_heads, prefix, suffix))


def _dn_gates(bd, aneg, dtb, n_heads):
    t = bd.shape[0]
    tm = min(512, t)
    assert t % tm == 0 and tm % (2 * DN_CHUNK) == 0
    return pl.pallas_call(
        functools.partial(_dn_gates_kernel, n_heads=n_heads),
        grid=(t // tm,),
        in_specs=[pl.BlockSpec((tm, LANE), lambda i: (i, 0)),
                  pl.BlockSpec((1, LANE), lambda i: (0, 0)),
                  pl.BlockSpec((1, LANE), lambda i: (0, 0))],
        out_specs=pl.BlockSpec((tm, LANE), lambda i: (i, 0)),
        out_shape=jax.ShapeDtypeStruct((t, LANE), F32),
        compiler_params=_cparams(("parallel",)),
        name="dn_gates",
    )(bd, aneg, dtb)


def _s5_kernel(u_ref, bt_ref, c_ref, pw_ref, a_ref, y_ref, s_ref, hp_ref, *, segments, batch):
    tc, cg = S5_CHUNK, S5_GROUP
    half = LANE // 2
    n_grp = u_ref.shape[0]

    def operators(g):
        btr = [bt_ref[g, d, 0] for d in range(2)]
        bti = [bt_ref[g, d, 1] for d in range(2)]
        cr, ci = c_ref[g, 0], c_ref[g, 1]
        pr = [pw_ref[g, d, 0] for d in range(2)]
        pi = [pw_ref[g, d, 1] for d in range(2)]
        l_re, l_im = [], []
        for m in range(2 * tc):
            tau = tc - 1 - m
            d = 0 if tau >= 0 else 1
            if tau == 0:
                l_re.append(btr[0] + btr[1])
                l_im.append(bti[0] + bti[1])
            elif tau == -tc:
                l_re.append(jnp.zeros_like(btr[0]))
                l_im.append(jnp.zeros_like(btr[0]))
            else:
                p_r, p_i = pr[d][abs(tau):abs(tau) + 1, :], pi[d][abs(tau):abs(tau) + 1, :]
                l_re.append(p_r * btr[d] - p_i * bti[d])
                l_im.append(p_r * bti[d] + p_i * btr[d])
        l_re = jnp.concatenate(l_re, axis=0)
        l_im = jnp.concatenate(l_im, axis=0)
        kxt = _dot_nt(cr, l_re, HIGHEST) - _dot_nt(ci, l_im, HIGHEST)
        wt_t = jnp.concatenate([kxt[:, cg * (tc - 1 - j):cg * (tc - 1 - j) + tc * cg]
                                for j in range(tc)], axis=0)
        win_rows, wst_rows = [], []
        for i in range(tc):
            pf_r, pf_i = pr[0][tc - 1 - i:tc - i, :], pi[0][tc - 1 - i:tc - i, :]
            pb_r, pb_i = pr[1][i:i + 1, :], pi[1][i:i + 1, :]
            win_rows.append(jnp.concatenate([
                pf_r * btr[0] - pf_i * bti[0], pb_r * btr[1] - pb_i * bti[1],
                pf_r * bti[0] + pf_i * btr[0], pb_r * bti[1] + pb_i * btr[1]], axis=1))
            qf_r, qf_i = pr[0][i + 1:i + 2, :], pi[0][i + 1:i + 2, :]
            qb_r, qb_i = pr[1][tc - i:tc - i + 1, :], pi[1][tc - i:tc - i + 1, :]
            wst_rows.append(jnp.concatenate([
                cr * qf_r - ci * qf_i, cr * qb_r - ci * qb_i,
                -(cr * qf_i + ci * qf_r), -(cr * qb_i + ci * qb_r)], axis=1))
        win = jnp.concatenate(win_rows, axis=0)
        wst_t = jnp.concatenate(wst_rows, axis=0)
        return wt_t.astype(BF16), win.astype(BF16), wst_t.astype(BF16)

    ops = [operators(g) for g in range(n_grp)]
    for g in range(n_grp):
        s_ref[g] = _dot(u_ref[g], ops[g][1])
    ars = [a_ref[g, 0:1, :] for g in range(n_grp)]
    ais = [a_ref[g, 1:2, :] for g in range(n_grp)]
    is_fwd = _iota((batch, LANE), 1) < half
    carry = tuple(jnp.zeros((batch, LANE), F32) for _ in range(2 * n_grp))
    for row0, n_chunks in segments:
        def body(s, carry, row0=row0, n_chunks=n_chunks):
            rf = pl.ds(pl.multiple_of(row0 + s * batch, batch), batch)
            rb = pl.ds(pl.multiple_of(row0 + (n_chunks - 1 - s) * batch, batch), batch)
            new = []
            for g in range(n_grp):
                hre, him = carry[2 * g], carry[2 * g + 1]
                hp_ref[g, rf, 0:half] = hre[:, 0:half]
                hp_ref[g, rb, half:LANE] = hre[:, half:LANE]
                hp_ref[g, rf, LANE:LANE + half] = him[:, 0:half]
                hp_ref[g, rb, LANE + half:2 * LANE] = him[:, half:LANE]
                sf = s_ref[g, rf, :]
                sb = s_ref[g, rb, :]
                sre = jnp.where(is_fwd, sf[:, 0:LANE], sb[:, 0:LANE])
                sim = jnp.where(is_fwd, sf[:, LANE:2 * LANE], sb[:, LANE:2 * LANE])
                new.append(ars[g] * hre - ais[g] * him + sre)
                new.append(ars[g] * him + ais[g] * hre + sim)
            return tuple(new)

        carry = lax.fori_loop(0, n_chunks, body, carry)
    for g in range(n_grp):
        y = _dot_nt(u_ref[g], ops[g][0]) + _dot_nt(hp_ref[g].astype(BF16), ops[g][2])
        y_ref[g] = y.astype(y_ref.dtype)


S5_GROUPS_PER_STEP = 4


def _s5_core(ug, bt, cc, pw, avec, *, segments, batch):
    g, rows, k = ug.shape
    gb = S5_GROUPS_PER_STEP
    assert batch == SUBLANE and k == 2 * LANE and g % gb == 0
    act = pl.BlockSpec((gb, rows, k), lambda i: (i, 0, 0))
    per_group = lambda a: pl.BlockSpec((gb,) + a.shape[1:],
                                       lambda i: (i,) + (0,) * (a.ndim - 1))
    return pl.pallas_call(
        functools.partial(_s5_kernel, segments=segments, batch=batch),
        grid=(g // gb,),
        in_specs=[act, per_group(bt), per_group(cc), per_group(pw), per_group(avec)],
        out_specs=act,
        out_shape=jax.ShapeDtypeStruct((g, rows, k), BF16),
        scratch_shapes=[pltpu.VMEM((gb, rows, k), F32), pltpu.VMEM((gb, rows, k), F32)],
        compiler_params=_cparams(("parallel",)),
        name="s5_core",
    )(ug, bt, cc, pw, avec)


def _s5_params(a_re, a_im, log_dt, b_re, b_im, c_re, c_im):
    tc = S5_CHUNK
    dt = jnp.exp(log_dt)[..., None]
    mag = jnp.exp(a_re * dt)
    abr, abi = mag * jnp.cos(a_im * dt), mag * jnp.sin(a_im * dt)
    den = a_re * a_re + a_im * a_im
    cr = ((abr - 1.0) * a_re + abi * a_im) / den
    ci = (abi * a_re - (abr - 1.0) * a_im) / den
    bbr = cr[..., None] * b_re - ci[..., None] * b_im
    bbi = cr[..., None] * b_im + ci[..., None] * b_re
    n = jnp.arange(tc + 1, dtype=F32)[:, None, None, None]
    pmag = jnp.exp(a_re * dt * n)
    pr, pi = pmag * jnp.cos(a_im * dt * n), pmag * jnp.sin(a_im * dt * n)
    bt = jnp.stack([jnp.swapaxes(bbr, 2, 3), jnp.swapaxes(bbi, 2, 3)], axis=2)
    bt = jnp.transpose(bt, (1, 0, 2, 3, 4))
    cc = jnp.stack([c_re, c_im], axis=1)
    pw = jnp.transpose(jnp.stack([pr, pi], axis=0), (3, 2, 0, 1, 4))
    pw = jnp.pad(pw, ((0, 0), (0, 0), (0, 0), (0, SUBLANE - 1), (0, 0)))
    avec = jnp.stack([jnp.concatenate([pr[tc, 0], pr[tc, 1]], axis=-1),
                      jnp.concatenate([pi[tc, 0], pi[tc, 1]], axis=-1)], axis=1)
    return bt, cc, pw, avec


def _s5_post_kernel(y_ref, u_ref, d_ref, w_ref, o_ref):
    y = y_ref[...].astype(F32) + d_ref[...] * u_ref[...].astype(F32)
    k0 = math.sqrt(2.0 / math.pi)
    g = 0.5 * y * (1.0 + jnp.tanh(k0 * (y + 0.044715 * (y * y * y))))
    o_ref[...] = (g * jax.nn.sigmoid(_dot(g.astype(BF16), w_ref[...]))).astype(o_ref.dtype)


def _s5_post(y, proj, s5_d, w_glu):
    t, d = y.shape
    tm = min(512, t)
    return pl.pallas_call(
        _s5_post_kernel,
        grid=(t // tm,),
        in_specs=[pl.BlockSpec((tm, d), lambda i: (i, 0)),
                  pl.BlockSpec((tm, d), lambda i: (i, 0)),
                  pl.BlockSpec((1, d), lambda i: (0, 0)),
                  pl.BlockSpec((d, d), lambda i: (0, 0))],
        out_specs=pl.BlockSpec((tm, d), lambda i: (i, 0)),
        out_shape=jax.ShapeDtypeStruct((t, d), BF16),
        compiler_params=_cparams(("parallel",)),
        name="s5_post",
    )(y, proj, s5_d.reshape(1, d), w_glu)


def _unit_tri_inverses(ms, r, c):
    mm = lambda a, b: _dot(a.astype(BF16), b.astype(BF16))
    eye = jnp.where(r == c, 1.0, 0.0)
    blk = lambda s: (r >> s) == (c >> s)
    mds = [jnp.where(blk(3), m, 0.0) for m in ms]
    m2s = [mm(md, md) for md in mds]
    m4s = [mm(m2, m2) for m2 in m2s]
    ts = [eye - md for md in mds]
    ts = [t + mm(t, m2) for t, m2 in zip(ts, m2s)]
    ts = [t + mm(t, m4) for t, m4 in zip(ts, m4s)]
    for s in (3, 4, 5):
        off = blk(s + 1) & jnp.logical_not(blk(s))
        xs = [mm(jnp.where(off, m, 0.0), t) for m, t in zip(ms, ts)]
        ts = [t - mm(t, x) for t, x in zip(ts, xs)]
    return ts


def _dn_kernel(q_ref, k_ref, v_ref, z_ref, col_ref, cwq_ref, cwk_ref, cwv_ref, nw_ref, s0_ref,
               o_ref, sfin_ref, xp_ref, qs_ref, ks_ref, vs_ref, u_ref, w_ref, a_ref, qg_ref,
               kt_ref, el_ref, od_ref, s_ref, *, seq, n_heads, hp, ca):
    c_sz = DN_CHUNK
    hd = DN_HEAD_DIM
    pad = SUBLANE
    rt = min(256, seq)
    n_chunks = seq // c_sz
    head0 = pl.program_id(1) * hp

    def prep(x_ref, cw_ref, dst_ref, normalise, scale):
        zeros = jnp.zeros((pad, hp * hd), F32)
        xp_ref[0:pad, :] = zeros
        xp_ref[seq + pad:seq + 2 * pad, :] = zeros
        xp_ref[pad:seq + pad, :] = x_ref[...].astype(F32)
        w = cw_ref[...]
        for t in range(seq // rt):
            r0 = t * rt
            acc = (w[0:1, :] * xp_ref[r0 + pad - 1:r0 + pad - 1 + rt, :]
                   + w[1:2, :] * xp_ref[r0 + pad:r0 + pad + rt, :]
                   + w[2:3, :] * xp_ref[r0 + pad + 1:r0 + pad + 1 + rt, :])
            y = _silu(acc)
            if normalise:
                parts = []
                for hl in range(hp):
                    yh = y[:, hl * hd:(hl + 1) * hd]
                    parts.append(yh * (lax.rsqrt(jnp.sum(yh * yh, axis=-1, keepdims=True) + EPS)
                                       * scale))
                y = jnp.concatenate(parts, axis=1)
            dst_ref[r0:r0 + rt, :] = y

    prep(q_ref, cwq_ref, qs_ref, True, hd ** -0.5)
    prep(k_ref, cwk_ref, ks_ref, True, 1.0)
    prep(v_ref, cwv_ref, vs_ref, False, 1.0)

    r = _iota((c_sz, c_sz), 0)
    c = _iota((c_sz, c_sz), 1)
    lane = _iota((c_sz, LANE), 1)

    def pick(col, idx):
        v = jnp.sum(jnp.where(lane == idx, col, 0.0), axis=-1, keepdims=True)
        return jnp.broadcast_to(v, (c_sz, LANE))

    def phase_a(it):
        chains = []
        for cc in range(ca):
            for direction in range(2):
                step = it * ca + cc
                chunk = step if direction == 0 else n_chunks - 1 - step
                rows = pl.ds(_aligned(chunk * c_sz, c_sz), c_sz)
                col = col_ref[rows, :]
                for hl in range(hp):
                    hs = slice(hl * hd, (hl + 1) * hd)
                    chains.append((chunk, rows, hl, direction, ks_ref[rows, hs], qs_ref[rows, hs],
                                   vs_ref[rows, hs],
                                   pick(col, head0 + hl + direction * n_heads),
                                   pick(col, head0 + hl + (2 + direction) * n_heads)))
        k16s = [ch[4].astype(BF16) for ch in chains]
        kkts = [_dot_nt(k16, k16) for k16 in k16s]
        qkts = [_dot_nt(ch[5].astype(BF16), k16) for ch, k16 in zip(chains, k16s)]
        decays, ms = [], []
        for (chunk, rows, hl, direction, kc, qc, vc, bc, gc), kkt in zip(chains, kkts):
            lower = direction == 0
            grow = jnp.concatenate([gc, gc], axis=0).T[0:c_sz, 0:c_sz]
            incl = (r >= c) if lower else (r <= c)
            strict = (r > c) if lower else (r < c)
            decay = jnp.where(incl, jnp.exp(jnp.minimum(gc[:, 0:c_sz] - grow, 0.0)), 0.0)
            decays.append(decay)
            ms.append(jnp.where(strict, bc[:, 0:c_sz] * kkt * decay, 0.0))
        tinvs = _unit_tri_inverses(ms, r, c)
        egs = [jnp.exp(ch[8]) for ch in chains]
        uws = []
        for (chunk, rows, hl, direction, kc, qc, vc, bc, gc), tinv, eg in zip(chains, tinvs, egs):
            rhs = jnp.concatenate([vc * bc, kc * bc * eg], axis=1).astype(BF16)
            uws.append(_dot(tinv.astype(BF16), rhs))
        for (chunk, rows, hl, direction, kc, qc, vc, bc, gc), qkt, decay, eg, uw in zip(
                chains, qkts, decays, egs, uws):
            idx = hl * 2 + direction
            g_last = gc[c_sz - 1:c_sz, :] if direction == 0 else gc[0:1, :]
            u_ref[idx, rows, :] = uw[:, 0:hd]
            w_ref[idx, rows, :] = uw[:, hd:2 * hd].astype(BF16)
            a_ref[idx, rows, :] = (qkt * decay).astype(BF16)
            qg_ref[idx, rows, :] = (qc * eg).astype(BF16)
            kt_ref[idx, rows, :] = (kc * jnp.exp(g_last - gc)).astype(BF16)
            el_ref[idx, pl.ds(_aligned(chunk * SUBLANE, SUBLANE), SUBLANE), :] = (
                jnp.broadcast_to(jnp.exp(g_last), (SUBLANE, LANE)))

    for hl in range(hp):
        for direction in range(2):
            s_ref[hl * 2 + direction] = s0_ref[direction, hl]

    def phase_b(step):
        ids, rows, erows = [], [], []
        for hl in range(hp):
            for direction in range(2):
                chunk = step if direction == 0 else n_chunks - 1 - step
                ids.append(hl * 2 + direction)
                rows.append(pl.ds(_aligned(chunk * c_sz, c_sz), c_sz))
                erows.append(pl.ds(_aligned(chunk * SUBLANE, SUBLANE), SUBLANE))
        ss = [s_ref[i] for i in ids]
        s16s = [s.astype(BF16) for s in ss]
        wss = [_dot(w_ref[i, rw, :], s16) for i, rw, s16 in zip(ids, rows, s16s)]
        qss = [_dot(qg_ref[i, rw, :], s16) for i, rw, s16 in zip(ids, rows, s16s)]
        vns = [(u_ref[i, rw, :] - ws).astype(BF16) for i, rw, ws in zip(ids, rows, wss)]
        avs = [_dot(a_ref[i, rw, :], vn) for i, rw, vn in zip(ids, rows, vns)]
        kvs = [_dot_tn(kt_ref[i, rw, :], vn) for i, rw, vn in zip(ids, rows, vns)]
        for i, rw, er, s, qs_, av, kv in zip(ids, rows, erows, ss, qss, avs, kvs):
            od_ref[i, rw, :] = qs_ + av
            s_ref[i] = s * el_ref[i, er, :][0:1, :] + kv

    n_blocks = n_chunks // ca
    phase_a(0)

    def pipelined(it, carry):
        for s in range(ca):
            phase_b((it - 1) * ca + s)
        phase_a(it)
        return carry

    lax.fori_loop(1, n_blocks, pipelined, 0)
    for s in range(ca):
        phase_b((n_blocks - 1) * ca + s)

    for hl in range(hp):
        for direction in range(2):
            sfin_ref[direction, hl] = s_ref[hl * 2 + direction]

    for t in range(seq // rt):
        r0 = t * rt
        for hl in range(hp):
            hs = slice(hl * hd, (hl + 1) * hd)
            o = od_ref[hl * 2, r0:r0 + rt, :] + od_ref[hl * 2 + 1, r0:r0 + rt, :]
            o = o * lax.rsqrt(jnp.mean(o * o, axis=-1, keepdims=True) + EPS) * nw_ref[...]
            o_ref[r0:r0 + rt, hs] = (o * _silu(z_ref[r0:r0 + rt, hs].astype(F32))).astype(o_ref.dtype)


def _deltanet(proj, col, conv_w, norm_w, s0, *, batch, seq, n_heads, col_q, col_z):
    t = proj.shape[0]
    hd = DN_HEAD_DIM
    hp = 2
    d = n_heads * hd
    wd = hp * hd
    n_chunks = seq // DN_CHUNK
    ca = min(8, n_chunks)
    assert n_heads % hp == 0 and col_q % hp == 0 and col_z % hp == 0 and n_chunks % ca == 0
    blk = lambda off: pl.BlockSpec((seq, wd), lambda b, h: (b, off // hp + h))
    cw = lambda off: pl.BlockSpec((3, wd), lambda b, h: (0, off // hp + h))
    st = pl.BlockSpec((None, 2, hp, hd, hd), lambda b, h: (b, 0, h, 0, 0))
    nst = 2 * hp
    return pl.pallas_call(
        functools.partial(_dn_kernel, seq=seq, n_heads=n_heads, hp=hp, ca=ca),
        grid=(batch, n_heads // hp),
        in_specs=[blk(col_q), blk(col_q + n_heads), blk(col_q + 2 * n_heads), blk(col_z),
                  pl.BlockSpec((seq, LANE), lambda b, h: (b, 0)),
                  cw(0), cw(n_heads), cw(2 * n_heads),
                  pl.BlockSpec((1, hd), lambda b, h: (0, 0)),
                  st],
        out_specs=[pl.BlockSpec((seq, wd), lambda b, h: (b, h)), st],
        out_shape=[jax.ShapeDtypeStruct((t, d), BF16),
                   jax.ShapeDtypeStruct(s0.shape, F32)],
        scratch_shapes=[pltpu.VMEM((seq + 2 * SUBLANE, wd), F32),
                        pltpu.VMEM((seq, wd), F32), pltpu.VMEM((seq, wd), F32),
                        pltpu.VMEM((seq, wd), F32),
                        pltpu.VMEM((nst, seq, hd), F32),
                        pltpu.VMEM((nst, seq, hd), BF16),
                        pltpu.VMEM((nst, seq, DN_CHUNK), BF16),
                        pltpu.VMEM((nst, seq, hd), BF16),
                        pltpu.VMEM((nst, seq, hd), BF16),
                        pltpu.VMEM((nst, n_chunks * SUBLANE, LANE), F32),
                        pltpu.VMEM((nst, seq, hd), F32),
                        pltpu.VMEM((nst, hd, hd), F32)],
        compiler_params=_cparams(("parallel", "parallel")),
        name="deltanet",
    )(proj, proj, proj, proj, col, conv_w, conv_w, conv_w, norm_w.reshape(1, hd), s0)


def _conv_module_kernel(a_ref, g_ref, w_ref, b_ref, lnw_ref, lnb_ref, o_ref, yp_ref, tmp_ref,
                        sh_ref, *, seq, taps):
    d = a_ref.shape[1]
    half = taps // 2
    pad = 2 * SUBLANE
    assert half < pad
    rt = CV_ROW_TILE
    ct = CV_COL_TILE
    zeros = jnp.zeros((pad, d), F32)
    yp_ref[0:pad, :] = zeros
    yp_ref[seq + pad:seq + 2 * pad, :] = zeros

    def fill(t, carry):
        r0 = pl.multiple_of(t * rt, rt)
        a = a_ref[pl.ds(r0, rt), :].astype(F32)
        g = g_ref[pl.ds(r0, rt), :].astype(F32)
        yp_ref[pl.ds(r0 + pad, rt), :] = a * jax.nn.sigmoid(g)
        return carry

    lax.fori_loop(0, seq // rt, fill, 0)

    def tile(t, carry):
        r0 = pl.multiple_of(t * rt, rt)
        for cc in range(d // ct):
            cs = slice(cc * ct, (cc + 1) * ct)
            win = yp_ref[pl.ds(r0, rt + 2 * pad), cs]
            span = rt + 2 * pad - SUBLANE
            for m in range(1, SUBLANE):
                sh_ref[m - 1] = win[m:m + span]
            acc = jnp.zeros((rt, ct), F32) + b_ref[:, cs]
            for j in range(taps):
                off = pad - half + j
                lo = (off // SUBLANE) * SUBLANE
                m = off - lo
                if m == 0:
                    rows = yp_ref[pl.ds(pl.multiple_of(r0 + lo, SUBLANE), rt), cs]
                else:
                    rows = sh_ref[m - 1, lo:lo + rt, :]
                acc = acc + w_ref[j:j + 1, cs] * rows
            tmp_ref[:, cs] = acc
        y = tmp_ref[...]
        mu = jnp.mean(y, axis=-1, keepdims=True)
        yc = y - mu
        var = jnp.mean(yc * yc, axis=-1, keepdims=True)
        y = yc * lax.rsqrt(var + EPS) * lnw_ref[...] + lnb_ref[...]
        o_ref[pl.ds(r0, rt), :] = _silu(y).astype(o_ref.dtype)
        return carry

    lax.fori_loop(0, seq // rt, tile, 0)


def _conv_module(proj, w, b, lnw, lnb, *, batch, seq, col_a):
    t = proj.shape[0]
    taps, d = w.shape
    return pl.pallas_call(
        functools.partial(_conv_module_kernel, seq=seq, taps=taps),
        grid=(batch,),
        in_specs=[pl.BlockSpec((seq, d), lambda i: (i, col_a)),
                  pl.BlockSpec((seq, d), lambda i: (i, col_a + 1)),
                  pl.BlockSpec((taps, d), lambda i: (0, 0)),
                  pl.BlockSpec((1, d), lambda i: (0, 0)),
                  pl.BlockSpec((1, d), lambda i: (0, 0)),
                  pl.BlockSpec((1, d), lambda i: (0, 0))],
        out_specs=pl.BlockSpec((seq, d), lambda i: (i, 0)),
        out_shape=jax.ShapeDtypeStruct((t, d), BF16),
        scratch_shapes=[pltpu.VMEM((seq + 4 * SUBLANE, d), F32),
                        pltpu.VMEM((CV_ROW_TILE, d), F32),
                        pltpu.VMEM((SUBLANE - 1, CV_ROW_TILE + 3 * SUBLANE, CV_COL_TILE), F32)],
        compiler_params=_cparams(("parallel",)),
        name="conv_module",
    )(proj, proj, w, b.reshape(1, d), lnw.reshape(1, d), lnb.reshape(1, d))


def _merge_kernel(s5_ref, dn_ref, cv_ref, g0_ref, g1_ref, g2_ref, w0_ref, w1_ref, w2_ref,
                  wo_ref, x_ref, gate_ref, o_ref):
    merged = (jax.nn.sigmoid(g0_ref[...].astype(F32)) * _dot(s5_ref[...], w0_ref[...])
              + jax.nn.sigmoid(g1_ref[...].astype(F32)) * _dot(dn_ref[...], w1_ref[...])
              + jax.nn.sigmoid(g2_ref[...].astype(F32)) * _dot(cv_ref[...], w2_ref[...]))
    y = _dot(merged.astype(BF16), wo_ref[...])
    o_ref[...] = x_ref[...] + gate_ref[0] * y


def _merge(s5_out, dn_out, cv_out, proj, w_s5, w_dn, w_cv, w_out, x2, mod3, gate_idx,
           *, rows_per_batch, fixed_row, col_gate):
    t, d = x2.shape
    tm = min(512, t if fixed_row is not None else rows_per_batch)
    assert t % tm == 0
    row = _row_of_tile(tm, rows_per_batch, fixed_row)
    act = pl.BlockSpec((tm, d), lambda i: (i, 0))
    gat = lambda k: pl.BlockSpec((tm, d), lambda i: (i, col_gate + k))
    wsp = pl.BlockSpec((d, d), lambda i: (0, 0))
    return pl.pallas_call(
        _merge_kernel,
        grid=(t // tm,),
        in_specs=[act, act, act, gat(0), gat(1), gat(2), wsp, wsp, wsp, wsp, act,
                  pl.BlockSpec((None, 1, d), lambda i: (row(i), 0, gate_idx))],
        out_specs=act,
        out_shape=jax.ShapeDtypeStruct((t, d), F32),
        compiler_params=_cparams(("parallel",)),
        name="merge",
    )(s5_out, dn_out, cv_out, proj, proj, proj, w_s5, w_dn, w_cv, w_out, x2, mod3)


def _ffn_conv_kernel(a_ref, v_ref, w_ref, b_ref, o_ref, a0_ref, al_ref, ar_ref, *, seq, grid2d):
    ct = a_ref.shape[1]
    pad = 72 if grid2d else SUBLANE
    rt = 128
    zeros = jnp.zeros((pad, ct), F32)
    for buf in (a0_ref, al_ref, ar_ref):
        buf[0:pad, :] = zeros
        buf[seq + pad:seq + 2 * pad, :] = zeros

    def fill(t, carry):
        r0 = pl.multiple_of(t * rt, rt)
        a0_ref[pl.ds(r0 + pad, rt), :] = a_ref[pl.ds(r0, rt), :].astype(F32)
        return carry

    lax.fori_loop(0, seq // rt, fill, 0)

    def neighbours(t, carry):
        r0 = pl.multiple_of(t * rt, rt)
        left = _shifted_rows(a0_ref, r0 + pad, -1, rt, slice(None))
        right = _shifted_rows(a0_ref, r0 + pad, 1, rt, slice(None))
        if grid2d:
            colpos = (_iota((rt, ct), 0) + r0) & (GRID_W - 1)
            left = jnp.where(colpos == 0, 0.0, left)
            right = jnp.where(colpos == GRID_W - 1, 0.0, right)
        al_ref[pl.ds(r0 + pad, rt), :] = left
        ar_ref[pl.ds(r0 + pad, rt), :] = right
        return carry

    lax.fori_loop(0, seq // rt, neighbours, 0)

    if grid2d:
        taps = [(dr, dc) for dr in (-1, 0, 1) for dc in (-1, 0, 1)]
    else:
        taps = [(0, dc) for dc in (-1, 0, 1)]
    src = {-1: al_ref, 0: a0_ref, 1: ar_ref}

    def tile(t, carry):
        r0 = pl.multiple_of(t * rt, rt)
        acc = jnp.zeros((rt, ct), F32) + b_ref[...]
        for dr, dc in taps:
            widx = (dr + 1) * 3 + (dc + 1)
            rows = pl.ds(pl.multiple_of(r0 + pad + dr * GRID_W, SUBLANE), rt)
            acc = acc + w_ref[widx:widx + 1, :] * src[dc][rows, :]
        o_ref[pl.ds(r0, rt), :] = (_silu(acc) * v_ref[pl.ds(r0, rt), :].astype(F32)).astype(o_ref.dtype)
        return carry

    lax.fori_loop(0, seq // rt, tile, 0)


def _ffn_conv(av, w9, bias, *, batch, seq, grid2d):
    t = av.shape[0]
    f = w9.shape[1]
    ct = 256
    assert f % ct == 0 and seq % 128 == 0
    nct = f // ct
    pad = 72 if grid2d else SUBLANE
    return pl.pallas_call(
        functools.partial(_ffn_conv_kernel, seq=seq, grid2d=grid2d),
        grid=(batch, nct),
        in_specs=[pl.BlockSpec((seq, ct), lambda b, c: (b, c)),
                  pl.BlockSpec((seq, ct), lambda b, c: (b, nct + c)),
                  pl.BlockSpec((9, ct), lambda b, c: (0, c)),
                  pl.BlockSpec((1, ct), lambda b, c: (0, c))],
        out_specs=pl.BlockSpec((seq, ct), lambda b, c: (b, c)),
        out_shape=jax.ShapeDtypeStruct((t, f), BF16),
        scratch_shapes=[pltpu.VMEM((seq + 2 * pad, ct), F32)] * 3,
        compiler_params=_cparams(("parallel", "parallel")),
        name="ffn_conv",
    )(av, av, w9, bias.reshape(1, f))


def _resid_matmul_kernel(*refs, final_norm):
    if final_norm:
        a_ref, w_ref, x_ref, gate_ref, fw_ref, o_ref = refs
    else:
        a_ref, w_ref, x_ref, gate_ref, o_ref = refs
    y = x_ref[...] + gate_ref[0] * _dot(a_ref[...], w_ref[...])
    if final_norm:
        y = y * lax.rsqrt(jnp.mean(y * y, axis=-1, keepdims=True) + EPS) * fw_ref[...]
    o_ref[...] = y


def _resid_matmul(a, w, x2, mod3, gate_idx, *, rows_per_batch, fixed_row, final_w=None):
    t, d = x2.shape
    k = a.shape[1]
    tm = min(512, t if fixed_row is not None else rows_per_batch)
    assert t % tm == 0
    row = _row_of_tile(tm, rows_per_batch, fixed_row)
    final_norm = final_w is not None
    in_specs = [pl.BlockSpec((tm, k), lambda i: (i, 0)),
                pl.BlockSpec((k, d), lambda i: (0, 0)),
                pl.BlockSpec((tm, d), lambda i: (i, 0)),
                pl.BlockSpec((None, 1, d), lambda i: (row(i), 0, gate_idx))]
    args = [a, w, x2, mod3]
    if final_norm:
        in_specs.append(pl.BlockSpec((1, d), lambda i: (0, 0)))
        args.append(final_w.reshape(1, d))
    return pl.pallas_call(
        functools.partial(_resid_matmul_kernel, final_norm=final_norm),
        grid=(t // tm,),
        in_specs=in_specs,
        out_specs=pl.BlockSpec((tm, d), lambda i: (i, 0)),
        out_shape=jax.ShapeDtypeStruct((t, d), F32),
        compiler_params=_cparams(("parallel",)),
        name="resid_matmul",
    )(*args)


def _group_major(u2, batch, seq, groups):
    nk = seq // S5_CHUNK
    u5 = u2.reshape(batch, nk, S5_CHUNK, groups, S5_GROUP)
    return jnp.transpose(u5, (3, 1, 0, 2, 4)).reshape(groups, nk * batch, S5_CHUNK * S5_GROUP)


def _token_major(yg, batch, seq, groups):
    nk = seq // S5_CHUNK
    y5 = yg.reshape(groups, nk, batch, S5_CHUNK, S5_GROUP)
    return jnp.transpose(y5, (2, 1, 3, 0, 4)).reshape(batch * seq, groups * S5_GROUP)


def kernel(x, c, ctx, c_ctx, ada_w, ada_b, norm1_w, norm2_w, w_in, s5_a_re, s5_a_im, s5_log_dt, s5_b_re, s5_b_im, s5_c_re, s5_c_im, s5_d, s5_w_glu, dn_conv_w, dn_a_log, dn_dt_bias, dn_norm_w, cv_dw_w, cv_dw_b, cv_ln_w, cv_ln_b, w_br_s5, w_br_dn, w_br_cv, w_out, ffn_w_up, ffn_dw_w, ffn_dw_b, ffn_w_down, final_norm_w):
    batch, seq, d = x.shape
    ctx_len = ctx.shape[1]
    depth = ada_w.shape[0]
    groups = s5_a_re.shape[2]
    n_heads = dn_a_log.shape[2]
    dn_width = n_heads * DN_HEAD_DIM
    ffn_hidden = ffn_dw_b.shape[1]
    assert batch == SUBLANE and batch < MOD_ROWS and d == groups * S5_GROUP == dn_width
    assert d % 1024 == 0 and 4 * n_heads <= LANE

    col_qkv = d
    col_beta = col_qkv + 3 * dn_width
    n_state = col_beta + 4 * n_heads
    col_cv = n_state + dn_width
    col_gate = col_cv + 2 * d
    pk_q = d // LANE
    pk_z = (d + 3 * dn_width) // LANE
    pk_cv = (d + 4 * dn_width) // d
    pk_gate = pk_cv + 2

    cmat = jnp.zeros((MOD_ROWS, d), F32).at[:batch].set(c).at[batch].set(c_ctx)
    xl = x.reshape(batch * seq, d)
    xc = ctx.reshape(batch * ctx_len, d)

    for i in range(depth):
        last = i == depth - 1
        w_main = jnp.concatenate([w_in[i][:, :col_beta], w_in[i][:, n_state:]], axis=1).astype(BF16)
        w_bd = jnp.pad(w_in[i][:, col_beta:n_state], ((0, 0), (0, LANE - 4 * n_heads)))
        aneg = jnp.pad(-jnp.exp(dn_a_log[i].reshape(1, -1)), ((0, 0), (2 * n_heads, LANE - 4 * n_heads)))
        dtb = jnp.pad(dn_dt_bias[i].reshape(1, -1), ((0, 0), (2 * n_heads, LANE - 4 * n_heads)))
        s5_par = _s5_params(s5_a_re[i], s5_a_im[i], s5_log_dt[i], s5_b_re[i], s5_b_im[i],
                            s5_c_re[i], s5_c_im[i])
        w_glu = s5_w_glu[i].astype(BF16)
        w_brs = w_br_s5[i].astype(BF16)
        w_brd = w_br_dn[i].astype(BF16)
        w_brc = w_br_cv[i].astype(BF16)
        w_o = w_out[i].astype(BF16)
        w_up = ffn_w_up[i].astype(BF16)
        w_dn = ffn_w_down[i].astype(BF16)
        w9 = ffn_dw_w[i].reshape(9, ffn_hidden)

        mod3 = _modulation(cmat, ada_w[i], ada_b[i]).reshape(MOD_ROWS, 1, N_MOD * d)

        rows_c = dict(rows_per_batch=ctx_len, fixed_row=batch)
        rows_l = dict(rows_per_batch=seq, fixed_row=None)
        proj_c, bd_c = _norm_matmul(xc, norm1_w[i], mod3, 0, 1, w_main, wbd=w_bd, tn=2 * d,
                                    **rows_c)
        proj_l, bd_l = _norm_matmul(xl, norm1_w[i], mod3, 0, 1, w_main, wbd=w_bd, tn=2 * d,
                                    **rows_l)

        u_rows = jnp.concatenate([_group_major(proj_c[:, :d], batch, ctx_len, groups),
                                  _group_major(proj_l[:, :d], batch, seq, groups)], axis=1)
        rows_ctx = (ctx_len // S5_CHUNK) * batch
        segments = ((0, ctx_len // S5_CHUNK), (rows_ctx, seq // S5_CHUNK))
        y_rows = _s5_core(u_rows, *s5_par, segments=segments, batch=batch)

        zero_dn = jnp.zeros((batch, 2, n_heads, DN_HEAD_DIM, DN_HEAD_DIM), F32)
        dn_args = dict(batch=batch, n_heads=n_heads, col_q=pk_q, col_z=pk_z)
        dn_c, dn_state = _deltanet(proj_c, _dn_gates(bd_c, aneg, dtb, n_heads), dn_conv_w[i],
                                   dn_norm_w[i], zero_dn, seq=ctx_len, **dn_args)
        dn_l, _ = _deltanet(proj_l, _dn_gates(bd_l, aneg, dtb, n_heads), dn_conv_w[i],
                            dn_norm_w[i], dn_state, seq=seq, **dn_args)

        def finish(x2, proj, y, dn_out, length, rows, grid2d, final_w):
            s5_out = _s5_post(y, proj, s5_d[i], w_glu)
            cv_out = _conv_module(proj, cv_dw_w[i], cv_dw_b[i], cv_ln_w[i], cv_ln_b[i],
                                  batch=batch, seq=length, col_a=pk_cv)
            x2 = _merge(s5_out, dn_out, cv_out, proj, w_brs, w_brd, w_brc, w_o, x2, mod3, 2,
                        col_gate=pk_gate, **rows)
            av = _norm_matmul(x2, norm2_w[i], mod3, 3, 4, w_up, tn=w_up.shape[1] // 2, **rows)
            hidden = _ffn_conv(av, w9, ffn_dw_b[i], batch=batch, seq=length, grid2d=grid2d)
            return _resid_matmul(hidden, w_dn, x2, mod3, 5, final_w=final_w, **rows)

        if not last:
            xc = finish(xc, proj_c, _token_major(y_rows[:, :rows_ctx], batch, ctx_len, groups),
                        dn_c, ctx_len, rows_c, False, None)
        xl = finish(xl, proj_l, _token_major(y_rows[:, rows_ctx:], batch, seq, groups), dn_l, seq,
                    rows_l, True, final_norm_w if last else None)

    return xl.reshape(batch, seq, d)
```

```python
import functools
import math

import jax
import jax.numpy as jnp
from jax import lax
from jax.experimental import pallas as pl
from jax.experimental.pallas import tpu as pltpu

F32 = jnp.float32
BF16 = jnp.bfloat16
HIGHEST = lax.Precision.HIGHEST

EPS = 1e-6
GRID_W = 64
S5_GROUP = 16
DN_HEAD_DIM = 128
N_MOD = 6
N_BRANCH = 3

LANE = 128
SUBLANE = 8
S5_CHUNK = 16
DN_CHUNK = 64
MOD_ROWS = 16
CV_ROW_TILE = 64
CV_COL_TILE = 256
VMEM_LIMIT = 56 * 1024 * 1024


def _cparams(sem, vmem=VMEM_LIMIT):
    return pltpu.CompilerParams(dimension_semantics=sem, vmem_limit_bytes=vmem)


def _dot(a, b, precision=None):
    return jnp.dot(a, b, preferred_element_type=F32, precision=precision)


def _dot_nt(a, b, precision=None):
    return lax.dot_general(a, b, (((1,), (1,)), ((), ())), preferred_element_type=F32,
                           precision=precision)


def _dot_tn(a, b, precision=None):
    return lax.dot_general(a, b, (((0,), (0,)), ((), ())), preferred_element_type=F32,
                           precision=precision)


def _silu(x):
    return x * jax.nn.sigmoid(x)


def _iota(shape, dim):
    return lax.broadcasted_iota(jnp.int32, shape, dim)


def _aligned(index, multiple):
    return index if isinstance(index, int) else pl.multiple_of(index, multiple)


def _shifted_rows(ref, base, off, rows, cols):
    lo = (off // SUBLANE) * SUBLANE
    shift = off - lo
    start = pl.multiple_of(base + lo, SUBLANE)
    if shift == 0:
        return ref[pl.ds(start, rows), cols]
    return ref[pl.ds(start, rows + SUBLANE), cols][shift:shift + rows]


def _mod_kernel(c_ref, w_ref, b_ref, o_ref):
    o_ref[...] = _dot(_silu(c_ref[...]), w_ref[...], HIGHEST) + b_ref[...]


def _modulation(cmat, ada_w, ada_b, layer):
    rows, d = cmat.shape
    depth, _, n = ada_w.shape
    tn = 1024
    return pl.pallas_call(
        _mod_kernel,
        grid=(n // tn,),
        in_specs=[pl.BlockSpec((rows, d), lambda j: (0, 0)),
                  pl.BlockSpec((None, d, tn), lambda j: (layer, 0, j)),
                  pl.BlockSpec((None, 1, tn), lambda j: (layer, 0, j))],
        out_specs=pl.BlockSpec((rows, tn), lambda j: (0, j)),
        out_shape=jax.ShapeDtypeStruct((rows, n), F32),
        compiler_params=_cparams(("parallel",)),
        name="modulation",
    )(cmat, ada_w, ada_b.reshape(depth, 1, n))


def _norm_matmul_kernel(*refs, with_bd):
    if with_bd:
        x_ref, nw_ref, sh_ref, sc_ref, w_ref, wbd_ref, o_ref, bd_ref, h_ref = refs
    else:
        x_ref, nw_ref, sh_ref, sc_ref, w_ref, o_ref, h_ref = refs

    @pl.when(pl.program_id(1) == 0)
    def _():
        x = x_ref[...]
        h = x * lax.rsqrt(jnp.mean(x * x, axis=-1, keepdims=True) + EPS) * nw_ref[...]
        h = h * (1.0 + sc_ref[0]) + sh_ref[0]
        h_ref[...] = h.astype(BF16)
        if with_bd:
            bd_ref[...] = _dot(h, wbd_ref[...], HIGHEST)

    o_ref[...] = _dot(h_ref[...], w_ref[...]).astype(o_ref.dtype)


def _row_of_tile(tm, rows_per_batch, fixed_row):
    if fixed_row is not None:
        return lambda i: fixed_row
    return lambda i: (i * tm) // rows_per_batch


def _norm_matmul(x2, norm_w, mod3, sh_idx, sc_idx, w, *, rows_per_batch, fixed_row, wbd=None,
                 tn):
    t, d = x2.shape
    n = w.shape[1]
    tm = min(1024, t if fixed_row is not None else rows_per_batch)
    assert t % tm == 0 and n % tn == 0
    row = _row_of_tile(tm, rows_per_batch, fixed_row)
    with_bd = wbd is not None
    in_specs = [pl.BlockSpec((tm, d), lambda i, j: (i, 0)),
                pl.BlockSpec((1, d), lambda i, j: (0, 0)),
                pl.BlockSpec((None, 1, d), lambda i, j: (row(i), 0, sh_idx)),
                pl.BlockSpec((None, 1, d), lambda i, j: (row(i), 0, sc_idx)),
                pl.BlockSpec((d, tn), lambda i, j: (0, j))]
    args = [x2, norm_w.reshape(1, d), mod3, mod3, w]
    out_specs = [pl.BlockSpec((tm, tn), lambda i, j: (i, j))]
    out_shape = [jax.ShapeDtypeStruct((t, n), BF16)]
    if with_bd:
        in_specs.append(pl.BlockSpec((d, LANE), lambda i, j: (0, 0)))
        args.append(wbd)
        out_specs.append(pl.BlockSpec((tm, LANE), lambda i, j: (i, 0)))
        out_shape.append(jax.ShapeDtypeStruct((t, LANE), F32))
    res = pl.pallas_call(
        functools.partial(_norm_matmul_kernel, with_bd=with_bd),
        grid=(t // tm, n // tn),
        in_specs=in_specs,
        out_specs=out_specs,
        out_shape=out_shape,
        scratch_shapes=[pltpu.VMEM((tm, d), BF16)],
        compiler_params=_cparams(("parallel", "arbitrary")),
        name="norm_matmul",
    )(*args)
    return res if with_bd else res[0]


def _dn_gates_kernel(bd_ref, aneg_ref, dtb_ref, o_ref, *, n_heads):
    x = bd_ref[...]
    tm = x.shape[0]
    y = x + dtb_ref[...]
    g = aneg_ref[...] * (jnp.maximum(y, 0.0) + jnp.log1p(jnp.exp(-jnp.abs(y))))
    blk = 2 * DN_CHUNK
    r = _iota((blk, blk), 0)
    c = _iota((blk, blk), 1)
    shift = DN_CHUNK.bit_length() - 1
    same = (r >> shift) == (c >> shift)
    tri_lo = jnp.where(same & (c <= r), 1.0, 0.0)
    tri_hi = jnp.where(same & (c >= r), 1.0, 0.0)
    parts = [g[t * blk:(t + 1) * blk] for t in range(tm // blk)]
    prefix = jnp.concatenate([_dot(tri_lo, p, HIGHEST) for p in parts], axis=0)
    suffix = jnp.concatenate([_dot(tri_hi, p, HIGHEST) for p in parts], axis=0)
    lane = _iota(x.shape, 1)
    o_ref[...] = jnp.where(lane < 2 * n_heads, jax.nn.sigmoid(x),
                           jnp.where(lane < 3 * n_heads, prefix, suffix))


def _dn_gates(bd, aneg, dtb, n_heads):
    t = bd.shape[0]
    tm = min(512, t)
    assert t % tm == 0 and tm % (2 * DN_CHUNK) == 0
    return pl.pallas_call(
        functools.partial(_dn_gates_kernel, n_heads=n_heads),
        grid=(t // tm,),
        in_specs=[pl.BlockSpec((tm, LANE), lambda i: (i, 0)),
                  pl.BlockSpec((1, LANE), lambda i: (0, 0)),
                  pl.BlockSpec((1, LANE), lambda i: (0, 0))],
        out_specs=pl.BlockSpec((tm, LANE), lambda i: (i, 0)),
        out_shape=jax.ShapeDtypeStruct((t, LANE), F32),
        compiler_params=_cparams(("parallel",)),
        name="dn_gates",
    )(bd, aneg, dtb)


def _s5_kernel(u_ref, bt_ref, c_ref, pw_ref, a_ref, y_ref, s_ref, hp_ref, *, segments, batch):
    tc, cg = S5_CHUNK, S5_GROUP
    half = LANE // 2
    n_grp = u_ref.shape[0]

    def operators(g):
        btr = [bt_ref[g, d, 0] for d in range(2)]
        bti = [bt_ref[g, d, 1] for d in range(2)]
        cr, ci = c_ref[g, 0], c_ref[g, 1]
        pr = [pw_ref[g, d, 0] for d in range(2)]
        pi = [pw_ref[g, d, 1] for d in range(2)]
        l_re, l_im = [], []
        for m in range(2 * tc):
            tau = tc - 1 - m
            d = 0 if tau >= 0 else 1
            if tau == 0:
                l_re.append(btr[0] + btr[1])
                l_im.append(bti[0] + bti[1])
            elif tau == -tc:
                l_re.append(jnp.zeros_like(btr[0]))
                l_im.append(jnp.zeros_like(btr[0]))
            else:
                p_r, p_i = pr[d][abs(tau):abs(tau) + 1, :], pi[d][abs(tau):abs(tau) + 1, :]
                l_re.append(p_r * btr[d] - p_i * bti[d])
                l_im.append(p_r * bti[d] + p_i * btr[d])
        l_re = jnp.concatenate(l_re, axis=0)
        l_im = jnp.concatenate(l_im, axis=0)
        kxt = _dot_nt(cr, l_re, HIGHEST) - _dot_nt(ci, l_im, HIGHEST)
        wt_t = jnp.concatenate([kxt[:, cg * (tc - 1 - j):cg * (tc - 1 - j) + tc * cg]
                                for j in range(tc)], axis=0)
        win_rows, wst_rows = [], []
        for i in range(tc):
            pf_r, pf_i = pr[0][tc - 1 - i:tc - i, :], pi[0][tc - 1 - i:tc - i, :]
            pb_r, pb_i = pr[1][i:i + 1, :], pi[1][i:i + 1, :]
            win_rows.append(jnp.concatenate([
                pf_r * btr[0] - pf_i * bti[0], pb_r * btr[1] - pb_i * bti[1],
                pf_r * bti[0] + pf_i * btr[0], pb_r * bti[1] + pb_i * btr[1]], axis=1))
            qf_r, qf_i = pr[0][i + 1:i + 2, :], pi[0][i + 1:i + 2, :]
            qb_r, qb_i = pr[1][tc - i:tc - i + 1, :], pi[1][tc - i:tc - i + 1, :]
            wst_rows.append(jnp.concatenate([
                cr * qf_r - ci * qf_i, cr * qb_r - ci * qb_i,
                -(cr * qf_i + ci * qf_r), -(cr * qb_i + ci * qb_r)], axis=1))
        win = jnp.concatenate(win_rows, axis=0)
        wst_t = jnp.concatenate(wst_rows, axis=0)
        return wt_t.astype(BF16), win.astype(BF16), wst_t.astype(BF16)

    ops = [operators(g) for g in range(n_grp)]
    for g in range(n_grp):
        s_ref[g] = _dot(u_ref[g], ops[g][1])
    ars = [a_ref[g, 0:1, :] for g in range(n_grp)]
    ais = [a_ref[g, 1:2, :] for g in range(n_grp)]
    is_fwd = _iota((batch, LANE), 1) < half
    carry = tuple(jnp.zeros((batch, LANE), F32) for _ in range(2 * n_grp))
    for row0, n_chunks in segments:
        def body(s, carry, row0=row0, n_chunks=n_chunks):
            rf = pl.ds(pl.multiple_of(row0 + s * batch, batch), batch)
            rb = pl.ds(pl.multiple_of(row0 + (n_chunks - 1 - s) * batch, batch), batch)
            new = []
            for g in range(n_grp):
                hre, him = carry[2 * g], carry[2 * g + 1]
                hp_ref[g, rf, 0:half] = hre[:, 0:half]
                hp_ref[g, rb, half:LANE] = hre[:, half:LANE]
                hp_ref[g, rf, LANE:LANE + half] = him[:, 0:half]
                hp_ref[g, rb, LANE + half:2 * LANE] = him[:, half:LANE]
                sf = s_ref[g, rf, :]
                sb = s_ref[g, rb, :]
                sre = jnp.where(is_fwd, sf[:, 0:LANE], sb[:, 0:LANE])
                sim = jnp.where(is_fwd, sf[:, LANE:2 * LANE], sb[:, LANE:2 * LANE])
                new.append(ars[g] * hre - ais[g] * him + sre)
                new.append(ars[g] * him + ais[g] * hre + sim)
            return tuple(new)

        carry = lax.fori_loop(0, n_chunks, body, carry)
    for g in range(n_grp):
        y = _dot_nt(u_ref[g], ops[g][0]) + _dot_nt(hp_ref[g].astype(BF16), ops[g][2])
        y_ref[g] = y.astype(y_ref.dtype)


S5_GROUPS_PER_STEP = 4


def _s5_core(ug, bt, cc, pw, avec, *, segments, batch):
    g, rows, k = ug.shape
    gb = S5_GROUPS_PER_STEP
    assert batch == SUBLANE and k == 2 * LANE and g % gb == 0
    act = pl.BlockSpec((gb, rows, k), lambda i: (i, 0, 0))
    per_group = lambda a: pl.BlockSpec((gb,) + a.shape[1:],
                                       lambda i: (i,) + (0,) * (a.ndim - 1))
    return pl.pallas_call(
        functools.partial(_s5_kernel, segments=segments, batch=batch),
        grid=(g // gb,),
        in_specs=[act, per_group(bt), per_group(cc), per_group(pw), per_group(avec)],
        out_specs=act,
        out_shape=jax.ShapeDtypeStruct((g, rows, k), BF16),
        scratch_shapes=[pltpu.VMEM((gb, rows, k), F32), pltpu.VMEM((gb, rows, k), F32)],
        compiler_params=_cparams(("parallel",)),
        name="s5_core",
    )(ug, bt, cc, pw, avec)


def _s5_params(a_re, a_im, log_dt, b_re, b_im, c_re, c_im):
    tc = S5_CHUNK
    dt = jnp.exp(log_dt)[..., None]
    mag = jnp.exp(a_re * dt)
    abr, abi = mag * jnp.cos(a_im * dt), mag * jnp.sin(a_im * dt)
    den = a_re * a_re + a_im * a_im
    cr = ((abr - 1.0) * a_re + abi * a_im) / den
    ci = (abi * a_re - (abr - 1.0) * a_im) / den
    bbr = cr[..., None] * b_re - ci[..., None] * b_im
    bbi = cr[..., None] * b_im + ci[..., None] * b_re
    n = jnp.arange(tc + 1, dtype=F32)[:, None, None, None]
    pmag = jnp.exp(a_re * dt * n)
    pr, pi = pmag * jnp.cos(a_im * dt * n), pmag * jnp.sin(a_im * dt * n)
    bt = jnp.stack([jnp.swapaxes(bbr, 2, 3), jnp.swapaxes(bbi, 2, 3)], axis=2)
    bt = jnp.transpose(bt, (1, 0, 2, 3, 4))
    cc = jnp.stack([c_re, c_im], axis=1)
    pw = jnp.transpose(jnp.stack([pr, pi], axis=0), (3, 2, 0, 1, 4))
    pw = jnp.pad(pw, ((0, 0), (0, 0), (0, 0), (0, SUBLANE - 1), (0, 0)))
    avec = jnp.stack([jnp.concatenate([pr[tc, 0], pr[tc, 1]], axis=-1),
                      jnp.concatenate([pi[tc, 0], pi[tc, 1]], axis=-1)], axis=1)
    return bt, cc, pw, avec


def _s5_post_kernel(y_ref, u_ref, d_ref, w_ref, o_ref):
    y = y_ref[...].astype(F32) + d_ref[...] * u_ref[...].astype(F32)
    k0 = math.sqrt(2.0 / math.pi)
    g = 0.5 * y * (1.0 + jnp.tanh(k0 * (y + 0.044715 * (y * y * y))))
    o_ref[...] = (g * jax.nn.sigmoid(_dot(g.astype(BF16), w_ref[...]))).astype(o_ref.dtype)


def _s5_post(y, proj, s5_d, w_glu):
    t, d = y.shape
    tm = min(512, t)
    return pl.pallas_call(
        _s5_post_kernel,
        grid=(t // tm,),
        in_specs=[pl.BlockSpec((tm, d), lambda i: (i, 0)),
                  pl.BlockSpec((tm, d), lambda i: (i, 0)),
                  pl.BlockSpec((1, d), lambda i: (0, 0)),
                  pl.BlockSpec((d, d), lambda i: (0, 0))],
        out_specs=pl.BlockSpec((tm, d), lambda i: (i, 0)),
        out_shape=jax.ShapeDtypeStruct((t, d), BF16),
        compiler_params=_cparams(("parallel",)),
        name="s5_post",
    )(y, proj, s5_d.reshape(1, d), w_glu)


def _unit_tri_inverses(ms, r, c):
    mm = lambda a, b: _dot(a.astype(BF16), b.astype(BF16))
    eye = jnp.where(r == c, 1.0, 0.0)
    blk = lambda s: (r >> s) == (c >> s)
    mds = [jnp.where(blk(3), m, 0.0) for m in ms]
    m2s = [mm(md, md) for md in mds]
    m4s = [mm(m2, m2) for m2 in m2s]
    ts = [eye - md for md in mds]
    ts = [t + mm(t, m2) for t, m2 in zip(ts, m2s)]
    ts = [t + mm(t, m4) for t, m4 in zip(ts, m4s)]
    for s in (3, 4, 5):
        off = blk(s + 1) & jnp.logical_not(blk(s))
        xs = [mm(jnp.where(off, m, 0.0), t) for m, t in zip(ms, ts)]
        ts = [t - mm(t, x) for t, x in zip(ts, xs)]
    return ts


def _dn_kernel(q_ref, k_ref, v_ref, z_ref, col_ref, cwq_ref, cwk_ref, cwv_ref, nw_ref, s0_ref,
               o_ref, sfin_ref, xp_ref, qs_ref, ks_ref, vs_ref, u_ref, w_ref, a_ref, qg_ref,
               kt_ref, el_ref, od_ref, s_ref, *, seq, n_heads, hp, ca):
    c_sz = DN_CHUNK
    hd = DN_HEAD_DIM
    pad = SUBLANE
    rt = min(256, seq)
    n_chunks = seq // c_sz
    head0 = pl.program_id(1) * hp

    def prep(x_ref, cw_ref, dst_ref, normalise, scale):
        zeros = jnp.zeros((pad, hp * hd), F32)
        xp_ref[0:pad, :] = zeros
        xp_ref[seq + pad:seq + 2 * pad, :] = zeros
        xp_ref[pad:seq + pad, :] = x_ref[...].astype(F32)
        w = cw_ref[...]
        for t in range(seq // rt):
            r0 = t * rt
            acc = (w[0:1, :] * xp_ref[r0 + pad - 1:r0 + pad - 1 + rt, :]
                   + w[1:2, :] * xp_ref[r0 + pad:r0 + pad + rt, :]
                   + w[2:3, :] * xp_ref[r0 + pad + 1:r0 + pad + 1 + rt, :])
            y = _silu(acc)
            if normalise:
                parts = []
                for hl in range(hp):
                    yh = y[:, hl * hd:(hl + 1) * hd]
                    parts.append(yh * (lax.rsqrt(jnp.sum(yh * yh, axis=-1, keepdims=True) + EPS)
                                       * scale))
                y = jnp.concatenate(parts, axis=1)
            dst_ref[r0:r0 + rt, :] = y

    prep(q_ref, cwq_ref, qs_ref, True, hd ** -0.5)
    prep(k_ref, cwk_ref, ks_ref, True, 1.0)
    prep(v_ref, cwv_ref, vs_ref, False, 1.0)

    r = _iota((c_sz, c_sz), 0)
    c = _iota((c_sz, c_sz), 1)
    lane = _iota((c_sz, LANE), 1)

    def pick(col, idx):
        v = jnp.sum(jnp.where(lane == idx, col, 0.0), axis=-1, keepdims=True)
        return jnp.broadcast_to(v, (c_sz, LANE))

    def phase_a(it):
        chains = []
        for cc in range(ca):
            for direction in range(2):
                step = it * ca + cc
                chunk = step if direction == 0 else n_chunks - 1 - step
                rows = pl.ds(_aligned(chunk * c_sz, c_sz), c_sz)
                col = col_ref[rows, :]
                for hl in range(hp):
                    hs = slice(hl * hd, (hl + 1) * hd)
                    chains.append((chunk, rows, hl, direction, ks_ref[rows, hs], qs_ref[rows, hs],
                                   vs_ref[rows, hs],
                                   pick(col, head0 + hl + direction * n_heads),
                                   pick(col, head0 + hl + (2 + direction) * n_heads)))
        k16s = [ch[4].astype(BF16) for ch in chains]
        kkts = [_dot_nt(k16, k16) for k16 in k16s]
        qkts = [_dot_nt(ch[5].astype(BF16), k16) for ch, k16 in zip(chains, k16s)]
        decays, ms = [], []
        for (chunk, rows, hl, direction, kc, qc, vc, bc, gc), kkt in zip(chains, kkts):
            lower = direction == 0
            grow = jnp.concatenate([gc, gc], axis=0).T[0:c_sz, 0:c_sz]
            incl = (r >= c) if lower else (r <= c)
            strict = (r > c) if lower else (r < c)
            decay = jnp.where(incl, jnp.exp(jnp.minimum(gc[:, 0:c_sz] - grow, 0.0)), 0.0)
            decays.append(decay)
            ms.append(jnp.where(strict, bc[:, 0:c_sz] * kkt * decay, 0.0))
        tinvs = _unit_tri_inverses(ms, r, c)
        egs = [jnp.exp(ch[8]) for ch in chains]
        uws = []
        for (chunk, rows, hl, direction, kc, qc, vc, bc, gc), tinv, eg in zip(chains, tinvs, egs):
            rhs = jnp.concatenate([vc * bc, kc * bc * eg], axis=1).astype(BF16)
            uws.append(_dot(tinv.astype(BF16), rhs))
        for (chunk, rows, hl, direction, kc, qc, vc, bc, gc), qkt, decay, eg, uw in zip(
                chains, qkts, decays, egs, uws):
            idx = hl * 2 + direction
            g_last = gc[c_sz - 1:c_sz, :] if direction == 0 else gc[0:1, :]
            u_ref[idx, rows, :] = uw[:, 0:hd]
            w_ref[idx, rows, :] = uw[:, hd:2 * hd].astype(BF16)
            a_ref[idx, rows, :] = (qkt * decay).astype(BF16)
            qg_ref[idx, rows, :] = (qc * eg).astype(BF16)
            kt_ref[idx, rows, :] = (kc * jnp.exp(g_last - gc)).astype(BF16)
            el_ref[idx, pl.ds(_aligned(chunk * SUBLANE, SUBLANE), SUBLANE), :] = (
                jnp.broadcast_to(jnp.exp(g_last), (SUBLANE, LANE)))

    for hl in range(hp):
        for direction in range(2):
            s_ref[hl * 2 + direction] = s0_ref[direction, hl]

    def phase_b(step):
        ids, rows, erows = [], [], []
        for hl in range(hp):
            for direction in range(2):
                chunk = step if direction == 0 else n_chunks - 1 - step
                ids.append(hl * 2 + direction)
                rows.append(pl.ds(_aligned(chunk * c_sz, c_sz), c_sz))
                erows.append(pl.ds(_aligned(chunk * SUBLANE, SUBLANE), SUBLANE))
        ss = [s_ref[i] for i in ids]
        s16s = [s.astype(BF16) for s in ss]
        wss = [_dot(w_ref[i, rw, :], s16) for i, rw, s16 in zip(ids, rows, s16s)]
        qss = [_dot(qg_ref[i, rw, :], s16) for i, rw, s16 in zip(ids, rows, s16s)]
        vns = [(u_ref[i, rw, :] - ws).astype(BF16) for i, rw, ws in zip(ids, rows, wss)]
        avs = [_dot(a_ref[i, rw, :], vn) for i, rw, vn in zip(ids, rows, vns)]
        kvs = [_dot_tn(kt_ref[i, rw, :], vn) for i, rw, vn in zip(ids, rows, vns)]
        for i, rw, er, s, qs_, av, kv in zip(ids, rows, erows, ss, qss, avs, kvs):
            od_ref[i, rw, :] = qs_ + av
            s_ref[i] = s * el_ref[i, er, :][0:1, :] + kv

    n_blocks = n_chunks // ca
    phase_a(0)

    def pipelined(it, carry):
        for s in range(ca):
            phase_b((it - 1) * ca + s)
        phase_a(it)
        return carry

    lax.fori_loop(1, n_blocks, pipelined, 0)
    for s in range(ca):
        phase_b((n_blocks - 1) * ca + s)

    for hl in range(hp):
        for direction in range(2):
            sfin_ref[direction, hl] = s_ref[hl * 2 + direction]

    for t in range(seq // rt):
        r0 = t * rt
        for hl in range(hp):
            hs = slice(hl * hd, (hl + 1) * hd)
            o = od_ref[hl * 2, r0:r0 + rt, :] + od_ref[hl * 2 + 1, r0:r0 + rt, :]
            o = o * lax.rsqrt(jnp.mean(o * o, axis=-1, keepdims=True) + EPS) * nw_ref[...]
            o_ref[r0:r0 + rt, hs] = (o * _silu(z_ref[r0:r0 + rt, hs].astype(F32))).astype(o_ref.dtype)


def _deltanet(proj, col, conv_w, norm_w, s0, *, batch, seq, n_heads, col_q, col_z):
    t = proj.shape[0]
    hd = DN_HEAD_DIM
    hp = 2
    d = n_heads * hd
    wd = hp * hd
    n_chunks = seq // DN_CHUNK
    ca = min(8, n_chunks)
    assert n_heads % hp == 0 and col_q % hp == 0 and col_z % hp == 0 and n_chunks % ca == 0
    blk = lambda off: pl.BlockSpec((seq, wd), lambda b, h: (b, off // hp + h))
    cw = lambda off: pl.BlockSpec((3, wd), lambda b, h: (0, off // hp + h))
    st = pl.BlockSpec((None, 2, hp, hd, hd), lambda b, h: (b, 0, h, 0, 0))
    nst = 2 * hp
    return pl.pallas_call(
        functools.partial(_dn_kernel, seq=seq, n_heads=n_heads, hp=hp, ca=ca),
        grid=(batch, n_heads // hp),
        in_specs=[blk(col_q), blk(col_q + n_heads), blk(col_q + 2 * n_heads), blk(col_z),
                  pl.BlockSpec((seq, LANE), lambda b, h: (b, 0)),
                  cw(0), cw(n_heads), cw(2 * n_heads),
                  pl.BlockSpec((1, hd), lambda b, h: (0, 0)),
                  st],
        out_specs=[pl.BlockSpec((seq, wd), lambda b, h: (b, h)), st],
        out_shape=[jax.ShapeDtypeStruct((t, d), BF16),
                   jax.ShapeDtypeStruct(s0.shape, F32)],
        scratch_shapes=[pltpu.VMEM((seq + 2 * SUBLANE, wd), F32),
                        pltpu.VMEM((seq, wd), F32), pltpu.VMEM((seq, wd), F32),
                        pltpu.VMEM((seq, wd), F32),
                        pltpu.VMEM((nst, seq, hd), F32),
                        pltpu.VMEM((nst, seq, hd), BF16),
                        pltpu.VMEM((nst, seq, DN_CHUNK), BF16),
                        pltpu.VMEM((nst, seq, hd), BF16),
                        pltpu.VMEM((nst, seq, hd), BF16),
                        pltpu.VMEM((nst, n_chunks * SUBLANE, LANE), F32),
                        pltpu.VMEM((nst, seq, hd), F32),
                        pltpu.VMEM((nst, hd, hd), F32)],
        compiler_params=_cparams(("parallel", "parallel")),
        name="deltanet",
    )(proj, proj, proj, proj, col, conv_w, conv_w, conv_w, norm_w.reshape(1, hd), s0)


def _conv_module_kernel(a_ref, g_ref, w_ref, b_ref, lnw_ref, lnb_ref, o_ref, yp_ref, tmp_ref,
                        sh_ref, *, seq, taps):
    d = a_ref.shape[1]
    half = taps // 2
    pad = 2 * SUBLANE
    assert half < pad
    rt = CV_ROW_TILE
    ct = CV_COL_TILE
    zeros = jnp.zeros((pad, d), F32)
    yp_ref[0:pad, :] = zeros
    yp_ref[seq + pad:seq + 2 * pad, :] = zeros

    def fill(t, carry):
        r0 = pl.multiple_of(t * rt, rt)
        a = a_ref[pl.ds(r0, rt), :].astype(F32)
        g = g_ref[pl.ds(r0, rt), :].astype(F32)
        yp_ref[pl.ds(r0 + pad, rt), :] = a * jax.nn.sigmoid(g)
        return carry

    lax.fori_loop(0, seq // rt, fill, 0)

    def tile(t, carry):
        r0 = pl.multiple_of(t * rt, rt)
        for cc in range(d // ct):
            cs = slice(cc * ct, (cc + 1) * ct)
            win = yp_ref[pl.ds(r0, rt + 2 * pad), cs]
            span = rt + 2 * pad - SUBLANE
            for m in range(1, SUBLANE):
                sh_ref[m - 1] = win[m:m + span]
            acc = jnp.zeros((rt, ct), F32) + b_ref[:, cs]
            for j in range(taps):
                off = pad - half + j
                lo = (off // SUBLANE) * SUBLANE
                m = off - lo
                if m == 0:
                    rows = yp_ref[pl.ds(pl.multiple_of(r0 + lo, SUBLANE), rt), cs]
                else:
                    rows = sh_ref[m - 1, lo:lo + rt, :]
                acc = acc + w_ref[j:j + 1, cs] * rows
            tmp_ref[:, cs] = acc
        y = tmp_ref[...]
        mu = jnp.mean(y, axis=-1, keepdims=True)
        yc = y - mu
        var = jnp.mean(yc * yc, axis=-1, keepdims=True)
        y = yc * lax.rsqrt(var + EPS) * lnw_ref[...] + lnb_ref[...]
        o_ref[pl.ds(r0, rt), :] = _silu(y).astype(o_ref.dtype)
        return carry

    lax.fori_loop(0, seq // rt, tile, 0)


def _conv_module(proj, w, b, lnw, lnb, *, batch, seq, col_a):
    t = proj.shape[0]
    taps, d = w.shape
    return pl.pallas_call(
        functools.partial(_conv_module_kernel, seq=seq, taps=taps),
        grid=(batch,),
        in_specs=[pl.BlockSpec((seq, d), lambda i: (i, col_a)),
                  pl.BlockSpec((seq, d), lambda i: (i, col_a + 1)),
                  pl.BlockSpec((taps, d), lambda i: (0, 0)),
                  pl.BlockSpec((1, d), lambda i: (0, 0)),
                  pl.BlockSpec((1, d), lambda i: (0, 0)),
                  pl.BlockSpec((1, d), lambda i: (0, 0))],
        out_specs=pl.BlockSpec((seq, d), lambda i: (i, 0)),
        out_shape=jax.ShapeDtypeStruct((t, d), BF16),
        scratch_shapes=[pltpu.VMEM((seq + 4 * SUBLANE, d), F32),
                        pltpu.VMEM((CV_ROW_TILE, d), F32),
                        pltpu.VMEM((SUBLANE - 1, CV_ROW_TILE + 3 * SUBLANE, CV_COL_TILE), F32)],
        compiler_params=_cparams(("parallel",)),
        name="conv_module",
    )(proj, proj, w, b.reshape(1, d), lnw.reshape(1, d), lnb.reshape(1, d))


def _merge_kernel(s5_ref, dn_ref, cv_ref, g0_ref, g1_ref, g2_ref, w0_ref, w1_ref, w2_ref,
                  wo_ref, x_ref, gate_ref, o_ref):
    merged = (jax.nn.sigmoid(g0_ref[...].astype(F32)) * _dot(s5_ref[...], w0_ref[...])
              + jax.nn.sigmoid(g1_ref[...].astype(F32)) * _dot(dn_ref[...], w1_ref[...])
              + jax.nn.sigmoid(g2_ref[...].astype(F32)) * _dot(cv_ref[...], w2_ref[...]))
    y = _dot(merged.astype(BF16), wo_ref[...])
    o_ref[...] = x_ref[...] + gate_ref[0] * y


def _merge(s5_out, dn_out, cv_out, proj, w_s5, w_dn, w_cv, w_out, x2, mod3, gate_idx,
           *, rows_per_batch, fixed_row, col_gate):
    t, d = x2.shape
    tm = min(512, t if fixed_row is not None else rows_per_batch)
    assert t % tm == 0
    row = _row_of_tile(tm, rows_per_batch, fixed_row)
    act = pl.BlockSpec((tm, d), lambda i: (i, 0))
    gat = lambda k: pl.BlockSpec((tm, d), lambda i: (i, col_gate + k))
    wsp = pl.BlockSpec((d, d), lambda i: (0, 0))
    return pl.pallas_call(
        _merge_kernel,
        grid=(t // tm,),
        in_specs=[act, act, act, gat(0), gat(1), gat(2), wsp, wsp, wsp, wsp, act,
                  pl.BlockSpec((None, 1, d), lambda i: (row(i), 0, gate_idx))],
        out_specs=act,
        out_shape=jax.ShapeDtypeStruct((t, d), F32),
        compiler_params=_cparams(("parallel",)),
        name="merge",
    )(s5_out, dn_out, cv_out, proj, proj, proj, w_s5, w_dn, w_cv, w_out, x2, mod3)


def _ffn_conv_kernel(a_ref, v_ref, w_ref, b_ref, o_ref, a0_ref, al_ref, ar_ref, *, seq, grid2d):
    ct = a_ref.shape[1]
    pad = 72 if grid2d else SUBLANE
    rt = 128
    zeros = jnp.zeros((pad, ct), F32)
    for buf in (a0_ref, al_ref, ar_ref):
        buf[0:pad, :] = zeros
        buf[seq + pad:seq + 2 * pad, :] = zeros

    def fill(t, carry):
        r0 = pl.multiple_of(t * rt, rt)
        a0_ref[pl.ds(r0 + pad, rt), :] = a_ref[pl.ds(r0, rt), :].astype(F32)
        return carry

    lax.fori_loop(0, seq // rt, fill, 0)

    def neighbours(t, carry):
        r0 = pl.multiple_of(t * rt, rt)
        left = _shifted_rows(a0_ref, r0 + pad, -1, rt, slice(None))
        right = _shifted_rows(a0_ref, r0 + pad, 1, rt, slice(None))
        if grid2d:
            colpos = (_iota((rt, ct), 0) + r0) & (GRID_W - 1)
            left = jnp.where(colpos == 0, 0.0, left)
            right = jnp.where(colpos == GRID_W - 1, 0.0, right)
        al_ref[pl.ds(r0 + pad, rt), :] = left
        ar_ref[pl.ds(r0 + pad, rt), :] = right
        return carry

    lax.fori_loop(0, seq // rt, neighbours, 0)

    if grid2d:
        taps = [(dr, dc) for dr in (-1, 0, 1) for dc in (-1, 0, 1)]
    else:
        taps = [(0, dc) for dc in (-1, 0, 1)]
    src = {-1: al_ref, 0: a0_ref, 1: ar_ref}

    def tile(t, carry):
        r0 = pl.multiple_of(t * rt, rt)
        acc = jnp.zeros((rt, ct), F32) + b_ref[...]
        for dr, dc in taps:
            widx = (dr + 1) * 3 + (dc + 1)
            rows = pl.ds(pl.multiple_of(r0 + pad + dr * GRID_W, SUBLANE), rt)
            acc = acc + w_ref[widx:widx + 1, :] * src[dc][rows, :]
        o_ref[pl.ds(r0, rt), :] = (_silu(acc) * v_ref[pl.ds(r0, rt), :].astype(F32)).astype(o_ref.dtype)
        return carry

    lax.fori_loop(0, seq // rt, tile, 0)


def _ffn_conv(av, w9, bias, *, batch, seq, grid2d):
    t = av.shape[0]
    f = w9.shape[1]
    ct = 256
    assert f % ct == 0 and seq % 128 == 0
    nct = f // ct
    pad = 72 if grid2d else SUBLANE
    return pl.pallas_call(
        functools.partial(_ffn_conv_kernel, seq=seq, grid2d=grid2d),
        grid=(batch, nct),
        in_specs=[pl.BlockSpec((seq, ct), lambda b, c: (b, c)),
                  pl.BlockSpec((seq, ct), lambda b, c: (b, nct + c)),
                  pl.BlockSpec((9, ct), lambda b, c: (0, c)),
                  pl.BlockSpec((1, ct), lambda b, c: (0, c))],
        out_specs=pl.BlockSpec((seq, ct), lambda b, c: (b, c)),
        out_shape=jax.ShapeDtypeStruct((t, f), BF16),
        scratch_shapes=[pltpu.VMEM((seq + 2 * pad, ct), F32)] * 3,
        compiler_params=_cparams(("parallel", "parallel")),
        name="ffn_conv",
    )(av, av, w9, bias.reshape(1, f))


def _resid_matmul_kernel(*refs, final_norm):
    if final_norm:
        a_ref, w_ref, x_ref, gate_ref, fw_ref, o_ref = refs
    else:
        a_ref, w_ref, x_ref, gate_ref, o_ref = refs
    y = x_ref[...] + gate_ref[0] * _dot(a_ref[...], w_ref[...])
    if final_norm:
        y = y * lax.rsqrt(jnp.mean(y * y, axis=-1, keepdims=True) + EPS) * fw_ref[...]
    o_ref[...] = y


def _resid_matmul(a, w, x2, mod3, gate_idx, *, rows_per_batch, fixed_row, final_w=None):
    t, d = x2.shape
    k = a.shape[1]
    tm = min(512, t if fixed_row is not None else rows_per_batch)
    assert t % tm == 0
    row = _row_of_tile(tm, rows_per_batch, fixed_row)
    final_norm = final_w is not None
    in_specs = [pl.BlockSpec((tm, k), lambda i: (i, 0)),
                pl.BlockSpec((k, d), lambda i: (0, 0)),
                pl.BlockSpec((tm, d), lambda i: (i, 0)),
                pl.BlockSpec((None, 1, d), lambda i: (row(i), 0, gate_idx))]
    args = [a, w, x2, mod3]
    if final_norm:
        in_specs.append(pl.BlockSpec((1, d), lambda i: (0, 0)))
        args.append(final_w.reshape(1, d))
    return pl.pallas_call(
        functools.partial(_resid_matmul_kernel, final_norm=final_norm),
        grid=(t // tm,),
        in_specs=in_specs,
        out_specs=pl.BlockSpec((tm, d), lambda i: (i, 0)),
        out_shape=jax.ShapeDtypeStruct((t, d), F32),
        compiler_params=_cparams(("parallel",)),
        name="resid_matmul",
    )(*args)


def _group_major(u2, batch, seq, groups):
    nk = seq // S5_CHUNK
    u5 = u2.reshape(batch, nk, S5_CHUNK, groups, S5_GROUP)
    return jnp.transpose(u5, (3, 1, 0, 2, 4)).reshape(groups, nk * batch, S5_CHUNK * S5_GROUP)


def _token_major(yg, batch, seq, groups):
    nk = seq // S5_CHUNK
    y5 = yg.reshape(groups, nk, batch, S5_CHUNK, S5_GROUP)
    return jnp.transpose(y5, (2, 1, 3, 0, 4)).reshape(batch * seq, groups * S5_GROUP)


def kernel(x, c, ctx, c_ctx, ada_w, ada_b, norm1_w, norm2_w, w_in, s5_a_re, s5_a_im, s5_log_dt, s5_b_re, s5_b_im, s5_c_re, s5_c_im, s5_d, s5_w_glu, dn_conv_w, dn_a_log, dn_dt_bias, dn_norm_w, cv_dw_w, cv_dw_b, cv_ln_w, cv_ln_b, w_br_s5, w_br_dn, w_br_cv, w_out, ffn_w_up, ffn_dw_w, ffn_dw_b, ffn_w_down, final_norm_w):
    batch, seq, d = x.shape
    ctx_len = ctx.shape[1]
    depth = ada_w.shape[0]
    groups = s5_a_re.shape[2]
    n_heads = dn_a_log.shape[2]
    dn_width = n_heads * DN_HEAD_DIM
    ffn_hidden = ffn_dw_b.shape[1]
    assert batch == SUBLANE and batch < MOD_ROWS and d == groups * S5_GROUP == dn_width
    assert d % 1024 == 0 and 4 * n_heads <= LANE

    col_qkv = d
    col_beta = col_qkv + 3 * dn_width
    n_state = col_beta + 4 * n_heads
    col_cv = n_state + dn_width
    col_gate = col_cv + 2 * d
    pk_q = d // LANE
    pk_z = (d + 3 * dn_width) // LANE
    pk_cv = (d + 4 * dn_width) // d
    pk_gate = pk_cv + 2

    cmat = jnp.zeros((MOD_ROWS, d), F32).at[:batch].set(c).at[batch].set(c_ctx)
    xl = x.reshape(batch * seq, d)
    xc = ctx.reshape(batch * ctx_len, d)

    for i in range(depth):
        last = i == depth - 1
        w_main = jnp.concatenate([w_in[i][:, :col_beta].astype(BF16),
                                  w_in[i][:, n_state:].astype(BF16)], axis=1)
        w_bd = jnp.pad(w_in[i][:, col_beta:n_state], ((0, 0), (0, LANE - 4 * n_heads)))
        aneg = jnp.pad(-jnp.exp(dn_a_log[i].reshape(1, -1)), ((0, 0), (2 * n_heads, LANE - 4 * n_heads)))
        dtb = jnp.pad(dn_dt_bias[i].reshape(1, -1), ((0, 0), (2 * n_heads, LANE - 4 * n_heads)))
        s5_par = _s5_params(s5_a_re[i], s5_a_im[i], s5_log_dt[i], s5_b_re[i], s5_b_im[i],
                            s5_c_re[i], s5_c_im[i])
        w_glu = s5_w_glu[i].astype(BF16)
        w_brs = w_br_s5[i].astype(BF16)
        w_brd = w_br_dn[i].astype(BF16)
        w_brc = w_br_cv[i].astype(BF16)
        w_o = w_out[i].astype(BF16)
        w_up = ffn_w_up[i].astype(BF16)
        w_dn = ffn_w_down[i].astype(BF16)
        w9 = ffn_dw_w[i].reshape(9, ffn_hidden)

        mod3 = _modulation(cmat, ada_w, ada_b, i).reshape(MOD_ROWS, 1, N_MOD * d)

        rows_c = dict(rows_per_batch=ctx_len, fixed_row=batch)
        rows_l = dict(rows_per_batch=seq, fixed_row=None)
        proj_c, bd_c = _norm_matmul(xc, norm1_w[i], mod3, 0, 1, w_main, wbd=w_bd, tn=2 * d,
                                    **rows_c)
        proj_l, bd_l = _norm_matmul(xl, norm1_w[i], mod3, 0, 1, w_main, wbd=w_bd, tn=2 * d,
                                    **rows_l)

        u_rows = jnp.concatenate([_group_major(proj_c[:, :d], batch, ctx_len, groups),
                                  _group_major(proj_l[:, :d], batch, seq, groups)], axis=1)
        rows_ctx = (ctx_len // S5_CHUNK) * batch
        segments = ((0, ctx_len // S5_CHUNK), (rows_ctx, seq // S5_CHUNK))
        y_rows = _s5_core(u_rows, *s5_par, segments=segments, batch=batch)

        zero_dn = jnp.zeros((batch, 2, n_heads, DN_HEAD_DIM, DN_HEAD_DIM), F32)
        dn_args = dict(batch=batch, n_heads=n_heads, col_q=pk_q, col_z=pk_z)
        dn_c, dn_state = _deltanet(proj_c, _dn_gates(bd_c, aneg, dtb, n_heads), dn_conv_w[i],
                                   dn_norm_w[i], zero_dn, seq=ctx_len, **dn_args)
        dn_l, _ = _deltanet(proj_l, _dn_gates(bd_l, aneg, dtb, n_heads), dn_conv_w[i],
                            dn_norm_w[i], dn_state, seq=seq, **dn_args)

        def finish(x2, proj, y, dn_out, length, rows, grid2d, final_w):
            s5_out = _s5_post(y, proj, s5_d[i], w_glu)
            cv_out = _conv_module(proj, cv_dw_w[i], cv_dw_b[i], cv_ln_w[i], cv_ln_b[i],
                                  batch=batch, seq=length, col_a=pk_cv)
            x2 = _merge(s5_out, dn_out, cv_out, proj, w_brs, w_brd, w_brc, w_o, x2, mod3, 2,
                        col_gate=pk_gate, **rows)
            av = _norm_matmul(x2, norm2_w[i], mod3, 3, 4, w_up, tn=w_up.shape[1] // 2, **rows)
            hidden = _ffn_conv(av, w9, ffn_dw_b[i], batch=batch, seq=length, grid2d=grid2d)
            return _resid_matmul(hidden, w_dn, x2, mod3, 5, final_w=final_w, **rows)

        if not last:
            xc = finish(xc, proj_c, _token_major(y_rows[:, :rows_ctx], batch, ctx_len, groups),
                        dn_c, ctx_len, rows_c, False, None)
        xl = finish(xl, proj_l, _token_major(y_rows[:, rows_ctx:], batch, seq, groups), dn_l, seq,
                    rows_l, True, final_norm_w if last else None)

    return xl.reshape(batch, seq, d)
```

```python
import functools
import math

import jax
import jax.numpy as jnp
from jax import lax
from jax.experimental import pallas as pl
from jax.experimental.pallas import tpu as pltpu

F32 = jnp.float32
BF16 = jnp.bfloat16
HIGHEST = lax.Precision.HIGHEST

EPS = 1e-6
GRID_W = 64
S5_GROUP = 16
DN_HEAD_DIM = 128
N_MOD = 6
N_BRANCH = 3

LANE = 128
SUBLANE = 8
S5_CHUNK = 16
DN_CHUNK = 64
MOD_ROWS = 16
CV_ROW_TILE = 64
CV_COL_TILE = 256
VMEM_LIMIT = 56 * 1024 * 1024


def _cparams(sem, vmem=VMEM_LIMIT):
    return pltpu.CompilerParams(dimension_semantics=sem, vmem_limit_bytes=vmem)


def _dot(a, b, precision=None):
    return jnp.dot(a, b, preferred_element_type=F32, precision=precision)


def _dot_nt(a, b, precision=None):
    return lax.dot_general(a, b, (((1,), (1,)), ((), ())), preferred_element_type=F32,
                           precision=precision)


def _dot_tn(a, b, precision=None):
    return lax.dot_general(a, b, (((0,), (0,)), ((), ())), preferred_element_type=F32,
                           precision=precision)


def _silu(x):
    return x * jax.nn.sigmoid(x)


def _iota(shape, dim):
    return lax.broadcasted_iota(jnp.int32, shape, dim)


def _aligned(index, multiple):
    return index if isinstance(index, int) else pl.multiple_of(index, multiple)


def _shifted_rows(ref, base, off, rows, cols):
    lo = (off // SUBLANE) * SUBLANE
    shift = off - lo
    start = pl.multiple_of(base + lo, SUBLANE)
    if shift == 0:
        return ref[pl.ds(start, rows), cols]
    return ref[pl.ds(start, rows + SUBLANE), cols][shift:shift + rows]


def _mod_kernel(c_ref, w_ref, b_ref, o_ref):
    o_ref[...] = _dot(_silu(c_ref[...]), w_ref[...], HIGHEST) + b_ref[...]


def _modulation(cmat, ada_w, ada_b, layer):
    rows, d = cmat.shape
    depth, _, n = ada_w.shape
    tn = 1024
    return pl.pallas_call(
        _mod_kernel,
        grid=(n // tn,),
        in_specs=[pl.BlockSpec((rows, d), lambda j: (0, 0)),
                  pl.BlockSpec((None, d, tn), lambda j: (layer, 0, j)),
                  pl.BlockSpec((None, 1, tn), lambda j: (layer, 0, j))],
        out_specs=pl.BlockSpec((rows, tn), lambda j: (0, j)),
        out_shape=jax.ShapeDtypeStruct((rows, n), F32),
        compiler_params=_cparams(("parallel",)),
        name="modulation",
    )(cmat, ada_w, ada_b.reshape(depth, 1, n))


def _norm_matmul_kernel(*refs, with_bd):
    if with_bd:
        x_ref, nw_ref, sh_ref, sc_ref, w_ref, wbd_ref, o_ref, bd_ref, h_ref = refs
    else:
        x_ref, nw_ref, sh_ref, sc_ref, w_ref, o_ref, h_ref = refs

    @pl.when(pl.program_id(1) == 0)
    def _():
        x = x_ref[...]
        h = x * lax.rsqrt(jnp.mean(x * x, axis=-1, keepdims=True) + EPS) * nw_ref[...]
        h = h * (1.0 + sc_ref[0]) + sh_ref[0]
        h16 = h.astype(BF16)
        h_ref[...] = h16
        if with_bd:
            bd_ref[...] = _dot(h16, wbd_ref[...])

    o_ref[...] = _dot(h_ref[...], w_ref[...]).astype(o_ref.dtype)


def _row_of_tile(tm, rows_per_batch, fixed_row):
    if fixed_row is not None:
        return lambda i: fixed_row
    return lambda i: (i * tm) // rows_per_batch


def _norm_matmul(x2, norm_w, mod3, sh_idx, sc_idx, w, *, rows_per_batch, fixed_row, wbd=None,
                 tn):
    t, d = x2.shape
    n = w.shape[1]
    tm = min(1024, t if fixed_row is not None else rows_per_batch)
    assert t % tm == 0 and n % tn == 0
    row = _row_of_tile(tm, rows_per_batch, fixed_row)
    with_bd = wbd is not None
    in_specs = [pl.BlockSpec((tm, d), lambda i, j: (i, 0)),
                pl.BlockSpec((1, d), lambda i, j: (0, 0)),
                pl.BlockSpec((None, 1, d), lambda i, j: (row(i), 0, sh_idx)),
                pl.BlockSpec((None, 1, d), lambda i, j: (row(i), 0, sc_idx)),
                pl.BlockSpec((d, tn), lambda i, j: (0, j))]
    args = [x2, norm_w.reshape(1, d), mod3, mod3, w]
    out_specs = [pl.BlockSpec((tm, tn), lambda i, j: (i, j))]
    out_shape = [jax.ShapeDtypeStruct((t, n), BF16)]
    if with_bd:
        in_specs.append(pl.BlockSpec((d, LANE), lambda i, j: (0, 0)))
        args.append(wbd)
        out_specs.append(pl.BlockSpec((tm, LANE), lambda i, j: (i, 0)))
        out_shape.append(jax.ShapeDtypeStruct((t, LANE), F32))
    res = pl.pallas_call(
        functools.partial(_norm_matmul_kernel, with_bd=with_bd),
        grid=(t // tm, n // tn),
        in_specs=in_specs,
        out_specs=out_specs,
        out_shape=out_shape,
        scratch_shapes=[pltpu.VMEM((tm, d), BF16)],
        compiler_params=_cparams(("parallel", "arbitrary")),
        name="norm_matmul",
    )(*args)
    return res if with_bd else res[0]


def _dn_gates_kernel(bd_ref, aneg_ref, dtb_ref, o_ref, *, n_heads):
    x = bd_ref[...]
    tm = x.shape[0]
    y = x + dtb_ref[...]
    g = aneg_ref[...] * (jnp.maximum(y, 0.0) + jnp.log1p(jnp.exp(-jnp.abs(y))))
    blk = 2 * DN_CHUNK
    r = _iota((blk, blk), 0)
    c = _iota((blk, blk), 1)
    shift = DN_CHUNK.bit_length() - 1
    same = (r >> shift) == (c >> shift)
    tri_lo = jnp.where(same & (c <= r), 1.0, 0.0)
    tri_hi = jnp.where(same & (c >= r), 1.0, 0.0)
    parts = [g[t * blk:(t + 1) * blk] for t in range(tm // blk)]
    prefix = jnp.concatenate([_dot(tri_lo, p, HIGHEST) for p in parts], axis=0)
    suffix = jnp.concatenate([_dot(tri_hi, p, HIGHEST) for p in parts], axis=0)
    lane = _iota(x.shape, 1)
    o_ref[...] = jnp.where(lane < 2 * n_heads, jax.nn.sigmoid(x),
                           jnp.where(lane < 3 * n_heads, prefix, suffix))


def _dn_gates(bd, aneg, dtb, n_heads):
    t = bd.shape[0]
    tm = min(512, t)
    assert t % tm == 0 and tm % (2 * DN_CHUNK) == 0
    return pl.pallas_call(
        functools.partial(_dn_gates_kernel, n_heads=n_heads),
        grid=(t // tm,),
        in_specs=[pl.BlockSpec((tm, LANE), lambda i: (i, 0)),
                  pl.BlockSpec((1, LANE), lambda i: (0, 0)),
                  pl.BlockSpec((1, LANE), lambda i: (0, 0))],
        out_specs=pl.BlockSpec((tm, LANE), lambda i: (i, 0)),
        out_shape=jax.ShapeDtypeStruct((t, LANE), F32),
        compiler_params=_cparams(("parallel",)),
        name="dn_gates",
    )(bd, aneg, dtb)


def _s5_kernel(u_ref, bt_ref, c_ref, pw_ref, a_ref, y_ref, s_ref, hp_ref, *, segments, batch):
    tc, cg = S5_CHUNK, S5_GROUP
    half = LANE // 2
    n_grp = u_ref.shape[0]

    def operators(g):
        btr = [bt_ref[g, d, 0] for d in range(2)]
        bti = [bt_ref[g, d, 1] for d in range(2)]
        cr, ci = c_ref[g, 0], c_ref[g, 1]
        pr = [pw_ref[g, d, 0] for d in range(2)]
        pi = [pw_ref[g, d, 1] for d in range(2)]
        l_re, l_im = [], []
        for m in range(2 * tc):
            tau = tc - 1 - m
            d = 0 if tau >= 0 else 1
            if tau == 0:
                l_re.append(btr[0] + btr[1])
                l_im.append(bti[0] + bti[1])
            elif tau == -tc:
                l_re.append(jnp.zeros_like(btr[0]))
                l_im.append(jnp.zeros_like(btr[0]))
            else:
                p_r, p_i = pr[d][abs(tau):abs(tau) + 1, :], pi[d][abs(tau):abs(tau) + 1, :]
                l_re.append(p_r * btr[d] - p_i * bti[d])
                l_im.append(p_r * bti[d] + p_i * btr[d])
        l_re = jnp.concatenate(l_re, axis=0)
        l_im = jnp.concatenate(l_im, axis=0)
        kxt = _dot_nt(cr, l_re, HIGHEST) - _dot_nt(ci, l_im, HIGHEST)
        wt_t = jnp.concatenate([kxt[:, cg * (tc - 1 - j):cg * (tc - 1 - j) + tc * cg]
                                for j in range(tc)], axis=0)
        win_rows, wst_rows = [], []
        for i in range(tc):
            pf_r, pf_i = pr[0][tc - 1 - i:tc - i, :], pi[0][tc - 1 - i:tc - i, :]
            pb_r, pb_i = pr[1][i:i + 1, :], pi[1][i:i + 1, :]
            win_rows.append(jnp.concatenate([
                pf_r * btr[0] - pf_i * bti[0], pb_r * btr[1] - pb_i * bti[1],
                pf_r * bti[0] + pf_i * btr[0], pb_r * bti[1] + pb_i * btr[1]], axis=1))
            qf_r, qf_i = pr[0][i + 1:i + 2, :], pi[0][i + 1:i + 2, :]
            qb_r, qb_i = pr[1][tc - i:tc - i + 1, :], pi[1][tc - i:tc - i + 1, :]
            wst_rows.append(jnp.concatenate([
                cr * qf_r - ci * qf_i, cr * qb_r - ci * qb_i,
                -(cr * qf_i + ci * qf_r), -(cr * qb_i + ci * qb_r)], axis=1))
        win = jnp.concatenate(win_rows, axis=0)
        wst_t = jnp.concatenate(wst_rows, axis=0)
        return wt_t.astype(BF16), win.astype(BF16), wst_t.astype(BF16)

    ops = [operators(g) for g in range(n_grp)]
    for g in range(n_grp):
        s_ref[g] = _dot(u_ref[g], ops[g][1])
    ars = [a_ref[g, 0:1, :] for g in range(n_grp)]
    ais = [a_ref[g, 1:2, :] for g in range(n_grp)]
    is_fwd = _iota((batch, LANE), 1) < half
    carry = tuple(jnp.zeros((batch, LANE), F32) for _ in range(2 * n_grp))
    for row0, n_chunks in segments:
        def body(s, carry, row0=row0, n_chunks=n_chunks):
            rf = pl.ds(pl.multiple_of(row0 + s * batch, batch), batch)
            rb = pl.ds(pl.multiple_of(row0 + (n_chunks - 1 - s) * batch, batch), batch)
            new = []
            for g in range(n_grp):
                hre, him = carry[2 * g], carry[2 * g + 1]
                hp_ref[g, rf, 0:half] = hre[:, 0:half]
                hp_ref[g, rb, half:LANE] = hre[:, half:LANE]
                hp_ref[g, rf, LANE:LANE + half] = him[:, 0:half]
                hp_ref[g, rb, LANE + half:2 * LANE] = him[:, half:LANE]
                sf = s_ref[g, rf, :]
                sb = s_ref[g, rb, :]
                sre = jnp.where(is_fwd, sf[:, 0:LANE], sb[:, 0:LANE])
                sim = jnp.where(is_fwd, sf[:, LANE:2 * LANE], sb[:, LANE:2 * LANE])
                new.append(ars[g] * hre - ais[g] * him + sre)
                new.append(ars[g] * him + ais[g] * hre + sim)
            return tuple(new)

        carry = lax.fori_loop(0, n_chunks, body, carry)
    for g in range(n_grp):
        y = _dot_nt(u_ref[g], ops[g][0]) + _dot_nt(hp_ref[g].astype(BF16), ops[g][2])
        y_ref[g] = y.astype(y_ref.dtype)


S5_GROUPS_PER_STEP = 4


def _s5_core(ug, bt, cc, pw, avec, *, segments, batch):
    g, rows, k = ug.shape
    gb = S5_GROUPS_PER_STEP
    assert batch == SUBLANE and k == 2 * LANE and g % gb == 0
    act = pl.BlockSpec((gb, rows, k), lambda i: (i, 0, 0))
    per_group = lambda a: pl.BlockSpec((gb,) + a.shape[1:],
                                       lambda i: (i,) + (0,) * (a.ndim - 1))
    return pl.pallas_call(
        functools.partial(_s5_kernel, segments=segments, batch=batch),
        grid=(g // gb,),
        in_specs=[act, per_group(bt), per_group(cc), per_group(pw), per_group(avec)],
        out_specs=act,
        out_shape=jax.ShapeDtypeStruct((g, rows, k), BF16),
        scratch_shapes=[pltpu.VMEM((gb, rows, k), F32), pltpu.VMEM((gb, rows, k), F32)],
        compiler_params=_cparams(("parallel",)),
        name="s5_core",
    )(ug, bt, cc, pw, avec)


def _s5_params(a_re, a_im, log_dt, b_re, b_im, c_re, c_im):
    tc = S5_CHUNK
    dt = jnp.exp(log_dt)[..., None]
    mag = jnp.exp(a_re * dt)
    abr, abi = mag * jnp.cos(a_im * dt), mag * jnp.sin(a_im * dt)
    den = a_re * a_re + a_im * a_im
    cr = ((abr - 1.0) * a_re + abi * a_im) / den
    ci = (abi * a_re - (abr - 1.0) * a_im) / den
    bbr = cr[..., None] * b_re - ci[..., None] * b_im
    bbi = cr[..., None] * b_im + ci[..., None] * b_re
    n = jnp.arange(tc + 1, dtype=F32)[:, None, None, None]
    pmag = jnp.exp(a_re * dt * n)
    pr, pi = pmag * jnp.cos(a_im * dt * n), pmag * jnp.sin(a_im * dt * n)
    bt = jnp.stack([jnp.swapaxes(bbr, 2, 3), jnp.swapaxes(bbi, 2, 3)], axis=2)
    bt = jnp.transpose(bt, (1, 0, 2, 3, 4))
    cc = jnp.stack([c_re, c_im], axis=1)
    pw = jnp.transpose(jnp.stack([pr, pi], axis=0), (3, 2, 0, 1, 4))
    pw = jnp.pad(pw, ((0, 0), (0, 0), (0, 0), (0, SUBLANE - 1), (0, 0)))
    avec = jnp.stack([jnp.concatenate([pr[tc, 0], pr[tc, 1]], axis=-1),
                      jnp.concatenate([pi[tc, 0], pi[tc, 1]], axis=-1)], axis=1)
    return bt, cc, pw, avec


def _s5_post_kernel(y_ref, u_ref, d_ref, w_ref, o_ref):
    y = y_ref[...].astype(F32) + d_ref[...] * u_ref[...].astype(F32)
    k0 = math.sqrt(2.0 / math.pi)
    g = 0.5 * y * (1.0 + jnp.tanh(k0 * (y + 0.044715 * (y * y * y))))
    o_ref[...] = (g * jax.nn.sigmoid(_dot(g.astype(BF16), w_ref[...]))).astype(o_ref.dtype)


def _s5_post(y, proj, s5_d, w_glu):
    t, d = y.shape
    tm = min(512, t)
    return pl.pallas_call(
        _s5_post_kernel,
        grid=(t // tm,),
        in_specs=[pl.BlockSpec((tm, d), lambda i: (i, 0)),
                  pl.BlockSpec((tm, d), lambda i: (i, 0)),
                  pl.BlockSpec((1, d), lambda i: (0, 0)),
                  pl.BlockSpec((d, d), lambda i: (0, 0))],
        out_specs=pl.BlockSpec((tm, d), lambda i: (i, 0)),
        out_shape=jax.ShapeDtypeStruct((t, d), BF16),
        compiler_params=_cparams(("parallel",)),
        name="s5_post",
    )(y, proj, s5_d.reshape(1, d), w_glu)


def _unit_tri_inverses(ms, r, c):
    mm = lambda a, b: _dot(a.astype(BF16), b.astype(BF16))
    eye = jnp.where(r == c, 1.0, 0.0)
    blk = lambda s: (r >> s) == (c >> s)
    mds = [jnp.where(blk(3), m, 0.0) for m in ms]
    m2s = [mm(md, md) for md in mds]
    m4s = [mm(m2, m2) for m2 in m2s]
    ts = [eye - md for md in mds]
    ts = [t + mm(t, m2) for t, m2 in zip(ts, m2s)]
    ts = [t + mm(t, m4) for t, m4 in zip(ts, m4s)]
    for s in (3, 4, 5):
        off = blk(s + 1) & jnp.logical_not(blk(s))
        xs = [mm(jnp.where(off, m, 0.0), t) for m, t in zip(ms, ts)]
        ts = [t - mm(t, x) for t, x in zip(ts, xs)]
    return ts


def _dn_kernel(q_ref, k_ref, v_ref, z_ref, col_ref, cwq_ref, cwk_ref, cwv_ref, nw_ref, s0_ref,
               o_ref, sfin_ref, xp_ref, qs_ref, ks_ref, vs_ref, u_ref, w_ref, a_ref, qg_ref,
               kt_ref, el_ref, od_ref, s_ref, *, seq, n_heads, hp, ca):
    c_sz = DN_CHUNK
    hd = DN_HEAD_DIM
    pad = SUBLANE
    rt = min(256, seq)
    n_chunks = seq // c_sz
    head0 = pl.program_id(1) * hp

    def prep(x_ref, cw_ref, dst_ref, normalise, scale):
        zeros = jnp.zeros((pad, hp * hd), F32)
        xp_ref[0:pad, :] = zeros
        xp_ref[seq + pad:seq + 2 * pad, :] = zeros
        xp_ref[pad:seq + pad, :] = x_ref[...].astype(F32)
        w = cw_ref[...]
        for t in range(seq // rt):
            r0 = t * rt
            acc = (w[0:1, :] * xp_ref[r0 + pad - 1:r0 + pad - 1 + rt, :]
                   + w[1:2, :] * xp_ref[r0 + pad:r0 + pad + rt, :]
                   + w[2:3, :] * xp_ref[r0 + pad + 1:r0 + pad + 1 + rt, :])
            y = _silu(acc)
            if normalise:
                parts = []
                for hl in range(hp):
                    yh = y[:, hl * hd:(hl + 1) * hd]
                    parts.append(yh * (lax.rsqrt(jnp.sum(yh * yh, axis=-1, keepdims=True) + EPS)
                                       * scale))
                y = jnp.concatenate(parts, axis=1)
            dst_ref[r0:r0 + rt, :] = y

    prep(q_ref, cwq_ref, qs_ref, True, hd ** -0.5)
    prep(k_ref, cwk_ref, ks_ref, True, 1.0)
    prep(v_ref, cwv_ref, vs_ref, False, 1.0)

    r = _iota((c_sz, c_sz), 0)
    c = _iota((c_sz, c_sz), 1)
    lane = _iota((c_sz, LANE), 1)

    def pick(col, idx):
        v = jnp.sum(jnp.where(lane == idx, col, 0.0), axis=-1, keepdims=True)
        return jnp.broadcast_to(v, (c_sz, LANE))

    def phase_a(it):
        chains = []
        for cc in range(ca):
            for direction in range(2):
                step = it * ca + cc
                chunk = step if direction == 0 else n_chunks - 1 - step
                rows = pl.ds(_aligned(chunk * c_sz, c_sz), c_sz)
                col = col_ref[rows, :]
                for hl in range(hp):
                    hs = slice(hl * hd, (hl + 1) * hd)
                    chains.append((chunk, rows, hl, direction, ks_ref[rows, hs], qs_ref[rows, hs],
                                   vs_ref[rows, hs],
                                   pick(col, head0 + hl + direction * n_heads),
                                   pick(col, head0 + hl + (2 + direction) * n_heads)))
        k16s = [ch[4].astype(BF16) for ch in chains]
        kkts = [_dot_nt(k16, k16) for k16 in k16s]
        qkts = [_dot_nt(ch[5].astype(BF16), k16) for ch, k16 in zip(chains, k16s)]
        decays, ms = [], []
        for (chunk, rows, hl, direction, kc, qc, vc, bc, gc), kkt in zip(chains, kkts):
            lower = direction == 0
            grow = jnp.concatenate([gc, gc], axis=0).T[0:c_sz, 0:c_sz]
            incl = (r >= c) if lower else (r <= c)
            strict = (r > c) if lower else (r < c)
            decay = jnp.where(incl, jnp.exp(jnp.minimum(gc[:, 0:c_sz] - grow, 0.0)), 0.0)
            decays.append(decay)
            ms.append(jnp.where(strict, bc[:, 0:c_sz] * kkt * decay, 0.0))
        tinvs = _unit_tri_inverses(ms, r, c)
        egs = [jnp.exp(ch[8]) for ch in chains]
        uws = []
        for (chunk, rows, hl, direction, kc, qc, vc, bc, gc), tinv, eg in zip(chains, tinvs, egs):
            rhs = jnp.concatenate([vc * bc, kc * bc * eg], axis=1).astype(BF16)
            uws.append(_dot(tinv.astype(BF16), rhs))
        for (chunk, rows, hl, direction, kc, qc, vc, bc, gc), qkt, decay, eg, uw in zip(
                chains, qkts, decays, egs, uws):
            idx = hl * 2 + direction
            g_last = gc[c_sz - 1:c_sz, :] if direction == 0 else gc[0:1, :]
            u_ref[idx, rows, :] = uw[:, 0:hd]
            w_ref[idx, rows, :] = uw[:, hd:2 * hd].astype(BF16)
            a_ref[idx, rows, :] = (qkt * decay).astype(BF16)
            qg_ref[idx, rows, :] = (qc * eg).astype(BF16)
            kt_ref[idx, rows, :] = (kc * jnp.exp(g_last - gc)).astype(BF16)
            el_ref[idx, pl.ds(_aligned(chunk * SUBLANE, SUBLANE), SUBLANE), :] = (
                jnp.broadcast_to(jnp.exp(g_last), (SUBLANE, LANE)))

    for hl in range(hp):
        for direction in range(2):
            s_ref[hl * 2 + direction] = s0_ref[direction, hl]

    def phase_b(step):
        ids, rows, erows = [], [], []
        for hl in range(hp):
            for direction in range(2):
                chunk = step if direction == 0 else n_chunks - 1 - step
                ids.append(hl * 2 + direction)
                rows.append(pl.ds(_aligned(chunk * c_sz, c_sz), c_sz))
                erows.append(pl.ds(_aligned(chunk * SUBLANE, SUBLANE), SUBLANE))
        ss = [s_ref[i] for i in ids]
        s16s = [s.astype(BF16) for s in ss]
        wss = [_dot(w_ref[i, rw, :], s16) for i, rw, s16 in zip(ids, rows, s16s)]
        qss = [_dot(qg_ref[i, rw, :], s16) for i, rw, s16 in zip(ids, rows, s16s)]
        vns = [(u_ref[i, rw, :] - ws).astype(BF16) for i, rw, ws in zip(ids, rows, wss)]
        avs = [_dot(a_ref[i, rw, :], vn) for i, rw, vn in zip(ids, rows, vns)]
        kvs = [_dot_tn(kt_ref[i, rw, :], vn) for i, rw, vn in zip(ids, rows, vns)]
        for i, rw, er, s, qs_, av, kv in zip(ids, rows, erows, ss, qss, avs, kvs):
            od_ref[i, rw, :] = qs_ + av
            s_ref[i] = s * el_ref[i, er, :][0:1, :] + kv

    n_blocks = n_chunks // ca
    phase_a(0)

    def pipelined(it, carry):
        for s in range(ca):
            phase_b((it - 1) * ca + s)
        phase_a(it)
        return carry

    lax.fori_loop(1, n_blocks, pipelined, 0)
    for s in range(ca):
        phase_b((n_blocks - 1) * ca + s)

    for hl in range(hp):
        for direction in range(2):
            sfin_ref[direction, hl] = s_ref[hl * 2 + direction]

    for t in range(seq // rt):
        r0 = t * rt
        for hl in range(hp):
            hs = slice(hl * hd, (hl + 1) * hd)
            o = od_ref[hl * 2, r0:r0 + rt, :] + od_ref[hl * 2 + 1, r0:r0 + rt, :]
            o = o * lax.rsqrt(jnp.mean(o * o, axis=-1, keepdims=True) + EPS) * nw_ref[...]
            o_ref[r0:r0 + rt, hs] = (o * _silu(z_ref[r0:r0 + rt, hs].astype(F32))).astype(o_ref.dtype)


def _deltanet(proj, col, conv_w, norm_w, s0, *, batch, seq, n_heads, col_q, col_z):
    t = proj.shape[0]
    hd = DN_HEAD_DIM
    hp = 2
    d = n_heads * hd
    wd = hp * hd
    n_chunks = seq // DN_CHUNK
    ca = min(8, n_chunks)
    assert n_heads % hp == 0 and col_q % hp == 0 and col_z % hp == 0 and n_chunks % ca == 0
    blk = lambda off: pl.BlockSpec((seq, wd), lambda b, h: (b, off // hp + h))
    cw = lambda off: pl.BlockSpec((3, wd), lambda b, h: (0, off // hp + h))
    st = pl.BlockSpec((None, 2, hp, hd, hd), lambda b, h: (b, 0, h, 0, 0))
    nst = 2 * hp
    return pl.pallas_call(
        functools.partial(_dn_kernel, seq=seq, n_heads=n_heads, hp=hp, ca=ca),
        grid=(batch, n_heads // hp),
        in_specs=[blk(col_q), blk(col_q + n_heads), blk(col_q + 2 * n_heads), blk(col_z),
                  pl.BlockSpec((seq, LANE), lambda b, h: (b, 0)),
                  cw(0), cw(n_heads), cw(2 * n_heads),
                  pl.BlockSpec((1, hd), lambda b, h: (0, 0)),
                  st],
        out_specs=[pl.BlockSpec((seq, wd), lambda b, h: (b, h)), st],
        out_shape=[jax.ShapeDtypeStruct((t, d), BF16),
                   jax.ShapeDtypeStruct(s0.shape, F32)],
        scratch_shapes=[pltpu.VMEM((seq + 2 * SUBLANE, wd), F32),
                        pltpu.VMEM((seq, wd), F32), pltpu.VMEM((seq, wd), F32),
                        pltpu.VMEM((seq, wd), F32),
                        pltpu.VMEM((nst, seq, hd), F32),
                        pltpu.VMEM((nst, seq, hd), BF16),
                        pltpu.VMEM((nst, seq, DN_CHUNK), BF16),
                        pltpu.VMEM((nst, seq, hd), BF16),
                        pltpu.VMEM((nst, seq, hd), BF16),
                        pltpu.VMEM((nst, n_chunks * SUBLANE, LANE), F32),
                        pltpu.VMEM((nst, seq, hd), F32),
                        pltpu.VMEM((nst, hd, hd), F32)],
        compiler_params=_cparams(("parallel", "parallel")),
        name="deltanet",
    )(proj, proj, proj, proj, col, conv_w, conv_w, conv_w, norm_w.reshape(1, hd), s0)


def _conv_module_kernel(a_ref, g_ref, w_ref, b_ref, lnw_ref, lnb_ref, o_ref, yp_ref, tmp_ref,
                        sh_ref, *, seq, taps):
    d = a_ref.shape[1]
    half = taps // 2
    pad = 2 * SUBLANE
    assert half < pad
    rt = CV_ROW_TILE
    ct = CV_COL_TILE
    zeros = jnp.zeros((pad, d), F32)
    yp_ref[0:pad, :] = zeros
    yp_ref[seq + pad:seq + 2 * pad, :] = zeros

    def fill(t, carry):
        r0 = pl.multiple_of(t * rt, rt)
        a = a_ref[pl.ds(r0, rt), :].astype(F32)
        g = g_ref[pl.ds(r0, rt), :].astype(F32)
        yp_ref[pl.ds(r0 + pad, rt), :] = a * jax.nn.sigmoid(g)
        return carry

    lax.fori_loop(0, seq // rt, fill, 0)

    def tile(t, carry):
        r0 = pl.multiple_of(t * rt, rt)
        for cc in range(d // ct):
            cs = slice(cc * ct, (cc + 1) * ct)
            win = yp_ref[pl.ds(r0, rt + 2 * pad), cs]
            span = rt + 2 * pad - SUBLANE
            for m in range(1, SUBLANE):
                sh_ref[m - 1] = win[m:m + span]
            acc = jnp.zeros((rt, ct), F32) + b_ref[:, cs]
            for j in range(taps):
                off = pad - half + j
                lo = (off // SUBLANE) * SUBLANE
                m = off - lo
                if m == 0:
                    rows = yp_ref[pl.ds(pl.multiple_of(r0 + lo, SUBLANE), rt), cs]
                else:
                    rows = sh_ref[m - 1, lo:lo + rt, :]
                acc = acc + w_ref[j:j + 1, cs] * rows
            tmp_ref[:, cs] = acc
        y = tmp_ref[...]
        mu = jnp.mean(y, axis=-1, keepdims=True)
        yc = y - mu
        var = jnp.mean(yc * yc, axis=-1, keepdims=True)
        y = yc * lax.rsqrt(var + EPS) * lnw_ref[...] + lnb_ref[...]
        o_ref[pl.ds(r0, rt), :] = _silu(y).astype(o_ref.dtype)
        return carry

    lax.fori_loop(0, seq // rt, tile, 0)


def _conv_module(proj, w, b, lnw, lnb, *, batch, seq, col_a):
    t = proj.shape[0]
    taps, d = w.shape
    return pl.pallas_call(
        functools.partial(_conv_module_kernel, seq=seq, taps=taps),
        grid=(batch,),
        in_specs=[pl.BlockSpec((seq, d), lambda i: (i, col_a)),
                  pl.BlockSpec((seq, d), lambda i: (i, col_a + 1)),
                  pl.BlockSpec((taps, d), lambda i: (0, 0)),
                  pl.BlockSpec((1, d), lambda i: (0, 0)),
                  pl.BlockSpec((1, d), lambda i: (0, 0)),
                  pl.BlockSpec((1, d), lambda i: (0, 0))],
        out_specs=pl.BlockSpec((seq, d), lambda i: (i, 0)),
        out_shape=jax.ShapeDtypeStruct((t, d), BF16),
        scratch_shapes=[pltpu.VMEM((seq + 4 * SUBLANE, d), F32),
                        pltpu.VMEM((CV_ROW_TILE, d), F32),
                        pltpu.VMEM((SUBLANE - 1, CV_ROW_TILE + 3 * SUBLANE, CV_COL_TILE), F32)],
        compiler_params=_cparams(("parallel",)),
        name="conv_module",
    )(proj, proj, w, b.reshape(1, d), lnw.reshape(1, d), lnb.reshape(1, d))


def _merge_kernel(s5_ref, dn_ref, cv_ref, g0_ref, g1_ref, g2_ref, w0_ref, w1_ref, w2_ref,
                  wo_ref, x_ref, gate_ref, o_ref):
    merged = (jax.nn.sigmoid(g0_ref[...].astype(F32)) * _dot(s5_ref[...], w0_ref[...])
              + jax.nn.sigmoid(g1_ref[...].astype(F32)) * _dot(dn_ref[...], w1_ref[...])
              + jax.nn.sigmoid(g2_ref[...].astype(F32)) * _dot(cv_ref[...], w2_ref[...]))
    y = _dot(merged.astype(BF16), wo_ref[...])
    o_ref[...] = x_ref[...] + gate_ref[0] * y


def _merge(s5_out, dn_out, cv_out, proj, w_s5, w_dn, w_cv, w_out, x2, mod3, gate_idx,
           *, rows_per_batch, fixed_row, col_gate):
    t, d = x2.shape
    tm = min(512, t if fixed_row is not None else rows_per_batch)
    assert t % tm == 0
    row = _row_of_tile(tm, rows_per_batch, fixed_row)
    act = pl.BlockSpec((tm, d), lambda i: (i, 0))
    gat = lambda k: pl.BlockSpec((tm, d), lambda i: (i, col_gate + k))
    wsp = pl.BlockSpec((d, d), lambda i: (0, 0))
    return pl.pallas_call(
        _merge_kernel,
        grid=(t // tm,),
        in_specs=[act, act, act, gat(0), gat(1), gat(2), wsp, wsp, wsp, wsp, act,
                  pl.BlockSpec((None, 1, d), lambda i: (row(i), 0, gate_idx))],
        out_specs=act,
        out_shape=jax.ShapeDtypeStruct((t, d), F32),
        compiler_params=_cparams(("parallel",)),
        name="merge",
    )(s5_out, dn_out, cv_out, proj, proj, proj, w_s5, w_dn, w_cv, w_out, x2, mod3)


def _ffn_conv_kernel(a_ref, v_ref, w_ref, b_ref, o_ref, a0_ref, al_ref, ar_ref, *, seq, grid2d):
    ct = a_ref.shape[1]
    pad = 72 if grid2d else SUBLANE
    rt = 128
    zeros = jnp.zeros((pad, ct), F32)
    for buf in (a0_ref, al_ref, ar_ref):
        buf[0:pad, :] = zeros
        buf[seq + pad:seq + 2 * pad, :] = zeros

    def fill(t, carry):
        r0 = pl.multiple_of(t * rt, rt)
        a0_ref[pl.ds(r0 + pad, rt), :] = a_ref[pl.ds(r0, rt), :].astype(F32)
        return carry

    lax.fori_loop(0, seq // rt, fill, 0)

    def neighbours(t, carry):
        r0 = pl.multiple_of(t * rt, rt)
        left = _shifted_rows(a0_ref, r0 + pad, -1, rt, slice(None))
        right = _shifted_rows(a0_ref, r0 + pad, 1, rt, slice(None))
        if grid2d:
            colpos = (_iota((rt, ct), 0) + r0) & (GRID_W - 1)
            left = jnp.where(colpos == 0, 0.0, left)
            right = jnp.where(colpos == GRID_W - 1, 0.0, right)
        al_ref[pl.ds(r0 + pad, rt), :] = left
        ar_ref[pl.ds(r0 + pad, rt), :] = right
        return carry

    lax.fori_loop(0, seq // rt, neighbours, 0)

    if grid2d:
        taps = [(dr, dc) for dr in (-1, 0, 1) for dc in (-1, 0, 1)]
    else:
        taps = [(0, dc) for dc in (-1, 0, 1)]
    src = {-1: al_ref, 0: a0_ref, 1: ar_ref}

    def tile(t, carry):
        r0 = pl.multiple_of(t * rt, rt)
        acc = jnp.zeros((rt, ct), F32) + b_ref[...]
        for dr, dc in taps:
            widx = (dr + 1) * 3 + (dc + 1)
            rows = pl.ds(pl.multiple_of(r0 + pad + dr * GRID_W, SUBLANE), rt)
            acc = acc + w_ref[widx:widx + 1, :] * src[dc][rows, :]
        o_ref[pl.ds(r0, rt), :] = (_silu(acc) * v_ref[pl.ds(r0, rt), :].astype(F32)).astype(o_ref.dtype)
        return carry

    lax.fori_loop(0, seq // rt, tile, 0)


def _ffn_conv(av, w9, bias, *, batch, seq, grid2d):
    t = av.shape[0]
    f = w9.shape[1]
    ct = 256
    assert f % ct == 0 and seq % 128 == 0
    nct = f // ct
    pad = 72 if grid2d else SUBLANE
    return pl.pallas_call(
        functools.partial(_ffn_conv_kernel, seq=seq, grid2d=grid2d),
        grid=(batch, nct),
        in_specs=[pl.BlockSpec((seq, ct), lambda b, c: (b, c)),
                  pl.BlockSpec((seq, ct), lambda b, c: (b, nct + c)),
                  pl.BlockSpec((9, ct), lambda b, c: (0, c)),
                  pl.BlockSpec((1, ct), lambda b, c: (0, c))],
        out_specs=pl.BlockSpec((seq, ct), lambda b, c: (b, c)),
        out_shape=jax.ShapeDtypeStruct((t, f), BF16),
        scratch_shapes=[pltpu.VMEM((seq + 2 * pad, ct), F32)] * 3,
        compiler_params=_cparams(("parallel", "parallel")),
        name="ffn_conv",
    )(av, av, w9, bias.reshape(1, f))


def _resid_matmul_kernel(*refs, final_norm):
    if final_norm:
        a_ref, w_ref, x_ref, gate_ref, fw_ref, o_ref = refs
    else:
        a_ref, w_ref, x_ref, gate_ref, o_ref = refs
    y = x_ref[...] + gate_ref[0] * _dot(a_ref[...], w_ref[...])
    if final_norm:
        y = y * lax.rsqrt(jnp.mean(y * y, axis=-1, keepdims=True) + EPS) * fw_ref[...]
    o_ref[...] = y


def _resid_matmul(a, w, x2, mod3, gate_idx, *, rows_per_batch, fixed_row, final_w=None):
    t, d = x2.shape
    k = a.shape[1]
    tm = min(512, t if fixed_row is not None else rows_per_batch)
    assert t % tm == 0
    row = _row_of_tile(tm, rows_per_batch, fixed_row)
    final_norm = final_w is not None
    in_specs = [pl.BlockSpec((tm, k), lambda i: (i, 0)),
                pl.BlockSpec((k, d), lambda i: (0, 0)),
                pl.BlockSpec((tm, d), lambda i: (i, 0)),
                pl.BlockSpec((None, 1, d), lambda i: (row(i), 0, gate_idx))]
    args = [a, w, x2, mod3]
    if final_norm:
        in_specs.append(pl.BlockSpec((1, d), lambda i: (0, 0)))
        args.append(final_w.reshape(1, d))
    return pl.pallas_call(
        functools.partial(_resid_matmul_kernel, final_norm=final_norm),
        grid=(t // tm,),
        in_specs=in_specs,
        out_specs=pl.BlockSpec((tm, d), lambda i: (i, 0)),
        out_shape=jax.ShapeDtypeStruct((t, d), F32),
        compiler_params=_cparams(("parallel",)),
        name="resid_matmul",
    )(*args)


def _group_major(u2, batch, seq, groups):
    nk = seq // S5_CHUNK
    u5 = u2.reshape(batch, nk, S5_CHUNK, groups, S5_GROUP)
    return jnp.transpose(u5, (3, 1, 0, 2, 4)).reshape(groups, nk * batch, S5_CHUNK * S5_GROUP)


def _token_major(yg, batch, seq, groups):
    nk = seq // S5_CHUNK
    y5 = yg.reshape(groups, nk, batch, S5_CHUNK, S5_GROUP)
    return jnp.transpose(y5, (2, 1, 3, 0, 4)).reshape(batch * seq, groups * S5_GROUP)


def kernel(x, c, ctx, c_ctx, ada_w, ada_b, norm1_w, norm2_w, w_in, s5_a_re, s5_a_im, s5_log_dt, s5_b_re, s5_b_im, s5_c_re, s5_c_im, s5_d, s5_w_glu, dn_conv_w, dn_a_log, dn_dt_bias, dn_norm_w, cv_dw_w, cv_dw_b, cv_ln_w, cv_ln_b, w_br_s5, w_br_dn, w_br_cv, w_out, ffn_w_up, ffn_dw_w, ffn_dw_b, ffn_w_down, final_norm_w):
    batch, seq, d = x.shape
    ctx_len = ctx.shape[1]
    depth = ada_w.shape[0]
    groups = s5_a_re.shape[2]
    n_heads = dn_a_log.shape[2]
    dn_width = n_heads * DN_HEAD_DIM
    ffn_hidden = ffn_dw_b.shape[1]
    assert batch == SUBLANE and batch < MOD_ROWS and d == groups * S5_GROUP == dn_width
    assert d % 1024 == 0 and 4 * n_heads <= LANE

    col_qkv = d
    col_beta = col_qkv + 3 * dn_width
    n_state = col_beta + 4 * n_heads
    col_cv = n_state + dn_width
    col_gate = col_cv + 2 * d
    pk_q = d // LANE
    pk_z = (d + 3 * dn_width) // LANE
    pk_cv = (d + 4 * dn_width) // d
    pk_gate = pk_cv + 2

    cmat = jnp.zeros((MOD_ROWS, d), F32).at[:batch].set(c).at[batch].set(c_ctx)
    xl = x.reshape(batch * seq, d)
    xc = ctx.reshape(batch * ctx_len, d)

    for i in range(depth):
        last = i == depth - 1
        w_main = jnp.concatenate([w_in[i][:, :col_beta].astype(BF16),
                                  w_in[i][:, n_state:].astype(BF16)], axis=1)
        w_bd = jnp.pad(w_in[i][:, col_beta:n_state].astype(BF16),
                       ((0, 0), (0, LANE - 4 * n_heads)))
        aneg = jnp.pad(-jnp.exp(dn_a_log[i].reshape(1, -1)), ((0, 0), (2 * n_heads, LANE - 4 * n_heads)))
        dtb = jnp.pad(dn_dt_bias[i].reshape(1, -1), ((0, 0), (2 * n_heads, LANE - 4 * n_heads)))
        s5_par = _s5_params(s5_a_re[i], s5_a_im[i], s5_log_dt[i], s5_b_re[i], s5_b_im[i],
                            s5_c_re[i], s5_c_im[i])
        w_glu = s5_w_glu[i].astype(BF16)
        w_brs = w_br_s5[i].astype(BF16)
        w_brd = w_br_dn[i].astype(BF16)
        w_brc = w_br_cv[i].astype(BF16)
        w_o = w_out[i].astype(BF16)
        w_up = ffn_w_up[i].astype(BF16)
        w_dn = ffn_w_down[i].astype(BF16)
        w9 = ffn_dw_w[i].reshape(9, ffn_hidden)

        mod3 = _modulation(cmat, ada_w, ada_b, i).reshape(MOD_ROWS, 1, N_MOD * d)

        rows_c = dict(rows_per_batch=ctx_len, fixed_row=batch)
        rows_l = dict(rows_per_batch=seq, fixed_row=None)
        proj_c, bd_c = _norm_matmul(xc, norm1_w[i], mod3, 0, 1, w_main, wbd=w_bd, tn=2 * d,
                                    **rows_c)
        proj_l, bd_l = _norm_matmul(xl, norm1_w[i], mod3, 0, 1, w_main, wbd=w_bd, tn=2 * d,
                                    **rows_l)

        u_rows = jnp.concatenate([_group_major(proj_c[:, :d], batch, ctx_len, groups),
                                  _group_major(proj_l[:, :d], batch, seq, groups)], axis=1)
        rows_ctx = (ctx_len // S5_CHUNK) * batch
        segments = ((0, ctx_len // S5_CHUNK), (rows_ctx, seq // S5_CHUNK))
        y_rows = _s5_core(u_rows, *s5_par, segments=segments, batch=batch)

        zero_dn = jnp.zeros((batch, 2, n_heads, DN_HEAD_DIM, DN_HEAD_DIM), F32)
        dn_args = dict(batch=batch, n_heads=n_heads, col_q=pk_q, col_z=pk_z)
        dn_c, dn_state = _deltanet(proj_c, _dn_gates(bd_c, aneg, dtb, n_heads), dn_conv_w[i],
                                   dn_norm_w[i], zero_dn, seq=ctx_len, **dn_args)
        dn_l, _ = _deltanet(proj_l, _dn_gates(bd_l, aneg, dtb, n_heads), dn_conv_w[i],
                            dn_norm_w[i], dn_state, seq=seq, **dn_args)

        def finish(x2, proj, y, dn_out, length, rows, grid2d, final_w):
            s5_out = _s5_post(y, proj, s5_d[i], w_glu)
            cv_out = _conv_module(proj, cv_dw_w[i], cv_dw_b[i], cv_ln_w[i], cv_ln_b[i],
                                  batch=batch, seq=length, col_a=pk_cv)
            x2 = _merge(s5_out, dn_out, cv_out, proj, w_brs, w_brd, w_brc, w_o, x2, mod3, 2,
                        col_gate=pk_gate, **rows)
            av = _norm_matmul(x2, norm2_w[i], mod3, 3, 4, w_up, tn=w_up.shape[1] // 2, **rows)
            hidden = _ffn_conv(av, w9, ffn_dw_b[i], batch=batch, seq=length, grid2d=grid2d)
            return _resid_matmul(hidden, w_dn, x2, mod3, 5, final_w=final_w, **rows)

        if not last:
            xc = finish(xc, proj_c, _token_major(y_rows[:, :rows_ctx], batch, ctx_len, groups),
                        dn_c, ctx_len, rows_c, False, None)
        xl = finish(xl, proj_l, _token_major(y_rows[:, rows_ctx:], batch, seq, groups), dn_l, seq,
                    rows_l, True, final_norm_w if last else None)

    return xl.reshape(batch, seq, d)
```

```python
import functools
import math

import jax
import jax.numpy as jnp
from jax import lax
from jax.experimental import pallas as pl
from jax.experimental.pallas import tpu as pltpu

F32 = jnp.float32
BF16 = jnp.bfloat16
HIGHEST = lax.Precision.HIGHEST

EPS = 1e-6
GRID_W = 64
S5_GROUP = 16
DN_HEAD_DIM = 128
N_MOD = 6
N_BRANCH = 3

LANE = 128
SUBLANE = 8
S5_CHUNK = 16
DN_CHUNK = 64
MOD_ROWS = 16
CV_ROW_TILE = 64
CV_COL_TILE = 256
VMEM_LIMIT = 56 * 1024 * 1024


def _cparams(sem, vmem=VMEM_LIMIT):
    return pltpu.CompilerParams(dimension_semantics=sem, vmem_limit_bytes=vmem)


def _dot(a, b, precision=None):
    return jnp.dot(a, b, preferred_element_type=F32, precision=precision)


def _dot_nt(a, b, precision=None):
    return lax.dot_general(a, b, (((1,), (1,)), ((), ())), preferred_element_type=F32,
                           precision=precision)


def _dot_tn(a, b, precision=None):
    return lax.dot_general(a, b, (((0,), (0,)), ((), ())), preferred_element_type=F32,
                           precision=precision)


def _silu(x):
    return x * jax.nn.sigmoid(x)


def _iota(shape, dim):
    return lax.broadcasted_iota(jnp.int32, shape, dim)


def _aligned(index, multiple):
    return index if isinstance(index, int) else pl.multiple_of(index, multiple)


def _shifted_rows(ref, base, off, rows, cols):
    lo = (off // SUBLANE) * SUBLANE
    shift = off - lo
    start = pl.multiple_of(base + lo, SUBLANE)
    if shift == 0:
        return ref[pl.ds(start, rows), cols]
    return ref[pl.ds(start, rows + SUBLANE), cols][shift:shift + rows]


def _mod_kernel(c_ref, w_ref, b_ref, o_ref):
    o_ref[...] = _dot(_silu(c_ref[...]), w_ref[...], HIGHEST) + b_ref[...]


def _modulation(cmat, ada_w, ada_b, layer):
    rows, d = cmat.shape
    depth, _, n = ada_w.shape
    tn = 1024
    return pl.pallas_call(
        _mod_kernel,
        grid=(n // tn,),
        in_specs=[pl.BlockSpec((rows, d), lambda j: (0, 0)),
                  pl.BlockSpec((None, d, tn), lambda j: (layer, 0, j)),
                  pl.BlockSpec((None, 1, tn), lambda j: (layer, 0, j))],
        out_specs=pl.BlockSpec((rows, tn), lambda j: (0, j)),
        out_shape=jax.ShapeDtypeStruct((rows, n), F32),
        compiler_params=_cparams(("parallel",)),
        name="modulation",
    )(cmat, ada_w, ada_b.reshape(depth, 1, n))


def _norm_matmul_kernel(*refs, with_bd):
    if with_bd:
        x_ref, nw_ref, sh_ref, sc_ref, w_ref, wbd_ref, o_ref, bd_ref, h_ref = refs
    else:
        x_ref, nw_ref, sh_ref, sc_ref, w_ref, o_ref, h_ref = refs

    @pl.when(pl.program_id(1) == 0)
    def _():
        x = x_ref[...]
        h = x * lax.rsqrt(jnp.mean(x * x, axis=-1, keepdims=True) + EPS) * nw_ref[...]
        h = h * (1.0 + sc_ref[0]) + sh_ref[0]
        h16 = h.astype(BF16)
        h_ref[...] = h16
        if with_bd:
            bd_ref[...] = _dot(h16, wbd_ref[...])

    o_ref[...] = _dot(h_ref[...], w_ref[...]).astype(o_ref.dtype)


def _row_of_tile(tm, rows_per_batch, fixed_row):
    if fixed_row is not None:
        return lambda i: fixed_row
    return lambda i: (i * tm) // rows_per_batch


def _norm_matmul(x2, norm_w, mod3, sh_idx, sc_idx, w, *, rows_per_batch, fixed_row, wbd=None,
                 tn):
    t, d = x2.shape
    n = w.shape[1]
    tm = min(1024, t if fixed_row is not None else rows_per_batch)
    assert t % tm == 0 and n % tn == 0
    row = _row_of_tile(tm, rows_per_batch, fixed_row)
    with_bd = wbd is not None
    in_specs = [pl.BlockSpec((tm, d), lambda i, j: (i, 0)),
                pl.BlockSpec((1, d), lambda i, j: (0, 0)),
                pl.BlockSpec((None, 1, d), lambda i, j: (row(i), 0, sh_idx)),
                pl.BlockSpec((None, 1, d), lambda i, j: (row(i), 0, sc_idx)),
                pl.BlockSpec((d, tn), lambda i, j: (0, j))]
    args = [x2, norm_w.reshape(1, d), mod3, mod3, w]
    out_specs = [pl.BlockSpec((tm, tn), lambda i, j: (i, j))]
    out_shape = [jax.ShapeDtypeStruct((t, n), BF16)]
    if with_bd:
        in_specs.append(pl.BlockSpec((d, LANE), lambda i, j: (0, 0)))
        args.append(wbd)
        out_specs.append(pl.BlockSpec((tm, LANE), lambda i, j: (i, 0)))
        out_shape.append(jax.ShapeDtypeStruct((t, LANE), F32))
    res = pl.pallas_call(
        functools.partial(_norm_matmul_kernel, with_bd=with_bd),
        grid=(t // tm, n // tn),
        in_specs=in_specs,
        out_specs=out_specs,
        out_shape=out_shape,
        scratch_shapes=[pltpu.VMEM((tm, d), BF16)],
        compiler_params=_cparams(("parallel", "arbitrary")),
        name="norm_matmul",
    )(*args)
    return res if with_bd else res[0]


def _dn_gates_kernel(bd_ref, aneg_ref, dtb_ref, o_ref, *, n_heads):
    x = bd_ref[...]
    tm = x.shape[0]
    y = x + dtb_ref[...]
    g = aneg_ref[...] * (jnp.maximum(y, 0.0) + jnp.log1p(jnp.exp(-jnp.abs(y))))
    blk = 2 * DN_CHUNK
    r = _iota((blk, blk), 0)
    c = _iota((blk, blk), 1)
    shift = DN_CHUNK.bit_length() - 1
    same = (r >> shift) == (c >> shift)
    tri_lo = jnp.where(same & (c <= r), 1.0, 0.0)
    tri_hi = jnp.where(same & (c >= r), 1.0, 0.0)
    parts = [g[t * blk:(t + 1) * blk] for t in range(tm // blk)]
    prefix = jnp.concatenate([_dot(tri_lo, p, HIGHEST) for p in parts], axis=0)
    suffix = jnp.concatenate([_dot(tri_hi, p, HIGHEST) for p in parts], axis=0)
    lane = _iota(x.shape, 1)
    o_ref[...] = jnp.where(lane < 2 * n_heads, jax.nn.sigmoid(x),
                           jnp.where(lane < 3 * n_heads, prefix, suffix))


def _dn_gates(bd, aneg, dtb, n_heads):
    t = bd.shape[0]
    tm = min(512, t)
    assert t % tm == 0 and tm % (2 * DN_CHUNK) == 0
    return pl.pallas_call(
        functools.partial(_dn_gates_kernel, n_heads=n_heads),
        grid=(t // tm,),
        in_specs=[pl.BlockSpec((tm, LANE), lambda i: (i, 0)),
                  pl.BlockSpec((1, LANE), lambda i: (0, 0)),
                  pl.BlockSpec((1, LANE), lambda i: (0, 0))],
        out_specs=pl.BlockSpec((tm, LANE), lambda i: (i, 0)),
        out_shape=jax.ShapeDtypeStruct((t, LANE), F32),
        compiler_params=_cparams(("parallel",)),
        name="dn_gates",
    )(bd, aneg, dtb)


def _s5_kernel(u_ref, bt_ref, c_ref, pw_ref, a_ref, y_ref, s_ref, hp_ref, *, segments, batch):
    tc, cg = S5_CHUNK, S5_GROUP
    half = LANE // 2
    n_grp = u_ref.shape[0]

    def operators(g):
        btr = [bt_ref[g, d, 0] for d in range(2)]
        bti = [bt_ref[g, d, 1] for d in range(2)]
        cr, ci = c_ref[g, 0], c_ref[g, 1]
        pr = [pw_ref[g, d, 0] for d in range(2)]
        pi = [pw_ref[g, d, 1] for d in range(2)]
        l_re, l_im = [], []
        for m in range(2 * tc):
            tau = tc - 1 - m
            d = 0 if tau >= 0 else 1
            if tau == 0:
                l_re.append(btr[0] + btr[1])
                l_im.append(bti[0] + bti[1])
            elif tau == -tc:
                l_re.append(jnp.zeros_like(btr[0]))
                l_im.append(jnp.zeros_like(btr[0]))
            else:
                p_r, p_i = pr[d][abs(tau):abs(tau) + 1, :], pi[d][abs(tau):abs(tau) + 1, :]
                l_re.append(p_r * btr[d] - p_i * bti[d])
                l_im.append(p_r * bti[d] + p_i * btr[d])
        l_re = jnp.concatenate(l_re, axis=0)
        l_im = jnp.concatenate(l_im, axis=0)
        kxt = _dot_nt(cr, l_re, HIGHEST) - _dot_nt(ci, l_im, HIGHEST)
        wt_t = jnp.concatenate([kxt[:, cg * (tc - 1 - j):cg * (tc - 1 - j) + tc * cg]
                                for j in range(tc)], axis=0)
        win_rows, wst_rows = [], []
        for i in range(tc):
            pf_r, pf_i = pr[0][tc - 1 - i:tc - i, :], pi[0][tc - 1 - i:tc - i, :]
            pb_r, pb_i = pr[1][i:i + 1, :], pi[1][i:i + 1, :]
            win_rows.append(jnp.concatenate([
                pf_r * btr[0] - pf_i * bti[0], pb_r * btr[1] - pb_i * bti[1],
                pf_r * bti[0] + pf_i * btr[0], pb_r * bti[1] + pb_i * btr[1]], axis=1))
            qf_r, qf_i = pr[0][i + 1:i + 2, :], pi[0][i + 1:i + 2, :]
            qb_r, qb_i = pr[1][tc - i:tc - i + 1, :], pi[1][tc - i:tc - i + 1, :]
            wst_rows.append(jnp.concatenate([
                cr * qf_r - ci * qf_i, cr * qb_r - ci * qb_i,
                -(cr * qf_i + ci * qf_r), -(cr * qb_i + ci * qb_r)], axis=1))
        win = jnp.concatenate(win_rows, axis=0)
        wst_t = jnp.concatenate(wst_rows, axis=0)
        return wt_t.astype(BF16), win.astype(BF16), wst_t.astype(BF16)

    ops = [operators(g) for g in range(n_grp)]
    for g in range(n_grp):
        s_ref[g] = _dot(u_ref[g], ops[g][1])
    ars = [a_ref[g, 0:1, :] for g in range(n_grp)]
    ais = [a_ref[g, 1:2, :] for g in range(n_grp)]
    is_fwd = _iota((batch, LANE), 1) < half
    carry = tuple(jnp.zeros((batch, LANE), F32) for _ in range(2 * n_grp))
    for row0, n_chunks in segments:
        def body(s, carry, row0=row0, n_chunks=n_chunks):
            rf = pl.ds(pl.multiple_of(row0 + s * batch, batch), batch)
            rb = pl.ds(pl.multiple_of(row0 + (n_chunks - 1 - s) * batch, batch), batch)
            new = []
            for g in range(n_grp):
                hre, him = carry[2 * g], carry[2 * g + 1]
                hp_ref[g, rf, 0:half] = hre[:, 0:half]
                hp_ref[g, rb, half:LANE] = hre[:, half:LANE]
                hp_ref[g, rf, LANE:LANE + half] = him[:, 0:half]
                hp_ref[g, rb, LANE + half:2 * LANE] = him[:, half:LANE]
                sf = s_ref[g, rf, :]
                sb = s_ref[g, rb, :]
                sre = jnp.where(is_fwd, sf[:, 0:LANE], sb[:, 0:LANE])
                sim = jnp.where(is_fwd, sf[:, LANE:2 * LANE], sb[:, LANE:2 * LANE])
                new.append(ars[g] * hre - ais[g] * him + sre)
                new.append(ars[g] * him + ais[g] * hre + sim)
            return tuple(new)

        carry = lax.fori_loop(0, n_chunks, body, carry)
    for g in range(n_grp):
        y = _dot_nt(u_ref[g], ops[g][0]) + _dot_nt(hp_ref[g].astype(BF16), ops[g][2])
        y_ref[g] = y.astype(y_ref.dtype)


S5_GROUPS_PER_STEP = 4


def _s5_core(ug, bt, cc, pw, avec, *, segments, batch):
    g, rows, k = ug.shape
    gb = S5_GROUPS_PER_STEP
    assert batch == SUBLANE and k == 2 * LANE and g % gb == 0
    act = pl.BlockSpec((gb, rows, k), lambda i: (i, 0, 0))
    per_group = lambda a: pl.BlockSpec((gb,) + a.shape[1:],
                                       lambda i: (i,) + (0,) * (a.ndim - 1))
    return pl.pallas_call(
        functools.partial(_s5_kernel, segments=segments, batch=batch),
        grid=(g // gb,),
        in_specs=[act, per_group(bt), per_group(cc), per_group(pw), per_group(avec)],
        out_specs=act,
        out_shape=jax.ShapeDtypeStruct((g, rows, k), BF16),
        scratch_shapes=[pltpu.VMEM((gb, rows, k), F32), pltpu.VMEM((gb, rows, k), F32)],
        compiler_params=_cparams(("parallel",)),
        name="s5_core",
    )(ug, bt, cc, pw, avec)


def _s5_params(a_re, a_im, log_dt, b_re, b_im, c_re, c_im):
    tc = S5_CHUNK
    dt = jnp.exp(log_dt)[..., None]
    mag = jnp.exp(a_re * dt)
    abr, abi = mag * jnp.cos(a_im * dt), mag * jnp.sin(a_im * dt)
    den = a_re * a_re + a_im * a_im
    cr = ((abr - 1.0) * a_re + abi * a_im) / den
    ci = (abi * a_re - (abr - 1.0) * a_im) / den
    bbr = cr[..., None] * b_re - ci[..., None] * b_im
    bbi = cr[..., None] * b_im + ci[..., None] * b_re
    n = jnp.arange(tc + 1, dtype=F32)[:, None, None, None]
    pmag = jnp.exp(a_re * dt * n)
    pr, pi = pmag * jnp.cos(a_im * dt * n), pmag * jnp.sin(a_im * dt * n)
    bt = jnp.stack([jnp.swapaxes(bbr, 2, 3), jnp.swapaxes(bbi, 2, 3)], axis=2)
    bt = jnp.transpose(bt, (1, 0, 2, 3, 4))
    cc = jnp.stack([c_re, c_im], axis=1)
    pw = jnp.transpose(jnp.stack([pr, pi], axis=0), (3, 2, 0, 1, 4))
    pw = jnp.pad(pw, ((0, 0), (0, 0), (0, 0), (0, SUBLANE - 1), (0, 0)))
    avec = jnp.stack([jnp.concatenate([pr[tc, 0], pr[tc, 1]], axis=-1),
                      jnp.concatenate([pi[tc, 0], pi[tc, 1]], axis=-1)], axis=1)
    return bt, cc, pw, avec


def _s5_post_kernel(y_ref, u_ref, d_ref, w_ref, o_ref):
    y = y_ref[...].astype(F32) + d_ref[...] * u_ref[...].astype(F32)
    k0 = math.sqrt(2.0 / math.pi)
    g = 0.5 * y * (1.0 + jnp.tanh(k0 * (y + 0.044715 * (y * y * y))))
    o_ref[...] = (g * jax.nn.sigmoid(_dot(g.astype(BF16), w_ref[...]))).astype(o_ref.dtype)


def _s5_post(y, proj, s5_d, w_glu):
    t, d = y.shape
    tm = min(512, t)
    return pl.pallas_call(
        _s5_post_kernel,
        grid=(t // tm,),
        in_specs=[pl.BlockSpec((tm, d), lambda i: (i, 0)),
                  pl.BlockSpec((tm, d), lambda i: (i, 0)),
                  pl.BlockSpec((1, d), lambda i: (0, 0)),
                  pl.BlockSpec((d, d), lambda i: (0, 0))],
        out_specs=pl.BlockSpec((tm, d), lambda i: (i, 0)),
        out_shape=jax.ShapeDtypeStruct((t, d), BF16),
        compiler_params=_cparams(("parallel",)),
        name="s5_post",
    )(y, proj, s5_d.reshape(1, d), w_glu)


def _unit_tri_inverses(ms, r, c):
    mm = lambda a, b: _dot(a.astype(BF16), b.astype(BF16))
    eye = jnp.where(r == c, 1.0, 0.0)
    blk = lambda s: (r >> s) == (c >> s)
    mds = [jnp.where(blk(3), m, 0.0) for m in ms]
    m2s = [mm(md, md) for md in mds]
    m4s = [mm(m2, m2) for m2 in m2s]
    ts = [eye - md for md in mds]
    ts = [t + mm(t, m2) for t, m2 in zip(ts, m2s)]
    ts = [t + mm(t, m4) for t, m4 in zip(ts, m4s)]
    for s in (3, 4, 5):
        off = blk(s + 1) & jnp.logical_not(blk(s))
        xs = [mm(jnp.where(off, m, 0.0), t) for m, t in zip(ms, ts)]
        ts = [t - mm(t, x) for t, x in zip(ts, xs)]
    return ts


def _dn_kernel(q_ref, k_ref, v_ref, z_ref, col_ref, cwq_ref, cwk_ref, cwv_ref, nw_ref, s0_ref,
               o_ref, sfin_ref, xp_ref, qs_ref, ks_ref, vs_ref, u_ref, w_ref, a_ref, qg_ref,
               kt_ref, el_ref, od_ref, s_ref, *, seq, n_heads, hp, ca):
    c_sz = DN_CHUNK
    hd = DN_HEAD_DIM
    pad = SUBLANE
    rt = min(256, seq)
    n_chunks = seq // c_sz
    head0 = pl.program_id(1) * hp

    def prep(x_ref, cw_ref, dst_ref, normalise, scale):
        zeros = jnp.zeros((pad, hp * hd), F32)
        xp_ref[0:pad, :] = zeros
        xp_ref[seq + pad:seq + 2 * pad, :] = zeros
        xp_ref[pad:seq + pad, :] = x_ref[...].astype(F32)
        w = cw_ref[...]
        for t in range(seq // rt):
            r0 = t * rt
            acc = (w[0:1, :] * xp_ref[r0 + pad - 1:r0 + pad - 1 + rt, :]
                   + w[1:2, :] * xp_ref[r0 + pad:r0 + pad + rt, :]
                   + w[2:3, :] * xp_ref[r0 + pad + 1:r0 + pad + 1 + rt, :])
            y = _silu(acc)
            if normalise:
                parts = []
                for hl in range(hp):
                    yh = y[:, hl * hd:(hl + 1) * hd]
                    parts.append(yh * (lax.rsqrt(jnp.sum(yh * yh, axis=-1, keepdims=True) + EPS)
                                       * scale))
                y = jnp.concatenate(parts, axis=1)
            dst_ref[r0:r0 + rt, :] = y

    prep(q_ref, cwq_ref, qs_ref, True, hd ** -0.5)
    prep(k_ref, cwk_ref, ks_ref, True, 1.0)
    prep(v_ref, cwv_ref, vs_ref, False, 1.0)

    r = _iota((c_sz, c_sz), 0)
    c = _iota((c_sz, c_sz), 1)
    lane = _iota((c_sz, LANE), 1)

    def pick(col, idx):
        v = jnp.sum(jnp.where(lane == idx, col, 0.0), axis=-1, keepdims=True)
        return jnp.broadcast_to(v, (c_sz, LANE))

    def phase_a(it):
        chains = []
        for cc in range(ca):
            for direction in range(2):
                step = it * ca + cc
                chunk = step if direction == 0 else n_chunks - 1 - step
                rows = pl.ds(_aligned(chunk * c_sz, c_sz), c_sz)
                col = col_ref[rows, :]
                for hl in range(hp):
                    hs = slice(hl * hd, (hl + 1) * hd)
                    chains.append((chunk, rows, hl, direction, ks_ref[rows, hs], qs_ref[rows, hs],
                                   vs_ref[rows, hs],
                                   pick(col, head0 + hl + direction * n_heads),
                                   pick(col, head0 + hl + (2 + direction) * n_heads)))
        k16s = [ch[4].astype(BF16) for ch in chains]
        kkts = [_dot_nt(k16, k16) for k16 in k16s]
        qkts = [_dot_nt(ch[5].astype(BF16), k16) for ch, k16 in zip(chains, k16s)]
        decays, ms = [], []
        for (chunk, rows, hl, direction, kc, qc, vc, bc, gc), kkt in zip(chains, kkts):
            lower = direction == 0
            grow = jnp.concatenate([gc, gc], axis=0).T[0:c_sz, 0:c_sz]
            incl = (r >= c) if lower else (r <= c)
            strict = (r > c) if lower else (r < c)
            decay = jnp.where(incl, jnp.exp(jnp.minimum(gc[:, 0:c_sz] - grow, 0.0)), 0.0)
            decays.append(decay)
            ms.append(jnp.where(strict, bc[:, 0:c_sz] * kkt * decay, 0.0))
        tinvs = _unit_tri_inverses(ms, r, c)
        egs = [jnp.exp(ch[8]) for ch in chains]
        uws = []
        for (chunk, rows, hl, direction, kc, qc, vc, bc, gc), tinv, eg in zip(chains, tinvs, egs):
            rhs = jnp.concatenate([vc * bc, kc * bc * eg], axis=1).astype(BF16)
            uws.append(_dot(tinv.astype(BF16), rhs))
        for (chunk, rows, hl, direction, kc, qc, vc, bc, gc), qkt, decay, eg, uw in zip(
                chains, qkts, decays, egs, uws):
            idx = hl * 2 + direction
            g_last = gc[c_sz - 1:c_sz, :] if direction == 0 else gc[0:1, :]
            u_ref[idx, rows, :] = uw[:, 0:hd]
            w_ref[idx, rows, :] = uw[:, hd:2 * hd].astype(BF16)
            a_ref[idx, rows, :] = (qkt * decay).astype(BF16)
            qg_ref[idx, rows, :] = (qc * eg).astype(BF16)
            kt_ref[idx, rows, :] = (kc * jnp.exp(g_last - gc)).astype(BF16)
            el_ref[idx, pl.ds(_aligned(chunk * SUBLANE, SUBLANE), SUBLANE), :] = (
                jnp.broadcast_to(jnp.exp(g_last), (SUBLANE, LANE)))

    for hl in range(hp):
        for direction in range(2):
            s_ref[hl * 2 + direction] = s0_ref[direction, hl]

    def phase_b(step):
        ids, rows, erows = [], [], []
        for hl in range(hp):
            for direction in range(2):
                chunk = step if direction == 0 else n_chunks - 1 - step
                ids.append(hl * 2 + direction)
                rows.append(pl.ds(_aligned(chunk * c_sz, c_sz), c_sz))
                erows.append(pl.ds(_aligned(chunk * SUBLANE, SUBLANE), SUBLANE))
        ss = [s_ref[i] for i in ids]
        s16s = [s.astype(BF16) for s in ss]
        wss = [_dot(w_ref[i, rw, :], s16) for i, rw, s16 in zip(ids, rows, s16s)]
        qss = [_dot(qg_ref[i, rw, :], s16) for i, rw, s16 in zip(ids, rows, s16s)]
        vns = [(u_ref[i, rw, :] - ws).astype(BF16) for i, rw, ws in zip(ids, rows, wss)]
        avs = [_dot(a_ref[i, rw, :], vn) for i, rw, vn in zip(ids, rows, vns)]
        kvs = [_dot_tn(kt_ref[i, rw, :], vn) for i, rw, vn in zip(ids, rows, vns)]
        for i, rw, er, s, qs_, av, kv in zip(ids, rows, erows, ss, qss, avs, kvs):
            od_ref[i, rw, :] = qs_ + av
            s_ref[i] = s * el_ref[i, er, :][0:1, :] + kv

    n_blocks = n_chunks // ca
    phase_a(0)

    def pipelined(it, carry):
        for s in range(ca):
            phase_b((it - 1) * ca + s)
        phase_a(it)
        return carry

    lax.fori_loop(1, n_blocks, pipelined, 0)
    for s in range(ca):
        phase_b((n_blocks - 1) * ca + s)

    for hl in range(hp):
        for direction in range(2):
            sfin_ref[direction, hl] = s_ref[hl * 2 + direction]

    for t in range(seq // rt):
        r0 = t * rt
        for hl in range(hp):
            hs = slice(hl * hd, (hl + 1) * hd)
            o = od_ref[hl * 2, r0:r0 + rt, :] + od_ref[hl * 2 + 1, r0:r0 + rt, :]
            o = o * lax.rsqrt(jnp.mean(o * o, axis=-1, keepdims=True) + EPS) * nw_ref[...]
            o_ref[r0:r0 + rt, hs] = (o * _silu(z_ref[r0:r0 + rt, hs].astype(F32))).astype(o_ref.dtype)


def _deltanet(proj, col, conv_w, norm_w, s0, *, batch, seq, n_heads, col_q, col_z):
    t = proj.shape[0]
    hd = DN_HEAD_DIM
    hp = 2
    d = n_heads * hd
    wd = hp * hd
    n_chunks = seq // DN_CHUNK
    ca = min(8, n_chunks)
    assert n_heads % hp == 0 and col_q % hp == 0 and col_z % hp == 0 and n_chunks % ca == 0
    blk = lambda off: pl.BlockSpec((seq, wd), lambda b, h: (b, off // hp + h))
    cw = lambda off: pl.BlockSpec((3, wd), lambda b, h: (0, off // hp + h))
    st = pl.BlockSpec((None, 2, hp, hd, hd), lambda b, h: (b, 0, h, 0, 0))
    nst = 2 * hp
    return pl.pallas_call(
        functools.partial(_dn_kernel, seq=seq, n_heads=n_heads, hp=hp, ca=ca),
        grid=(batch, n_heads // hp),
        in_specs=[blk(col_q), blk(col_q + n_heads), blk(col_q + 2 * n_heads), blk(col_z),
                  pl.BlockSpec((seq, LANE), lambda b, h: (b, 0)),
                  cw(0), cw(n_heads), cw(2 * n_heads),
                  pl.BlockSpec((1, hd), lambda b, h: (0, 0)),
                  st],
        out_specs=[pl.BlockSpec((seq, wd), lambda b, h: (b, h)), st],
        out_shape=[jax.ShapeDtypeStruct((t, d), BF16),
                   jax.ShapeDtypeStruct(s0.shape, F32)],
        scratch_shapes=[pltpu.VMEM((seq + 2 * SUBLANE, wd), F32),
                        pltpu.VMEM((seq, wd), F32), pltpu.VMEM((seq, wd), F32),
                        pltpu.VMEM((seq, wd), F32),
                        pltpu.VMEM((nst, seq, hd), F32),
                        pltpu.VMEM((nst, seq, hd), BF16),
                        pltpu.VMEM((nst, seq, DN_CHUNK), BF16),
                        pltpu.VMEM((nst, seq, hd), BF16),
                        pltpu.VMEM((nst, seq, hd), BF16),
                        pltpu.VMEM((nst, n_chunks * SUBLANE, LANE), F32),
                        pltpu.VMEM((nst, seq, hd), F32),
                        pltpu.VMEM((nst, hd, hd), F32)],
        compiler_params=_cparams(("parallel", "parallel")),
        name="deltanet",
    )(proj, proj, proj, proj, col, conv_w, conv_w, conv_w, norm_w.reshape(1, hd), s0)


def _conv_module_kernel(a_ref, g_ref, w_ref, b_ref, lnw_ref, lnb_ref, o_ref, yp_ref, tmp_ref,
                        sh_ref, sh2_ref, *, seq, taps):
    d = a_ref.shape[1]
    half = taps // 2
    pad = 2 * SUBLANE
    assert half < pad
    rt = CV_ROW_TILE
    ct = CV_COL_TILE
    zeros = jnp.zeros((pad, d), F32)
    yp_ref[0:pad, :] = zeros
    yp_ref[seq + pad:seq + 2 * pad, :] = zeros

    def fill(t, carry):
        r0 = pl.multiple_of(t * rt, rt)
        a = a_ref[pl.ds(r0, rt), :].astype(F32)
        g = g_ref[pl.ds(r0, rt), :].astype(F32)
        yp_ref[pl.ds(r0 + pad, rt), :] = a * jax.nn.sigmoid(g)
        return carry

    lax.fori_loop(0, seq // rt, fill, 0)

    n_tiles = seq // rt
    span = rt + 2 * pad - SUBLANE
    src_row = _iota((span, rt + 2 * pad), 1) - _iota((span, rt + 2 * pad), 0)

    def stage_shifts(t, dst_ref):
        r0 = _aligned(t * rt, rt)
        win16 = yp_ref[pl.ds(r0, rt + 2 * pad), :].astype(BF16)
        for m in range(1, SUBLANE):
            shift_m = jnp.where(src_row == m, 1.0, 0.0).astype(BF16)
            dst_ref[m - 1] = _dot(shift_m, win16)

    stage_shifts(0, sh_ref)

    def tile_pair(tp, carry):
        t0 = tp * 2
        stage_shifts(t0 + 1, sh2_ref)
        tile(t0, sh_ref)
        stage_shifts(jnp.minimum(t0 + 2, n_tiles - 1), sh_ref)
        tile(t0 + 1, sh2_ref)
        return carry

    def tile(t, src_ref):
        r0 = pl.multiple_of(t * rt, rt)
        for cc in range(d // ct):
            cs = slice(cc * ct, (cc + 1) * ct)
            acc = jnp.zeros((rt, ct), F32) + b_ref[:, cs]
            for j in range(taps):
                off = pad - half + j
                lo = (off // SUBLANE) * SUBLANE
                m = off - lo
                if m == 0:
                    rows = yp_ref[pl.ds(pl.multiple_of(r0 + lo, SUBLANE), rt), cs]
                else:
                    rows = src_ref[m - 1, lo:lo + rt, cs]
                acc = acc + w_ref[j:j + 1, cs] * rows
            tmp_ref[:, cs] = acc
        y = tmp_ref[...]
        mu = jnp.mean(y, axis=-1, keepdims=True)
        yc = y - mu
        var = jnp.mean(yc * yc, axis=-1, keepdims=True)
        y = yc * lax.rsqrt(var + EPS) * lnw_ref[...] + lnb_ref[...]
        o_ref[pl.ds(r0, rt), :] = _silu(y).astype(o_ref.dtype)

    assert n_tiles % 2 == 0
    lax.fori_loop(0, n_tiles // 2, tile_pair, 0)


def _conv_module(proj, w, b, lnw, lnb, *, batch, seq, col_a):
    t = proj.shape[0]
    taps, d = w.shape
    return pl.pallas_call(
        functools.partial(_conv_module_kernel, seq=seq, taps=taps),
        grid=(batch,),
        in_specs=[pl.BlockSpec((seq, d), lambda i: (i, col_a)),
                  pl.BlockSpec((seq, d), lambda i: (i, col_a + 1)),
                  pl.BlockSpec((taps, d), lambda i: (0, 0)),
                  pl.BlockSpec((1, d), lambda i: (0, 0)),
                  pl.BlockSpec((1, d), lambda i: (0, 0)),
                  pl.BlockSpec((1, d), lambda i: (0, 0))],
        out_specs=pl.BlockSpec((seq, d), lambda i: (i, 0)),
        out_shape=jax.ShapeDtypeStruct((t, d), BF16),
        scratch_shapes=[pltpu.VMEM((seq + 4 * SUBLANE, d), F32),
                        pltpu.VMEM((CV_ROW_TILE, d), F32),
                        pltpu.VMEM((SUBLANE - 1, CV_ROW_TILE + 3 * SUBLANE, d), F32),
                        pltpu.VMEM((SUBLANE - 1, CV_ROW_TILE + 3 * SUBLANE, d), F32)],
        compiler_params=_cparams(("parallel",)),
        name="conv_module",
    )(proj, proj, w, b.reshape(1, d), lnw.reshape(1, d), lnb.reshape(1, d))


def _merge_kernel(s5_ref, dn_ref, cv_ref, g0_ref, g1_ref, g2_ref, w0_ref, w1_ref, w2_ref,
                  wo_ref, x_ref, gate_ref, o_ref):
    merged = (jax.nn.sigmoid(g0_ref[...].astype(F32)) * _dot(s5_ref[...], w0_ref[...])
              + jax.nn.sigmoid(g1_ref[...].astype(F32)) * _dot(dn_ref[...], w1_ref[...])
              + jax.nn.sigmoid(g2_ref[...].astype(F32)) * _dot(cv_ref[...], w2_ref[...]))
    y = _dot(merged.astype(BF16), wo_ref[...])
    o_ref[...] = x_ref[...] + gate_ref[0] * y


def _merge(s5_out, dn_out, cv_out, proj, w_s5, w_dn, w_cv, w_out, x2, mod3, gate_idx,
           *, rows_per_batch, fixed_row, col_gate):
    t, d = x2.shape
    tm = min(512, t if fixed_row is not None else rows_per_batch)
    assert t % tm == 0
    row = _row_of_tile(tm, rows_per_batch, fixed_row)
    act = pl.BlockSpec((tm, d), lambda i: (i, 0))
    gat = lambda k: pl.BlockSpec((tm, d), lambda i: (i, col_gate + k))
    wsp = pl.BlockSpec((d, d), lambda i: (0, 0))
    return pl.pallas_call(
        _merge_kernel,
        grid=(t // tm,),
        in_specs=[act, act, act, gat(0), gat(1), gat(2), wsp, wsp, wsp, wsp, act,
                  pl.BlockSpec((None, 1, d), lambda i: (row(i), 0, gate_idx))],
        out_specs=act,
        out_shape=jax.ShapeDtypeStruct((t, d), F32),
        compiler_params=_cparams(("parallel",)),
        name="merge",
    )(s5_out, dn_out, cv_out, proj, proj, proj, w_s5, w_dn, w_cv, w_out, x2, mod3)


def _ffn_conv_kernel(a_ref, v_ref, w_ref, b_ref, o_ref, a0_ref, al_ref, ar_ref, *, seq, grid2d):
    ct = a_ref.shape[1]
    pad = 72 if grid2d else SUBLANE
    rt = 128
    zeros = jnp.zeros((pad, ct), F32)
    for buf in (a0_ref, al_ref, ar_ref):
        buf[0:pad, :] = zeros
        buf[seq + pad:seq + 2 * pad, :] = zeros

    def fill(t, carry):
        r0 = pl.multiple_of(t * rt, rt)
        a0_ref[pl.ds(r0 + pad, rt), :] = a_ref[pl.ds(r0, rt), :].astype(F32)
        return carry

    lax.fori_loop(0, seq // rt, fill, 0)

    def neighbours(t, carry):
        r0 = pl.multiple_of(t * rt, rt)
        left = _shifted_rows(a0_ref, r0 + pad, -1, rt, slice(None))
        right = _shifted_rows(a0_ref, r0 + pad, 1, rt, slice(None))
        if grid2d:
            colpos = (_iota((rt, ct), 0) + r0) & (GRID_W - 1)
            left = jnp.where(colpos == 0, 0.0, left)
            right = jnp.where(colpos == GRID_W - 1, 0.0, right)
        al_ref[pl.ds(r0 + pad, rt), :] = left
        ar_ref[pl.ds(r0 + pad, rt), :] = right
        return carry

    lax.fori_loop(0, seq // rt, neighbours, 0)

    if grid2d:
        taps = [(dr, dc) for dr in (-1, 0, 1) for dc in (-1, 0, 1)]
    else:
        taps = [(0, dc) for dc in (-1, 0, 1)]
    src = {-1: al_ref, 0: a0_ref, 1: ar_ref}

    def tile(t, carry):
        r0 = pl.multiple_of(t * rt, rt)
        acc = jnp.zeros((rt, ct), F32) + b_ref[...]
        for dr, dc in taps:
            widx = (dr + 1) * 3 + (dc + 1)
            rows = pl.ds(pl.multiple_of(r0 + pad + dr * GRID_W, SUBLANE), rt)
            acc = acc + w_ref[widx:widx + 1, :] * src[dc][rows, :]
        o_ref[pl.ds(r0, rt), :] = (_silu(acc) * v_ref[pl.ds(r0, rt), :].astype(F32)).astype(o_ref.dtype)
        return carry

    lax.fori_loop(0, seq // rt, tile, 0)


def _ffn_conv(av, w9, bias, *, batch, seq, grid2d):
    t = av.shape[0]
    f = w9.shape[1]
    ct = 256
    assert f % ct == 0 and seq % 128 == 0
    nct = f // ct
    pad = 72 if grid2d else SUBLANE
    return pl.pallas_call(
        functools.partial(_ffn_conv_kernel, seq=seq, grid2d=grid2d),
        grid=(batch, nct),
        in_specs=[pl.BlockSpec((seq, ct), lambda b, c: (b, c)),
                  pl.BlockSpec((seq, ct), lambda b, c: (b, nct + c)),
                  pl.BlockSpec((9, ct), lambda b, c: (0, c)),
                  pl.BlockSpec((1, ct), lambda b, c: (0, c))],
        out_specs=pl.BlockSpec((seq, ct), lambda b, c: (b, c)),
        out_shape=jax.ShapeDtypeStruct((t, f), BF16),
        scratch_shapes=[pltpu.VMEM((seq + 2 * pad, ct), F32)] * 3,
        compiler_params=_cparams(("parallel", "parallel")),
        name="ffn_conv",
    )(av, av, w9, bias.reshape(1, f))


def _resid_matmul_kernel(*refs, final_norm):
    if final_norm:
        a_ref, w_ref, x_ref, gate_ref, fw_ref, o_ref = refs
    else:
        a_ref, w_ref, x_ref, gate_ref, o_ref = refs
    y = x_ref[...] + gate_ref[0] * _dot(a_ref[...], w_ref[...])
    if final_norm:
        y = y * lax.rsqrt(jnp.mean(y * y, axis=-1, keepdims=True) + EPS) * fw_ref[...]
    o_ref[...] = y


def _resid_matmul(a, w, x2, mod3, gate_idx, *, rows_per_batch, fixed_row, final_w=None):
    t, d = x2.shape
    k = a.shape[1]
    tm = min(512, t if fixed_row is not None else rows_per_batch)
    assert t % tm == 0
    row = _row_of_tile(tm, rows_per_batch, fixed_row)
    final_norm = final_w is not None
    in_specs = [pl.BlockSpec((tm, k), lambda i: (i, 0)),
                pl.BlockSpec((k, d), lambda i: (0, 0)),
                pl.BlockSpec((tm, d), lambda i: (i, 0)),
                pl.BlockSpec((None, 1, d), lambda i: (row(i), 0, gate_idx))]
    args = [a, w, x2, mod3]
    if final_norm:
        in_specs.append(pl.BlockSpec((1, d), lambda i: (0, 0)))
        args.append(final_w.reshape(1, d))
    return pl.pallas_call(
        functools.partial(_resid_matmul_kernel, final_norm=final_norm),
        grid=(t // tm,),
        in_specs=in_specs,
        out_specs=pl.BlockSpec((tm, d), lambda i: (i, 0)),
        out_shape=jax.ShapeDtypeStruct((t, d), F32),
        compiler_params=_cparams(("parallel",)),
        name="resid_matmul",
    )(*args)


def _group_major(u2, batch, seq, groups):
    nk = seq // S5_CHUNK
    u5 = u2.reshape(batch, nk, S5_CHUNK, groups, S5_GROUP)
    return jnp.transpose(u5, (3, 1, 0, 2, 4)).reshape(groups, nk * batch, S5_CHUNK * S5_GROUP)


def _token_major(yg, batch, seq, groups):
    nk = seq // S5_CHUNK
    y5 = yg.reshape(groups, nk, batch, S5_CHUNK, S5_GROUP)
    return jnp.transpose(y5, (2, 1, 3, 0, 4)).reshape(batch * seq, groups * S5_GROUP)


def kernel(x, c, ctx, c_ctx, ada_w, ada_b, norm1_w, norm2_w, w_in, s5_a_re, s5_a_im, s5_log_dt, s5_b_re, s5_b_im, s5_c_re, s5_c_im, s5_d, s5_w_glu, dn_conv_w, dn_a_log, dn_dt_bias, dn_norm_w, cv_dw_w, cv_dw_b, cv_ln_w, cv_ln_b, w_br_s5, w_br_dn, w_br_cv, w_out, ffn_w_up, ffn_dw_w, ffn_dw_b, ffn_w_down, final_norm_w):
    batch, seq, d = x.shape
    ctx_len = ctx.shape[1]
    depth = ada_w.shape[0]
    groups = s5_a_re.shape[2]
    n_heads = dn_a_log.shape[2]
    dn_width = n_heads * DN_HEAD_DIM
    ffn_hidden = ffn_dw_b.shape[1]
    assert batch == SUBLANE and batch < MOD_ROWS and d == groups * S5_GROUP == dn_width
    assert d % 1024 == 0 and 4 * n_heads <= LANE

    col_qkv = d
    col_beta = col_qkv + 3 * dn_width
    n_state = col_beta + 4 * n_heads
    col_cv = n_state + dn_width
    col_gate = col_cv + 2 * d
    pk_q = d // LANE
    pk_z = (d + 3 * dn_width) // LANE
    pk_cv = (d + 4 * dn_width) // d
    pk_gate = pk_cv + 2

    cmat = jnp.zeros((MOD_ROWS, d), F32).at[:batch].set(c).at[batch].set(c_ctx)
    xl = x.reshape(batch * seq, d)
    xc = ctx.reshape(batch * ctx_len, d)

    for i in range(depth):
        last = i == depth - 1
        w_main = jnp.concatenate([w_in[i][:, :col_beta].astype(BF16),
                                  w_in[i][:, n_state:].astype(BF16)], axis=1)
        w_bd = jnp.pad(w_in[i][:, col_beta:n_state].astype(BF16),
                       ((0, 0), (0, LANE - 4 * n_heads)))
        aneg = jnp.pad(-jnp.exp(dn_a_log[i].reshape(1, -1)), ((0, 0), (2 * n_heads, LANE - 4 * n_heads)))
        dtb = jnp.pad(dn_dt_bias[i].reshape(1, -1), ((0, 0), (2 * n_heads, LANE - 4 * n_heads)))
        s5_par = _s5_params(s5_a_re[i], s5_a_im[i], s5_log_dt[i], s5_b_re[i], s5_b_im[i],
                            s5_c_re[i], s5_c_im[i])
        w_glu = s5_w_glu[i].astype(BF16)
        w_brs = w_br_s5[i].astype(BF16)
        w_brd = w_br_dn[i].astype(BF16)
        w_brc = w_br_cv[i].astype(BF16)
        w_o = w_out[i].astype(BF16)
        w_up = ffn_w_up[i].astype(BF16)
        w_dn = ffn_w_down[i].astype(BF16)
        w9 = ffn_dw_w[i].reshape(9, ffn_hidden)

        mod3 = _modulation(cmat, ada_w, ada_b, i).reshape(MOD_ROWS, 1, N_MOD * d)

        rows_c = dict(rows_per_batch=ctx_len, fixed_row=batch)
        rows_l = dict(rows_per_batch=seq, fixed_row=None)
        proj_c, bd_c = _norm_matmul(xc, norm1_w[i], mod3, 0, 1, w_main, wbd=w_bd, tn=2 * d,
                                    **rows_c)
        proj_l, bd_l = _norm_matmul(xl, norm1_w[i], mod3, 0, 1, w_main, wbd=w_bd, tn=2 * d,
                                    **rows_l)

        u_rows = jnp.concatenate([_group_major(proj_c[:, :d], batch, ctx_len, groups),
                                  _group_major(proj_l[:, :d], batch, seq, groups)], axis=1)
        rows_ctx = (ctx_len // S5_CHUNK) * batch
        segments = ((0, ctx_len // S5_CHUNK), (rows_ctx, seq // S5_CHUNK))
        y_rows = _s5_core(u_rows, *s5_par, segments=segments, batch=batch)

        zero_dn = jnp.zeros((batch, 2, n_heads, DN_HEAD_DIM, DN_HEAD_DIM), F32)
        dn_args = dict(batch=batch, n_heads=n_heads, col_q=pk_q, col_z=pk_z)
        dn_c, dn_state = _deltanet(proj_c, _dn_gates(bd_c, aneg, dtb, n_heads), dn_conv_w[i],
                                   dn_norm_w[i], zero_dn, seq=ctx_len, **dn_args)
        dn_l, _ = _deltanet(proj_l, _dn_gates(bd_l, aneg, dtb, n_heads), dn_conv_w[i],
                            dn_norm_w[i], dn_state, seq=seq, **dn_args)

        def finish(x2, proj, y, dn_out, length, rows, grid2d, final_w):
            s5_out = _s5_post(y, proj, s5_d[i], w_glu)
            cv_out = _conv_module(proj, cv_dw_w[i], cv_dw_b[i], cv_ln_w[i], cv_ln_b[i],
                                  batch=batch, seq=length, col_a=pk_cv)
            x2 = _merge(s5_out, dn_out, cv_out, proj, w_brs, w_brd, w_brc, w_o, x2, mod3, 2,
                        col_gate=pk_gate, **rows)
            av = _norm_matmul(x2, norm2_w[i], mod3, 3, 4, w_up, tn=w_up.shape[1] // 2, **rows)
            hidden = _ffn_conv(av, w9, ffn_dw_b[i], batch=batch, seq=length, grid2d=grid2d)
            return _resid_matmul(hidden, w_dn, x2, mod3, 5, final_w=final_w, **rows)

        if not last:
            xc = finish(xc, proj_c, _token_major(y_rows[:, :rows_ctx], batch, ctx_len, groups),
                        dn_c, ctx_len, rows_c, False, None)
        xl = finish(xl, proj_l, _token_major(y_rows[:, rows_ctx:], batch, seq, groups), dn_l, seq,
                    rows_l, True, final_norm_w if last else None)

    return xl.reshape(batch, seq, d)
```
